```python
import math
import jax, jax.numpy as jnp
from jax import lax
import numpy as np

D_MODEL = 4096
BATCH = 2
SEQ = 8192
DEPTH = 4

N_EVEN = (DEPTH + 1) // 2
N_ODD = DEPTH // 2
N_SUBLAYERS = 3
D_FF = 3 * D_MODEL // 2
EPS = 1e-6

MIX_WIDTH = D_MODEL
POOL_WIDTH = 3 * MIX_WIDTH // 4
POOL_WINDOWS = (2, 4, 8, 16)
N_POOL_GROUPS = len(POOL_WINDOWS)
POOL_GROUP = POOL_WIDTH // N_POOL_GROUPS
FOURIER_WIDTH = MIX_WIDTH - POOL_WIDTH
FOURIER_HEADS = 4
FOURIER_HEAD_DIM = FOURIER_WIDTH // FOURIER_HEADS

ATTN_HEADS = 16
ATTN_HEAD_DIM = D_MODEL // (2 * ATTN_HEADS)
ATTN_V_DIM = 2 * ATTN_HEAD_DIM
Q_BLOCK = 128

kernel_name = "hybrid_pool_fourier_diffattn_macaron_encoder"


def rms_norm(x, g):
    xf = x.astype(jnp.float32)
    y = xf * lax.rsqrt(jnp.mean(xf * xf, axis=-1, keepdims=True) + EPS)
    return (y * g.astype(jnp.float32)).astype(x.dtype)


def swiglu(h, w_gate, w_up, w_down):
    return (jax.nn.silu(h @ w_gate) * (h @ w_up)) @ w_down


def centred_window_mean(x, w):
    b, s, c = x.shape
    cs = jnp.concatenate([jnp.zeros((b, 1, c), x.dtype), jnp.cumsum(x, axis=1)], axis=1)
    t = jnp.arange(s)
    lo = jnp.clip(t - w // 2, 0, s)
    hi = jnp.clip(t - w // 2 + w, 0, s)
    total = jnp.take(cs, hi, axis=1) - jnp.take(cs, lo, axis=1)
    count = (hi - lo).astype(jnp.float32)
    return total / count[None, :, None]


def pool_mixer(xa, w_pool, pool_scale):
    b, s, _ = xa.shape
    xg = xa.reshape(b, s, N_POOL_GROUPS, POOL_GROUP)
    outs = []
    for g, w in enumerate(POOL_WINDOWS):
        xi = xg[:, :, g].astype(jnp.float32)
        outs.append(centred_window_mean(xi, w) - xi)
    pooled = jnp.stack(outs, axis=2).astype(xa.dtype)
    y = jnp.einsum('bsgc,gcd->bsgd', pooled, w_pool).reshape(b, s, POOL_WIDTH)
    return y * pool_scale


def fourier_mixer(xb, w_fourier):
    b, s, _ = xb.shape
    xh = xb.reshape(b, s, FOURIER_HEADS, FOURIER_HEAD_DIM).astype(jnp.float32)
    y = jnp.fft.fft2(xh, axes=(1, 3), norm='ortho').real
    y = y.reshape(b, s, FOURIER_WIDTH).astype(xb.dtype)
    return y @ w_fourier


def pool_fourier_mixer(h, w_in, w_pool, pool_scale, w_fourier, w_out):
    u = h @ w_in
    a = pool_mixer(u[..., :POOL_WIDTH], w_pool, pool_scale)
    f = fourier_mixer(u[..., POOL_WIDTH:], w_fourier)
    return jnp.concatenate([a, f], axis=-1) @ w_out


def alibi_slopes():
    h = jnp.arange(1, ATTN_HEADS + 1, dtype=jnp.float32)
    return jnp.exp2(-8.0 * h / ATTN_HEADS)


def diff_attention(h, w_qkv, w_o, lq1, lk1, lq2, lk2, g_sub, lambda_init):
    b, s, _ = h.shape
    qkv = h @ w_qkv
    q, k, v = jnp.split(qkv, 3, axis=-1)
    q = q.reshape(b, s, ATTN_HEADS, 2, ATTN_HEAD_DIM) * (ATTN_HEAD_DIM ** -0.5)
    k = k.reshape(b, s, ATTN_HEADS, 2, ATTN_HEAD_DIM)
    v = v.reshape(b, s, ATTN_HEADS, ATTN_V_DIM)
    f32 = jnp.float32
    lam = (jnp.exp(jnp.sum(lq1.astype(f32) * lk1.astype(f32)))
           - jnp.exp(jnp.sum(lq2.astype(f32) * lk2.astype(f32))) + lambda_init)
    slopes = alibi_slopes()
    key_pos = jnp.arange(s, dtype=f32)
    nb = s // Q_BLOCK
    q_blocks = q.reshape(b, nb, Q_BLOCK, ATTN_HEADS, 2, ATTN_HEAD_DIM).transpose(1, 0, 2, 3, 4, 5)
    starts = jnp.arange(nb, dtype=jnp.int32) * Q_BLOCK

    def block(args):
        qb, start = args
        qpos = (start + jnp.arange(Q_BLOCK, dtype=jnp.int32)).astype(f32)
        bias = -slopes[:, None, None] * jnp.abs(qpos[:, None] - key_pos[None, :])
        sc = jnp.einsum('bqhmd,bkhmd->bhmqk', qb, k, preferred_element_type=f32)
        p = jax.nn.softmax(sc + bias[None, :, None], axis=-1)
        wts = p[:, :, 0] - lam * p[:, :, 1]
        return jnp.einsum('bhqk,bkhe->bqhe', wts.astype(v.dtype), v)

    o = lax.map(block, (q_blocks, starts))
    o = o.transpose(1, 0, 2, 3, 4).reshape(b, s, ATTN_HEADS, ATTN_V_DIM)
    o = rms_norm(o, g_sub) * (1.0 - lambda_init)
    return o.reshape(b, s, ATTN_HEADS * ATTN_V_DIM) @ w_o


def setup_inputs(seed: int = 0) -> dict:
    key = jax.random.key(seed)
    ks = jax.random.split(key, 20)
    f32 = jnp.float32
    nrm = lambda k, shape, fan_in: jax.random.normal(k, shape, f32) * (fan_in ** -0.5)
    return {
        "x": jax.random.normal(ks[0], (BATCH, SEQ, D_MODEL), f32),
        "norm_pre": 1.0 + 0.05 * jax.random.normal(ks[1], (DEPTH, N_SUBLAYERS, D_MODEL), f32),
        "norm_post": 1.0 + 0.05 * jax.random.normal(ks[2], (DEPTH, N_SUBLAYERS, D_MODEL), f32),
        "w_ffn_gate": nrm(ks[3], (DEPTH, 2, D_MODEL, D_FF), D_MODEL),
        "w_ffn_up": nrm(ks[4], (DEPTH, 2, D_MODEL, D_FF), D_MODEL),
        "w_ffn_down": nrm(ks[5], (DEPTH, 2, D_FF, D_MODEL), D_FF),
        "w_mix_in": nrm(ks[6], (N_EVEN, D_MODEL, MIX_WIDTH), D_MODEL),
        "w_pool": nrm(ks[7], (N_EVEN, N_POOL_GROUPS, POOL_GROUP, POOL_GROUP), POOL_GROUP),
        "pool_scale": 1.0 + 0.1 * jax.random.normal(ks[8], (N_EVEN, POOL_WIDTH), f32),
        "w_fourier": nrm(ks[9], (N_EVEN, FOURIER_WIDTH, FOURIER_WIDTH), FOURIER_WIDTH),
        "w_mix_out": nrm(ks[10], (N_EVEN, MIX_WIDTH, D_MODEL), MIX_WIDTH),
        "w_qkv": nrm(ks[11], (N_ODD, D_MODEL, 3 * D_MODEL), D_MODEL),
        "w_attn_out": nrm(ks[12], (N_ODD, D_MODEL, D_MODEL), D_MODEL),
        "lambda_q1": 0.1 * jax.random.normal(ks[13], (N_ODD, ATTN_HEAD_DIM), f32),
        "lambda_k1": 0.1 * jax.random.normal(ks[14], (N_ODD, ATTN_HEAD_DIM), f32),
        "lambda_q2": 0.1 * jax.random.normal(ks[15], (N_ODD, ATTN_HEAD_DIM), f32),
        "lambda_k2": 0.1 * jax.random.normal(ks[16], (N_ODD, ATTN_HEAD_DIM), f32),
        "subln_gain": 1.0 + 0.05 * jax.random.normal(ks[17], (N_ODD, ATTN_V_DIM), f32),
    }


def reference(x, norm_pre, norm_post, w_ffn_gate, w_ffn_up, w_ffn_down, w_mix_in, w_pool,
              pool_scale, w_fourier, w_mix_out, w_qkv, w_attn_out, lambda_q1, lambda_k1,
              lambda_q2, lambda_k2, subln_gain):
    for l in range(DEPTH):
        h = rms_norm(x, norm_pre[l, 0])
        x = x + 0.5 * rms_norm(swiglu(h, w_ffn_gate[l, 0], w_ffn_up[l, 0], w_ffn_down[l, 0]), norm_post[l, 0])
        h = rms_norm(x, norm_pre[l, 1])
        i = l // 2
        if l % 2 == 0:
            m = pool_fourier_mixer(h, w_mix_in[i], w_pool[i], pool_scale[i], w_fourier[i], w_mix_out[i])
        else:
            lambda_init = 0.8 - 0.6 * math.exp(-0.3 * l)
            m = diff_attention(h, w_qkv[i], w_attn_out[i], lambda_q1[i], lambda_k1[i],
                               lambda_q2[i], lambda_k2[i], subln_gain[i], lambda_init)
        x = x + rms_norm(m, norm_post[l, 1])
        h = rms_norm(x, norm_pre[l, 2])
        x = x + 0.5 * rms_norm(swiglu(h, w_ffn_gate[l, 1], w_ffn_up[l, 1], w_ffn_down[l, 1]), norm_post[l, 2])
    return x
```

```python
import functools
import math

import jax
import jax.numpy as jnp
from jax import lax
from jax.experimental import pallas as pl
from jax.experimental.pallas import tpu as pltpu

_V7X_VMEM_LIMIT_BYTES = 56 * 1024 * 1024
_SUBLANES = 8

_EPS = 1e-6
_POOL_WINDOWS = (2, 4, 8, 16)
_FOURIER_HEADS = 4
_ATTN_HEADS = 16
_NEG_BIG = -1e30

_BF16 = jnp.bfloat16
_F32 = jnp.float32


def _compiler_params(*semantics):
    return pltpu.CompilerParams(dimension_semantics=semantics,
                                vmem_limit_bytes=_V7X_VMEM_LIMIT_BYTES)


def _tile(n, want):
    t = min(n, want)
    assert n % t == 0, (n, want)
    return t


def _rms(x, g):
    return x * lax.rsqrt(jnp.mean(x * x, axis=-1, keepdims=True) + _EPS) * g


def _prenorm_kernel(x_ref, g_ref, h_ref):
    h_ref[...] = _rms(x_ref[...], g_ref[...]).astype(h_ref.dtype)


def _prenorm(x, g):
    m, d = x.shape
    tm = _tile(m, 256)
    return pl.pallas_call(
        _prenorm_kernel,
        grid=(m // tm,),
        in_specs=[pl.BlockSpec((tm, d), lambda i: (i, 0)),
                  pl.BlockSpec((1, d), lambda i: (0, 0))],
        out_specs=pl.BlockSpec((tm, d), lambda i: (i, 0)),
        out_shape=jax.ShapeDtypeStruct((m, d), _BF16),
        compiler_params=_compiler_params("parallel"),
        name="prenorm",
    )(x, g.reshape(1, d))


def _residual_kernel(y_ref, x_ref, gpost_ref, gpre_ref, xo_ref, h_ref, *, step):
    xn = x_ref[...] + step * _rms(y_ref[...], gpost_ref[...])
    xo_ref[...] = xn
    h_ref[...] = _rms(xn, gpre_ref[...]).astype(h_ref.dtype)


def _residual_last_kernel(y_ref, x_ref, gpost_ref, xo_ref, *, step):
    xo_ref[...] = x_ref[...] + step * _rms(y_ref[...], gpost_ref[...])


def _residual(y, x, g_post, g_pre_next, step):
    m, d = x.shape
    tm = _tile(m, 256)
    row = pl.BlockSpec((tm, d), lambda i: (i, 0))
    vec = pl.BlockSpec((1, d), lambda i: (0, 0))
    if g_pre_next is None:
        return pl.pallas_call(
            functools.partial(_residual_last_kernel, step=step),
            grid=(m // tm,),
            in_specs=[row, row, vec],
            out_specs=row,
            out_shape=jax.ShapeDtypeStruct((m, d), _F32),
            compiler_params=_compiler_params("parallel"),
            name="residual_last",
        )(y, x, g_post.reshape(1, d)), None
    return pl.pallas_call(
        functools.partial(_residual_kernel, step=step),
        grid=(m // tm,),
        in_specs=[row, row, vec, vec],
        out_specs=[row, row],
        out_shape=[jax.ShapeDtypeStruct((m, d), _F32), jax.ShapeDtypeStruct((m, d), _BF16)],
        compiler_params=_compiler_params("parallel"),
        name="residual",
    )(y, x, g_post.reshape(1, d), g_pre_next.reshape(1, d))


def _mm_kernel(a_ref, w_ref, o_ref, *, scaled_col_blocks, scale):
    acc = jnp.dot(a_ref[...].astype(_BF16), w_ref[...], preferred_element_type=_F32)
    if scaled_col_blocks:
        acc = acc * jnp.where(pl.program_id(1) < scaled_col_blocks, scale, 1.0)
    o_ref[...] = acc.astype(o_ref.dtype)


def _mm(a, w, out_dtype, *, tm=1024, tn=1024, a_col_block=None, scaled_cols=0, scale=1.0):
    m = a.shape[0]
    k, n = w.shape
    tm, tn = _tile(m, tm), _tile(n, tn)
    a_col = 0
    if a_col_block is not None:
        a_col, width = a_col_block
        assert width == k
    else:
        assert a.shape[1] == k
    assert scaled_cols % tn == 0
    return pl.pallas_call(
        functools.partial(_mm_kernel, scaled_col_blocks=scaled_cols // tn, scale=scale),
        grid=(m // tm, n // tn),
        in_specs=[pl.BlockSpec((tm, k), lambda i, j: (i, a_col)),
                  pl.BlockSpec((k, tn), lambda i, j: (0, j))],
        out_specs=pl.BlockSpec((tm, tn), lambda i, j: (i, j)),
        out_shape=jax.ShapeDtypeStruct((m, n), out_dtype),
        compiler_params=_compiler_params("parallel", "arbitrary"),
        name="matmul",
    )(a, w)


def _gate_up_kernel(h_ref, wg_ref, wu_ref, o_ref):
    h = h_ref[...]
    g = jnp.dot(h, wg_ref[...], preferred_element_type=_F32)
    u = jnp.dot(h, wu_ref[...], preferred_element_type=_F32)
    o_ref[...] = (g * jax.nn.sigmoid(g) * u).astype(o_ref.dtype)


def _gate_up(h, wg, wu, *, tm=1024, tn=512):
    m, k = h.shape
    n = wg.shape[1]
    tm, tn = _tile(m, tm), _tile(n, tn)
    return pl.pallas_call(
        _gate_up_kernel,
        grid=(m // tm, n // tn),
        in_specs=[pl.BlockSpec((tm, k), lambda i, j: (i, 0)),
                  pl.BlockSpec((k, tn), lambda i, j: (0, j)),
                  pl.BlockSpec((k, tn), lambda i, j: (0, j))],
        out_specs=pl.BlockSpec((tm, tn), lambda i, j: (i, j)),
        out_shape=jax.ShapeDtypeStruct((m, n), _BF16),
        compiler_params=_compiler_params("parallel", "arbitrary"),
        name="gate_up",
    )(h, wg, wu)


def _mm_cat_kernel(a_ref, f_ref, w_ref, o_ref):
    ka = a_ref.shape[1]
    acc = jnp.dot(a_ref[...], w_ref[:ka, :], preferred_element_type=_F32)
    acc = acc + jnp.dot(f_ref[...], w_ref[ka:, :], preferred_element_type=_F32)
    o_ref[...] = acc.astype(o_ref.dtype)


def _mm_cat(a, f, w, *, tm=1024, tn=1024):
    m, ka = a.shape
    kf = f.shape[1]
    k, n = w.shape
    assert ka + kf == k
    tm, tn = _tile(m, tm), _tile(n, tn)
    return pl.pallas_call(
        _mm_cat_kernel,
        grid=(m // tm, n // tn),
        in_specs=[pl.BlockSpec((tm, ka), lambda i, j: (i, 0)),
                  pl.BlockSpec((tm, kf), lambda i, j: (i, 0)),
                  pl.BlockSpec((k, tn), lambda i, j: (0, j))],
        out_specs=pl.BlockSpec((tm, tn), lambda i, j: (i, j)),
        out_shape=jax.ShapeDtypeStruct((m, n), _F32),
        compiler_params=_compiler_params("parallel", "arbitrary"),
        name="matmul_cat",
    )(a, f, w)


_POOL_HALO = 8
_POOL_ROWS = 256


def _pool_kernel(prev_ref, x_ref, next_ref, o_ref, *, seq, group):
    ts = x_ref.shape[0]
    ext_rows = ts + 2 * _POOL_HALO
    t = (pl.program_id(0) * ts + lax.broadcasted_iota(jnp.int32, (ts, 1), 0)) % seq
    for g, w in enumerate(_POOL_WINDOWS):
        cols = slice(g * group, (g + 1) * group)
        x = x_ref[:, cols]
        ext = jnp.concatenate([prev_ref[:, cols], x, next_ref[:, cols]], axis=0)
        total = jnp.zeros_like(x)
        count = jnp.zeros((ts, 1), _F32)
        for d in range(-(w // 2), w - w // 2):
            valid = (t + d >= 0) & (t + d < seq)
            if d == 0:
                shifted = x
            else:
                shifted = pltpu.roll(ext, (-d) % ext_rows, 0)[_POOL_HALO:_POOL_HALO + ts]
            total = total + jnp.where(valid, shifted, 0.0)
            count = count + valid.astype(_F32)
        o_ref[:, cols] = (total / count - x).astype(o_ref.dtype)


def _pool(u, seq, pool_width):
    m = u.shape[0]
    group = pool_width // len(_POOL_WINDOWS)
    assert max(_POOL_WINDOWS) // 2 <= _POOL_HALO
    ts = _tile(seq, _POOL_ROWS)
    per = ts // _POOL_HALO
    last = m // _POOL_HALO - 1
    return pl.pallas_call(
        functools.partial(_pool_kernel, seq=seq, group=group),
        grid=(m // ts,),
        in_specs=[pl.BlockSpec((_POOL_HALO, pool_width), lambda i: (jnp.maximum(i * per - 1, 0), 0)),
                  pl.BlockSpec((ts, pool_width), lambda i: (i, 0)),
                  pl.BlockSpec((_POOL_HALO, pool_width), lambda i: (jnp.minimum((i + 1) * per, last), 0))],
        out_specs=pl.BlockSpec((ts, pool_width), lambda i: (i, 0)),
        out_shape=jax.ShapeDtypeStruct((m, pool_width), _BF16),
        compiler_params=_compiler_params("parallel"),
        name="pool",
    )(u, u, u)


def _pool_proj_kernel(p_ref, w_ref, s_ref, o_ref):
    acc = jnp.dot(p_ref[...], w_ref[...], preferred_element_type=_F32)
    o_ref[...] = (acc * s_ref[...]).astype(o_ref.dtype)


def _pool_proj(pooled, w_pool, pool_scale, *, tm=1024):
    m, width = pooled.shape
    groups, group, _ = w_pool.shape
    tm = _tile(m, tm)
    return pl.pallas_call(
        _pool_proj_kernel,
        grid=(m // tm, groups),
        in_specs=[pl.BlockSpec((tm, group), lambda i, g: (i, g)),
                  pl.BlockSpec((None, group, group), lambda i, g: (g, 0, 0)),
                  pl.BlockSpec((1, group), lambda i, g: (0, g))],
        out_specs=pl.BlockSpec((tm, group), lambda i, g: (i, g)),
        out_shape=jax.ShapeDtypeStruct((m, width), _BF16),
        compiler_params=_compiler_params("parallel", "arbitrary"),
        name="pool_proj",
    )(pooled, w_pool, pool_scale.reshape(1, width))


def _dft_tables(n):
    idx = (jnp.arange(n, dtype=jnp.int32)[:, None] * jnp.arange(n, dtype=jnp.int32)[None, :]) % n
    ang = idx.astype(_F32) * (2.0 * math.pi / n)
    return jnp.cos(ang), jnp.sin(ang)


def _seq_dft_kernel(c_ref, s_ref, xc_ref, xs_ref, o_ref, acc_ref, *, norm):
    k = pl.program_id(3)

    @pl.when(k == 0)
    def _():
        acc_ref[...] = jnp.zeros_like(acc_ref)

    acc_ref[...] += (jnp.dot(c_ref[...], xc_ref[...], preferred_element_type=_F32)
                     - jnp.dot(s_ref[...], xs_ref[...], preferred_element_type=_F32))

    @pl.when(k == pl.num_programs(3) - 1)
    def _():
        o_ref[...] = (acc_ref[...] * norm).astype(o_ref.dtype)


def _seq_dft(cos_s, sin_s, xcs, batch, seq, width, norm, *, tm=1024, tn=1024, tk=1024):
    tm, tn, tk = _tile(seq, tm), _tile(width, tn), _tile(seq, tk)
    nj = width // tn
    rows_k = seq // tk
    rows_m = seq // tm
    return pl.pallas_call(
        functools.partial(_seq_dft_kernel, norm=norm),
        grid=(batch, rows_m, nj, rows_k),
        in_specs=[pl.BlockSpec((tm, tk), lambda b, i, j, k: (i, k)),
                  pl.BlockSpec((tm, tk), lambda b, i, j, k: (i, k)),
                  pl.BlockSpec((tk, tn), lambda b, i, j, k: (b * rows_k + k, j)),
                  pl.BlockSpec((tk, tn), lambda b, i, j, k: (b * rows_k + k, nj + j))],
        out_specs=pl.BlockSpec((tm, tn), lambda b, i, j, k: (b * rows_m + i, j)),
        out_shape=jax.ShapeDtypeStruct((batch * seq, width), _BF16),
        scratch_shapes=[pltpu.VMEM((tm, tn), _F32)],
        compiler_params=_compiler_params("parallel", "parallel", "parallel", "arbitrary"),
        name="seq_dft",
    )(cos_s, sin_s, xcs, xcs)


def _diff_attn_kernel(slope_ref, rel_ref, q_ref, k_ref, v_ref, lq1_ref, lk1_ref, lq2_ref, lk2_ref,
                      gsub_ref, o_ref, m_ref, l_ref, acc_ref, *, lambda_init):
    h = pl.program_id(1)
    qi = pl.program_id(2)
    ki = pl.program_id(3)
    tq, dh2 = q_ref.shape
    tk = k_ref.shape[0]
    dh = dh2 // 2

    @pl.when(ki == 0)
    def _():
        m_ref[...] = jnp.full_like(m_ref, _NEG_BIG)
        l_ref[...] = jnp.zeros_like(l_ref)
        acc_ref[...] = jnp.zeros_like(acc_ref)

    offset = (qi * tq - ki * tk).astype(_F32)
    bias = jnp.abs(rel_ref[...] + offset) * (-slope_ref[h])
    v = v_ref[...]
    for half in range(2):
        cols = slice(half * dh, (half + 1) * dh)
        s = lax.dot_general(q_ref[:, cols], k_ref[:, cols], (((1,), (1,)), ((), ())),
                            preferred_element_type=_F32) + bias
        m_old = m_ref[half]
        m_new = jnp.maximum(m_old, jnp.max(s, axis=-1, keepdims=True))
        alpha = jnp.exp(m_old - m_new)
        p = jnp.exp(s - m_new)
        l_ref[half] = alpha * l_ref[half] + jnp.sum(p, axis=-1, keepdims=True)
        acc_ref[half] = alpha * acc_ref[half] + jnp.dot(p.astype(v.dtype), v, preferred_element_type=_F32)
        m_ref[half] = m_new

    @pl.when(ki == pl.num_programs(3) - 1)
    def _():
        lam = (jnp.exp(jnp.sum(lq1_ref[...] * lk1_ref[...], keepdims=True))
               - jnp.exp(jnp.sum(lq2_ref[...] * lk2_ref[...], keepdims=True)) + lambda_init)
        o = acc_ref[0] / l_ref[0] - lam * (acc_ref[1] / l_ref[1])
        o_ref[...] = (_rms(o, gsub_ref[...]) * (1.0 - lambda_init)).astype(o_ref.dtype)


def _diff_attn(qkv, batch, seq, lq1, lk1, lq2, lk2, g_sub, lambda_init, *, tq=512, tk=512):
    m, d3 = qkv.shape
    d = d3 // 3
    heads = _ATTN_HEADS
    dv = d // heads
    dh = dv // 2
    tq, tk = _tile(seq, tq), _tile(seq, tk)
    nq, nk = seq // tq, seq // tk
    slopes = jnp.exp2(-8.0 * jnp.arange(1, heads + 1, dtype=_F32) / heads)
    rel = (jnp.arange(tq, dtype=_F32)[:, None] - jnp.arange(tk, dtype=_F32)[None, :])
    lam_spec = pl.BlockSpec((1, dh), lambda b, h, i, k: (0, 0))
    return pl.pallas_call(
        functools.partial(_diff_attn_kernel, lambda_init=lambda_init),
        grid=(batch, heads, nq, nk),
        in_specs=[pl.BlockSpec(memory_space=pltpu.SMEM),
                  pl.BlockSpec((tq, tk), lambda b, h, i, k: (0, 0)),
                  pl.BlockSpec((tq, dv), lambda b, h, i, k: (b * nq + i, h)),
                  pl.BlockSpec((tk, dv), lambda b, h, i, k: (b * nk + k, heads + h)),
                  pl.BlockSpec((tk, dv), lambda b, h, i, k: (b * nk + k, 2 * heads + h)),
                  lam_spec, lam_spec, lam_spec, lam_spec,
                  pl.BlockSpec((1, dv), lambda b, h, i, k: (0, 0))],
        out_specs=pl.BlockSpec((tq, dv), lambda b, h, i, k: (b * nq + i, h)),
        out_shape=jax.ShapeDtypeStruct((m, d), _BF16),
        scratch_shapes=[pltpu.VMEM((2, tq, 1), _F32),
                        pltpu.VMEM((2, tq, 1), _F32),
                        pltpu.VMEM((2, tq, dv), _F32)],
        compiler_params=_compiler_params("parallel", "parallel", "parallel", "arbitrary"),
        name="diff_attn",
    )(slopes, rel, qkv, qkv, qkv, lq1.reshape(1, dh), lk1.reshape(1, dh), lq2.reshape(1, dh),
      lk2.reshape(1, dh), g_sub.reshape(1, dv))


def _ffn(h, wg, wu, wd):
    return _mm(_gate_up(h, wg, wu), wd, _F32, tm=1024, tn=512)


def _pool_fourier(h, batch, seq, w_in, w_pool, pool_scale, w_fourier, w_out, dft):
    cos_s, sin_s, chan_dft = dft
    pool_width = pool_scale.shape[0]
    four_width = w_fourier.shape[0]
    u = _mm(h, w_in, _F32)
    a = _pool_proj(_pool(u, seq, pool_width), w_pool, pool_scale)
    xcs = _mm(u, chan_dft, _BF16, a_col_block=(pool_width // four_width, four_width))
    head_dim = four_width // _FOURIER_HEADS
    f = _seq_dft(cos_s, sin_s, xcs, batch, seq, four_width, 1.0 / math.sqrt(seq * head_dim))
    f = _mm(f, w_fourier, _BF16)
    return _mm_cat(a, f, w_out)


def kernel(x, norm_pre, norm_post, w_ffn_gate, w_ffn_up, w_ffn_down, w_mix_in, w_pool, pool_scale,
           w_fourier, w_mix_out, w_qkv, w_attn_out, lambda_q1, lambda_k1, lambda_q2, lambda_k2,
           subln_gain):
    batch, seq, d = x.shape
    depth = norm_pre.shape[0]
    m = batch * seq
    x = x.reshape(m, d)
    bf = lambda w: w.astype(_BF16)

    four_width = w_fourier.shape[1]
    head_dim = four_width // _FOURIER_HEADS
    cos_s, sin_s = _dft_tables(seq)
    cos_c, sin_c = _dft_tables(head_dim)
    eye = jnp.eye(_FOURIER_HEADS, dtype=_F32)
    chan_dft = jnp.concatenate([jnp.kron(eye, cos_c), jnp.kron(eye, sin_c)], axis=1)
    dft = (bf(cos_s), bf(sin_s), bf(chan_dft))

    attn_head_dim = d // (2 * _ATTN_HEADS)

    h = _prenorm(x, norm_pre[0, 0])
    for l in range(depth):
        i = l // 2
        y = _ffn(h, bf(w_ffn_gate[l, 0]), bf(w_ffn_up[l, 0]), bf(w_ffn_down[l, 0]))
        x, h = _residual(y, x, norm_post[l, 0], norm_pre[l, 1], 0.5)
        if l % 2 == 0:
            y = _pool_fourier(h, batch, seq, bf(w_mix_in[i]), bf(w_pool[i]), pool_scale[i],
                              bf(w_fourier[i]), bf(w_mix_out[i]), dft)
        else:
            lambda_init = 0.8 - 0.6 * math.exp(-0.3 * l)
            qkv = _mm(h, bf(w_qkv[i]), _BF16, scaled_cols=d, scale=attn_head_dim ** -0.5)
            o = _diff_attn(qkv, batch, seq, lambda_q1[i], lambda_k1[i], lambda_q2[i], lambda_k2[i],
                           subln_gain[i], lambda_init)
            y = _mm(o, bf(w_attn_out[i]), _F32)
        x, h = _residual(y, x, norm_post[l, 1], norm_pre[l, 2], 1.0)
        y = _ffn(h, bf(w_ffn_gate[l, 1]), bf(w_ffn_up[l, 1]), bf(w_ffn_down[l, 1]))
        g_next = norm_pre[l + 1, 0] if l + 1 < depth else None
        x, h = _residual(y, x, norm_post[l, 2], g_next, 0.5)
    return x.reshape(batch, seq, d)
```

```python
import functools
import math

import jax
import jax.numpy as jnp
from jax import lax
from jax.experimental import pallas as pl
from jax.experimental.pallas import tpu as pltpu

_V7X_VMEM_LIMIT_BYTES = 56 * 1024 * 1024
_SUBLANES = 8

_EPS = 1e-6
_POOL_WINDOWS = (2, 4, 8, 16)
_FOURIER_HEADS = 4
_ATTN_HEADS = 16
_NEG_BIG = -1e30

_BF16 = jnp.bfloat16
_F32 = jnp.float32


def _compiler_params(*semantics):
    return pltpu.CompilerParams(dimension_semantics=semantics,
                                vmem_limit_bytes=_V7X_VMEM_LIMIT_BYTES)


def _tile(n, want):
    t = min(n, want)
    assert n % t == 0, (n, want)
    return t


def _rms(x, g):
    return x * lax.rsqrt(jnp.mean(x * x, axis=-1, keepdims=True) + _EPS) * g


def _prenorm_kernel(x_ref, g_ref, h_ref):
    h_ref[...] = _rms(x_ref[...], g_ref[...]).astype(h_ref.dtype)


def _prenorm(x, g):
    m, d = x.shape
    tm = _tile(m, 256)
    return pl.pallas_call(
        _prenorm_kernel,
        grid=(m // tm,),
        in_specs=[pl.BlockSpec((tm, d), lambda i: (i, 0)),
                  pl.BlockSpec((1, d), lambda i: (0, 0))],
        out_specs=pl.BlockSpec((tm, d), lambda i: (i, 0)),
        out_shape=jax.ShapeDtypeStruct((m, d), _BF16),
        compiler_params=_compiler_params("parallel"),
        name="prenorm",
    )(x, g.reshape(1, d))


def _residual_kernel(y_ref, x_ref, gpost_ref, gpre_ref, xo_ref, h_ref, *, step):
    xn = x_ref[...] + step * _rms(y_ref[...], gpost_ref[...])
    xo_ref[...] = xn
    h_ref[...] = _rms(xn, gpre_ref[...]).astype(h_ref.dtype)


def _residual_last_kernel(y_ref, x_ref, gpost_ref, xo_ref, *, step):
    xo_ref[...] = x_ref[...] + step * _rms(y_ref[...], gpost_ref[...])


def _residual(y, x, g_post, g_pre_next, step):
    m, d = x.shape
    tm = _tile(m, 256)
    row = pl.BlockSpec((tm, d), lambda i: (i, 0))
    vec = pl.BlockSpec((1, d), lambda i: (0, 0))
    if g_pre_next is None:
        return pl.pallas_call(
            functools.partial(_residual_last_kernel, step=step),
            grid=(m // tm,),
            in_specs=[row, row, vec],
            out_specs=row,
            out_shape=jax.ShapeDtypeStruct((m, d), _F32),
            compiler_params=_compiler_params("parallel"),
            name="residual_last",
        )(y, x, g_post.reshape(1, d)), None
    return pl.pallas_call(
        functools.partial(_residual_kernel, step=step),
        grid=(m // tm,),
        in_specs=[row, row, vec, vec],
        out_specs=[row, row],
        out_shape=[jax.ShapeDtypeStruct((m, d), _F32), jax.ShapeDtypeStruct((m, d), _BF16)],
        compiler_params=_compiler_params("parallel"),
        name="residual",
    )(y, x, g_post.reshape(1, d), g_pre_next.reshape(1, d))


def _mm_kernel(a_ref, w_ref, o_ref, *, scaled_col_blocks, scale):
    acc = jnp.dot(a_ref[...].astype(_BF16), w_ref[...], preferred_element_type=_F32)
    if scaled_col_blocks:
        acc = acc * jnp.where(pl.program_id(1) < scaled_col_blocks, scale, 1.0)
    o_ref[...] = acc.astype(o_ref.dtype)


def _mm(a, w, out_dtype, *, tm=1024, tn=1024, a_col_block=None, scaled_cols=0, scale=1.0):
    m = a.shape[0]
    k, n = w.shape
    tm, tn = _tile(m, tm), _tile(n, tn)
    a_col = 0
    if a_col_block is not None:
        a_col, width = a_col_block
        assert width == k
    else:
        assert a.shape[1] == k
    assert scaled_cols % tn == 0
    return pl.pallas_call(
        functools.partial(_mm_kernel, scaled_col_blocks=scaled_cols // tn, scale=scale),
        grid=(m // tm, n // tn),
        in_specs=[pl.BlockSpec((tm, k), lambda i, j: (i, a_col)),
                  pl.BlockSpec((k, tn), lambda i, j: (0, j))],
        out_specs=pl.BlockSpec((tm, tn), lambda i, j: (i, j)),
        out_shape=jax.ShapeDtypeStruct((m, n), out_dtype),
        compiler_params=_compiler_params("parallel", "arbitrary"),
        name="matmul",
    )(a, w)


def _gate_up_kernel(h_ref, wg_ref, wu_ref, o_ref):
    h = h_ref[...]
    g = jnp.dot(h, wg_ref[...], preferred_element_type=_F32)
    u = jnp.dot(h, wu_ref[...], preferred_element_type=_F32)
    o_ref[...] = (g * jax.nn.sigmoid(g) * u).astype(o_ref.dtype)


def _gate_up(h, wg, wu, *, tm=1024, tn=512):
    m, k = h.shape
    n = wg.shape[1]
    tm, tn = _tile(m, tm), _tile(n, tn)
    return pl.pallas_call(
        _gate_up_kernel,
        grid=(m // tm, n // tn),
        in_specs=[pl.BlockSpec((tm, k), lambda i, j: (i, 0)),
                  pl.BlockSpec((k, tn), lambda i, j: (0, j)),
                  pl.BlockSpec((k, tn), lambda i, j: (0, j))],
        out_specs=pl.BlockSpec((tm, tn), lambda i, j: (i, j)),
        out_shape=jax.ShapeDtypeStruct((m, n), _BF16),
        compiler_params=_compiler_params("parallel", "arbitrary"),
        name="gate_up",
    )(h, wg, wu)


def _mm_cat_kernel(a_ref, f_ref, w_ref, o_ref):
    ka = a_ref.shape[1]
    acc = jnp.dot(a_ref[...], w_ref[:ka, :], preferred_element_type=_F32)
    acc = acc + jnp.dot(f_ref[...], w_ref[ka:, :], preferred_element_type=_F32)
    o_ref[...] = acc.astype(o_ref.dtype)


def _mm_cat(a, f, w, *, tm=1024, tn=1024):
    m, ka = a.shape
    kf = f.shape[1]
    k, n = w.shape
    assert ka + kf == k
    tm, tn = _tile(m, tm), _tile(n, tn)
    return pl.pallas_call(
        _mm_cat_kernel,
        grid=(m // tm, n // tn),
        in_specs=[pl.BlockSpec((tm, ka), lambda i, j: (i, 0)),
                  pl.BlockSpec((tm, kf), lambda i, j: (i, 0)),
                  pl.BlockSpec((k, tn), lambda i, j: (0, j))],
        out_specs=pl.BlockSpec((tm, tn), lambda i, j: (i, j)),
        out_shape=jax.ShapeDtypeStruct((m, n), _F32),
        compiler_params=_compiler_params("parallel", "arbitrary"),
        name="matmul_cat",
    )(a, f, w)


_POOL_HALO = 8
_POOL_ROWS = 256


def _pool_kernel(prev_ref, x_ref, next_ref, o_ref, *, seq, group):
    ts = x_ref.shape[0]
    ext_rows = ts + 2 * _POOL_HALO
    t = (pl.program_id(0) * ts + lax.broadcasted_iota(jnp.int32, (ts, 1), 0)) % seq
    for g, w in enumerate(_POOL_WINDOWS):
        cols = slice(g * group, (g + 1) * group)
        x = x_ref[:, cols]
        ext = jnp.concatenate([prev_ref[:, cols], x, next_ref[:, cols]], axis=0)
        total = jnp.zeros_like(x)
        count = jnp.zeros((ts, 1), _F32)
        for d in range(-(w // 2), w - w // 2):
            valid = (t + d >= 0) & (t + d < seq)
            if d == 0:
                shifted = x
            else:
                shifted = pltpu.roll(ext, (-d) % ext_rows, 0)[_POOL_HALO:_POOL_HALO + ts]
            total = total + jnp.where(valid, shifted, 0.0)
            count = count + valid.astype(_F32)
        o_ref[:, cols] = (total / count - x).astype(o_ref.dtype)


def _pool(u, seq, pool_width):
    m = u.shape[0]
    group = pool_width // len(_POOL_WINDOWS)
    assert max(_POOL_WINDOWS) // 2 <= _POOL_HALO
    ts = _tile(seq, _POOL_ROWS)
    per = ts // _POOL_HALO
    last = m // _POOL_HALO - 1
    return pl.pallas_call(
        functools.partial(_pool_kernel, seq=seq, group=group),
        grid=(m // ts,),
        in_specs=[pl.BlockSpec((_POOL_HALO, pool_width), lambda i: (jnp.maximum(i * per - 1, 0), 0)),
                  pl.BlockSpec((ts, pool_width), lambda i: (i, 0)),
                  pl.BlockSpec((_POOL_HALO, pool_width), lambda i: (jnp.minimum((i + 1) * per, last), 0))],
        out_specs=pl.BlockSpec((ts, pool_width), lambda i: (i, 0)),
        out_shape=jax.ShapeDtypeStruct((m, pool_width), _BF16),
        compiler_params=_compiler_params("parallel"),
        name="pool",
    )(u, u, u)


def _pool_proj_kernel(p_ref, w_ref, s_ref, o_ref):
    acc = jnp.dot(p_ref[...], w_ref[...], preferred_element_type=_F32)
    o_ref[...] = (acc * s_ref[...]).astype(o_ref.dtype)


def _pool_proj(pooled, w_pool, pool_scale, *, tm=1024):
    m, width = pooled.shape
    groups, group, _ = w_pool.shape
    tm = _tile(m, tm)
    return pl.pallas_call(
        _pool_proj_kernel,
        grid=(m // tm, groups),
        in_specs=[pl.BlockSpec((tm, group), lambda i, g: (i, g)),
                  pl.BlockSpec((None, group, group), lambda i, g: (g, 0, 0)),
                  pl.BlockSpec((1, group), lambda i, g: (0, g))],
        out_specs=pl.BlockSpec((tm, group), lambda i, g: (i, g)),
        out_shape=jax.ShapeDtypeStruct((m, width), _BF16),
        compiler_params=_compiler_params("parallel", "arbitrary"),
        name="pool_proj",
    )(pooled, w_pool, pool_scale.reshape(1, width))


def _dft_tables(n):
    idx = (jnp.arange(n, dtype=jnp.int32)[:, None] * jnp.arange(n, dtype=jnp.int32)[None, :]) % n
    ang = idx.astype(_F32) * (2.0 * math.pi / n)
    return jnp.cos(ang), jnp.sin(ang)


def _seq_dft_kernel(c_ref, s_ref, xc_ref, xs_ref, o_ref, acc_ref, *, norm):
    k = pl.program_id(3)

    @pl.when(k == 0)
    def _():
        acc_ref[...] = jnp.zeros_like(acc_ref)

    acc_ref[...] += (jnp.dot(c_ref[...], xc_ref[...], preferred_element_type=_F32)
                     - jnp.dot(s_ref[...], xs_ref[...], preferred_element_type=_F32))

    @pl.when(k == pl.num_programs(3) - 1)
    def _():
        o_ref[...] = (acc_ref[...] * norm).astype(o_ref.dtype)


def _seq_dft(cos_s, sin_s, xcs, batch, seq, width, norm, *, tm=1024, tn=1024, tk=1024):
    tm, tn, tk = _tile(seq, tm), _tile(width, tn), _tile(seq, tk)
    nj = width // tn
    rows_k = seq // tk
    rows_m = seq // tm
    return pl.pallas_call(
        functools.partial(_seq_dft_kernel, norm=norm),
        grid=(batch, rows_m, nj, rows_k),
        in_specs=[pl.BlockSpec((tm, tk), lambda b, i, j, k: (i, k)),
                  pl.BlockSpec((tm, tk), lambda b, i, j, k: (i, k)),
                  pl.BlockSpec((tk, tn), lambda b, i, j, k: (b * rows_k + k, j)),
                  pl.BlockSpec((tk, tn), lambda b, i, j, k: (b * rows_k + k, nj + j))],
        out_specs=pl.BlockSpec((tm, tn), lambda b, i, j, k: (b * rows_m + i, j)),
        out_shape=jax.ShapeDtypeStruct((batch * seq, width), _BF16),
        scratch_shapes=[pltpu.VMEM((tm, tn), _F32)],
        compiler_params=_compiler_params("parallel", "parallel", "parallel", "arbitrary"),
        name="seq_dft",
    )(cos_s, sin_s, xcs, xcs)


_KEYS_BEFORE, _KEYS_AROUND, _KEYS_AFTER = -1, 0, 1
_LOG2E = math.log2(math.e)


def _diff_attn_kernel(slope_ref, rel_ref, q_ref, k_ref, v_ref, lq1_ref, lk1_ref, lq2_ref, lk2_ref,
                      gsub_ref, o_ref, e_ref, m_ref, l_ref, acc_ref, *, lambda_init, tk):
    h = pl.program_id(1)
    qi = pl.program_id(2)
    tq, dv = q_ref.shape
    dh = dv // 2
    n_chunks = k_ref.shape[0] // tk
    slope = slope_ref[h]

    @pl.when(qi == 0)
    def _():
        e_ref[...] = rel_ref[...] * slope

    m_ref[...] = jnp.full_like(m_ref, _NEG_BIG)
    l_ref[...] = jnp.zeros_like(l_ref)
    acc_ref[...] = jnp.zeros_like(acc_ref)

    def chunk(c, side):
        start = pl.multiple_of(c * tk, tk)
        shift = slope * (qi * tq - c * tk).astype(_F32)
        e = e_ref[...]
        if side == _KEYS_AROUND:
            bias = -jnp.abs(e + shift)
        v = v_ref[pl.ds(start, tk), :]
        for half in range(2):
            cols = pl.ds(half * dh, dh)
            s = lax.dot_general(q_ref[:, cols], k_ref[pl.ds(start, tk), cols],
                                (((1,), (1,)), ((), ())), preferred_element_type=_F32)
            if side == _KEYS_BEFORE:
                t, kappa = s - e, -shift
            elif side == _KEYS_AFTER:
                t, kappa = s + e, shift
            else:
                t, kappa = s + bias, 0.0
            m_old = m_ref[half]
            m_new = jnp.maximum(m_old, jnp.max(t, axis=-1, keepdims=True) + kappa)
            alpha = jnp.exp2(m_old - m_new)
            p = jnp.exp2(t - (m_new - kappa))
            l_ref[half] = alpha * l_ref[half] + jnp.sum(p, axis=-1, keepdims=True)
            acc_ref[half] = alpha * acc_ref[half] + jnp.dot(p.astype(v.dtype), v,
                                                            preferred_element_type=_F32)
            m_ref[half] = m_new

    def walk(side):
        def body(c, carry):
            chunk(c, side)
            return carry
        return body

    c_own = (qi * tq) // tk
    lax.fori_loop(0, c_own, walk(_KEYS_BEFORE), 0)
    chunk(c_own, _KEYS_AROUND)
    lax.fori_loop(c_own + 1, n_chunks, walk(_KEYS_AFTER), 0)

    lam = (jnp.exp(jnp.sum(lq1_ref[...] * lk1_ref[...], keepdims=True))
           - jnp.exp(jnp.sum(lq2_ref[...] * lk2_ref[...], keepdims=True)) + lambda_init)
    o = acc_ref[0] / l_ref[0] - lam * (acc_ref[1] / l_ref[1])
    o_ref[...] = (_rms(o, gsub_ref[...]) * (1.0 - lambda_init)).astype(o_ref.dtype)


def _diff_attn(qkv, batch, seq, lq1, lk1, lq2, lk2, g_sub, lambda_init, *, tq=512, tk=1024):
    m, d3 = qkv.shape
    d = d3 // 3
    heads = _ATTN_HEADS
    dv = d // heads
    dh = dv // 2
    tq, tk = _tile(seq, tq), _tile(seq, tk)
    assert tk % tq == 0
    nq = seq // tq
    slopes = jnp.exp2(-8.0 * jnp.arange(1, heads + 1, dtype=_F32) / heads) * _LOG2E
    rel = (jnp.arange(tq, dtype=_F32)[:, None] - jnp.arange(tk, dtype=_F32)[None, :])
    lam_spec = pl.BlockSpec((1, dh), lambda b, h, i: (0, 0))
    return pl.pallas_call(
        functools.partial(_diff_attn_kernel, lambda_init=lambda_init, tk=tk),
        grid=(batch, heads, nq),
        in_specs=[pl.BlockSpec(memory_space=pltpu.SMEM),
                  pl.BlockSpec((tq, tk), lambda b, h, i: (0, 0)),
                  pl.BlockSpec((tq, dv), lambda b, h, i: (b * nq + i, h)),
                  pl.BlockSpec((seq, dv), lambda b, h, i: (b, heads + h)),
                  pl.BlockSpec((seq, dv), lambda b, h, i: (b, 2 * heads + h)),
                  lam_spec, lam_spec, lam_spec, lam_spec,
                  pl.BlockSpec((1, dv), lambda b, h, i: (0, 0))],
        out_specs=pl.BlockSpec((tq, dv), lambda b, h, i: (b * nq + i, h)),
        out_shape=jax.ShapeDtypeStruct((m, d), _BF16),
        scratch_shapes=[pltpu.VMEM((tq, tk), _F32),
                        pltpu.VMEM((2, tq, 1), _F32),
                        pltpu.VMEM((2, tq, 1), _F32),
                        pltpu.VMEM((2, tq, dv), _F32)],
        compiler_params=_compiler_params("parallel", "parallel", "arbitrary"),
        name="diff_attn",
    )(slopes, rel, qkv, qkv, qkv, lq1.reshape(1, dh), lk1.reshape(1, dh), lq2.reshape(1, dh),
      lk2.reshape(1, dh), g_sub.reshape(1, dv))


def _ffn(h, wg, wu, wd):
    return _mm(_gate_up(h, wg, wu), wd, _F32, tm=1024, tn=512)


def _pool_fourier(h, batch, seq, w_in, w_pool, pool_scale, w_fourier, w_out, dft):
    cos_s, sin_s, chan_dft = dft
    pool_width = pool_scale.shape[0]
    four_width = w_fourier.shape[0]
    u = _mm(h, w_in, _F32)
    a = _pool_proj(_pool(u, seq, pool_width), w_pool, pool_scale)
    xcs = _mm(u, chan_dft, _BF16, a_col_block=(pool_width // four_width, four_width))
    head_dim = four_width // _FOURIER_HEADS
    f = _seq_dft(cos_s, sin_s, xcs, batch, seq, four_width, 1.0 / math.sqrt(seq * head_dim))
    f = _mm(f, w_fourier, _BF16)
    return _mm_cat(a, f, w_out)


def kernel(x, norm_pre, norm_post, w_ffn_gate, w_ffn_up, w_ffn_down, w_mix_in, w_pool, pool_scale,
           w_fourier, w_mix_out, w_qkv, w_attn_out, lambda_q1, lambda_k1, lambda_q2, lambda_k2,
           subln_gain):
    batch, seq, d = x.shape
    depth = norm_pre.shape[0]
    m = batch * seq
    x = x.reshape(m, d)
    bf = lambda w: w.astype(_BF16)

    four_width = w_fourier.shape[1]
    head_dim = four_width // _FOURIER_HEADS
    cos_s, sin_s = _dft_tables(seq)
    cos_c, sin_c = _dft_tables(head_dim)
    eye = jnp.eye(_FOURIER_HEADS, dtype=_F32)
    chan_dft = jnp.concatenate([jnp.kron(eye, cos_c), jnp.kron(eye, sin_c)], axis=1)
    dft = (bf(cos_s), bf(sin_s), bf(chan_dft))

    attn_head_dim = d // (2 * _ATTN_HEADS)

    h = _prenorm(x, norm_pre[0, 0])
    for l in range(depth):
        i = l // 2
        y = _ffn(h, bf(w_ffn_gate[l, 0]), bf(w_ffn_up[l, 0]), bf(w_ffn_down[l, 0]))
        x, h = _residual(y, x, norm_post[l, 0], norm_pre[l, 1], 0.5)
        if l % 2 == 0:
            y = _pool_fourier(h, batch, seq, bf(w_mix_in[i]), bf(w_pool[i]), pool_scale[i],
                              bf(w_fourier[i]), bf(w_mix_out[i]), dft)
        else:
            lambda_init = 0.8 - 0.6 * math.exp(-0.3 * l)
            qkv = _mm(h, bf(w_qkv[i]), _BF16, scaled_cols=d, scale=attn_head_dim ** -0.5 * _LOG2E)
            o = _diff_attn(qkv, batch, seq, lambda_q1[i], lambda_k1[i], lambda_q2[i], lambda_k2[i],
                           subln_gain[i], lambda_init)
            y = _mm(o, bf(w_attn_out[i]), _F32)
        x, h = _residual(y, x, norm_post[l, 1], norm_pre[l, 2], 1.0)
        y = _ffn(h, bf(w_ffn_gate[l, 1]), bf(w_ffn_up[l, 1]), bf(w_ffn_down[l, 1]))
        g_next = norm_pre[l + 1, 0] if l + 1 < depth else None
        x, h = _residual(y, x, norm_post[l, 2], g_next, 0.5)
    return x.reshape(batch, seq, d)
```

```python
import functools
import math

import jax
import jax.numpy as jnp
from jax import lax
from jax.experimental import pallas as pl
from jax.experimental.pallas import tpu as pltpu

_V7X_VMEM_LIMIT_BYTES = 56 * 1024 * 1024
_SUBLANES = 8
_LANES = 128

_EPS = 1e-6
_POOL_WINDOWS = (2, 4, 8, 16)
_FOURIER_HEADS = 4
_ATTN_HEADS = 16
_NEG_BIG = -1e30

_BF16 = jnp.bfloat16
_F32 = jnp.float32


def _compiler_params(*semantics):
    return pltpu.CompilerParams(dimension_semantics=semantics,
                                vmem_limit_bytes=_V7X_VMEM_LIMIT_BYTES)


def _tile(n, want):
    t = min(n, want)
    assert n % t == 0, (n, want)
    return t


def _rms(x, g):
    return x * lax.rsqrt(jnp.mean(x * x, axis=-1, keepdims=True) + _EPS) * g


def _prenorm_kernel(x_ref, g_ref, h_ref):
    h_ref[...] = _rms(x_ref[...], g_ref[...]).astype(h_ref.dtype)


def _prenorm(x, g):
    m, d = x.shape
    tm = _tile(m, 256)
    return pl.pallas_call(
        _prenorm_kernel,
        grid=(m // tm,),
        in_specs=[pl.BlockSpec((tm, d), lambda i: (i, 0)),
                  pl.BlockSpec((1, d), lambda i: (0, 0))],
        out_specs=pl.BlockSpec((tm, d), lambda i: (i, 0)),
        out_shape=jax.ShapeDtypeStruct((m, d), _BF16),
        compiler_params=_compiler_params("parallel"),
        name="prenorm",
    )(x, g.reshape(1, d))


def _residual_kernel(y_ref, x_ref, gpost_ref, gpre_ref, xo_ref, h_ref, *, step):
    xn = x_ref[...] + step * _rms(y_ref[...], gpost_ref[...])
    xo_ref[...] = xn
    h_ref[...] = _rms(xn, gpre_ref[...]).astype(h_ref.dtype)


def _residual_last_kernel(y_ref, x_ref, gpost_ref, xo_ref, *, step):
    xo_ref[...] = x_ref[...] + step * _rms(y_ref[...], gpost_ref[...])


def _residual(y, x, g_post, g_pre_next, step):
    m, d = x.shape
    tm = _tile(m, 256)
    row = pl.BlockSpec((tm, d), lambda i: (i, 0))
    vec = pl.BlockSpec((1, d), lambda i: (0, 0))
    if g_pre_next is None:
        return pl.pallas_call(
            functools.partial(_residual_last_kernel, step=step),
            grid=(m // tm,),
            in_specs=[row, row, vec],
            out_specs=row,
            out_shape=jax.ShapeDtypeStruct((m, d), _F32),
            compiler_params=_compiler_params("parallel"),
            name="residual_last",
        )(y, x, g_post.reshape(1, d)), None
    return pl.pallas_call(
        functools.partial(_residual_kernel, step=step),
        grid=(m // tm,),
        in_specs=[row, row, vec, vec],
        out_specs=[row, row],
        out_shape=[jax.ShapeDtypeStruct((m, d), _F32), jax.ShapeDtypeStruct((m, d), _BF16)],
        compiler_params=_compiler_params("parallel"),
        name="residual",
    )(y, x, g_post.reshape(1, d), g_pre_next.reshape(1, d))


def _mm_kernel(a_ref, w_ref, o_ref, *, scaled_col_blocks, scale):
    acc = jnp.dot(a_ref[...].astype(_BF16), w_ref[...], preferred_element_type=_F32)
    if scaled_col_blocks:
        acc = acc * jnp.where(pl.program_id(1) < scaled_col_blocks, scale, 1.0)
    o_ref[...] = acc.astype(o_ref.dtype)


def _mm(a, w, out_dtype, *, tm=1024, tn=1024, a_col_block=None, scaled_cols=0, scale=1.0):
    m = a.shape[0]
    k, n = w.shape
    tm, tn = _tile(m, tm), _tile(n, tn)
    a_col = 0
    if a_col_block is not None:
        a_col, width = a_col_block
        assert width == k
    else:
        assert a.shape[1] == k
    assert scaled_cols % tn == 0
    return pl.pallas_call(
        functools.partial(_mm_kernel, scaled_col_blocks=scaled_cols // tn, scale=scale),
        grid=(m // tm, n // tn),
        in_specs=[pl.BlockSpec((tm, k), lambda i, j: (i, a_col)),
                  pl.BlockSpec((k, tn), lambda i, j: (0, j))],
        out_specs=pl.BlockSpec((tm, tn), lambda i, j: (i, j)),
        out_shape=jax.ShapeDtypeStruct((m, n), out_dtype),
        compiler_params=_compiler_params("parallel", "arbitrary"),
        name="matmul",
    )(a, w)


def _gate_up_kernel(h_ref, wg_ref, wu_ref, o_ref):
    h = h_ref[...]
    g = jnp.dot(h, wg_ref[...], preferred_element_type=_F32)
    u = jnp.dot(h, wu_ref[...], preferred_element_type=_F32)
    o_ref[...] = (g * jax.nn.sigmoid(g) * u).astype(o_ref.dtype)


def _gate_up(h, wg, wu, *, tm=1024, tn=512):
    m, k = h.shape
    n = wg.shape[1]
    tm, tn = _tile(m, tm), _tile(n, tn)
    return pl.pallas_call(
        _gate_up_kernel,
        grid=(m // tm, n // tn),
        in_specs=[pl.BlockSpec((tm, k), lambda i, j: (i, 0)),
                  pl.BlockSpec((k, tn), lambda i, j: (0, j)),
                  pl.BlockSpec((k, tn), lambda i, j: (0, j))],
        out_specs=pl.BlockSpec((tm, tn), lambda i, j: (i, j)),
        out_shape=jax.ShapeDtypeStruct((m, n), _BF16),
        compiler_params=_compiler_params("parallel", "arbitrary"),
        name="gate_up",
    )(h, wg, wu)


def _mm_cat_kernel(a_ref, f_ref, w_ref, o_ref):
    ka = a_ref.shape[1]
    acc = jnp.dot(a_ref[...], w_ref[:ka, :], preferred_element_type=_F32)
    acc = acc + jnp.dot(f_ref[...], w_ref[ka:, :], preferred_element_type=_F32)
    o_ref[...] = acc.astype(o_ref.dtype)


def _mm_cat(a, f, w, *, tm=1024, tn=1024):
    m, ka = a.shape
    kf = f.shape[1]
    k, n = w.shape
    assert ka + kf == k
    tm, tn = _tile(m, tm), _tile(n, tn)
    return pl.pallas_call(
        _mm_cat_kernel,
        grid=(m // tm, n // tn),
        in_specs=[pl.BlockSpec((tm, ka), lambda i, j: (i, 0)),
                  pl.BlockSpec((tm, kf), lambda i, j: (i, 0)),
                  pl.BlockSpec((k, tn), lambda i, j: (0, j))],
        out_specs=pl.BlockSpec((tm, tn), lambda i, j: (i, j)),
        out_shape=jax.ShapeDtypeStruct((m, n), _F32),
        compiler_params=_compiler_params("parallel", "arbitrary"),
        name="matmul_cat",
    )(a, f, w)


_POOL_HALO = 8
_POOL_ROWS = 256


def _pool_kernel(prev_ref, x_ref, next_ref, o_ref, *, seq, group):
    ts = x_ref.shape[0]
    ext_rows = ts + 2 * _POOL_HALO
    t = (pl.program_id(0) * ts + lax.broadcasted_iota(jnp.int32, (ts, 1), 0)) % seq
    for g, w in enumerate(_POOL_WINDOWS):
        cols = slice(g * group, (g + 1) * group)
        x = x_ref[:, cols]
        ext = jnp.concatenate([prev_ref[:, cols], x, next_ref[:, cols]], axis=0)
        total = jnp.zeros_like(x)
        count = jnp.zeros((ts, 1), _F32)
        for d in range(-(w // 2), w - w // 2):
            valid = (t + d >= 0) & (t + d < seq)
            if d == 0:
                shifted = x
            else:
                shifted = pltpu.roll(ext, (-d) % ext_rows, 0)[_POOL_HALO:_POOL_HALO + ts]
            total = total + jnp.where(valid, shifted, 0.0)
            count = count + valid.astype(_F32)
        o_ref[:, cols] = (total / count - x).astype(o_ref.dtype)


def _pool(u, seq, pool_width):
    m = u.shape[0]
    group = pool_width // len(_POOL_WINDOWS)
    assert max(_POOL_WINDOWS) // 2 <= _POOL_HALO
    ts = _tile(seq, _POOL_ROWS)
    per = ts // _POOL_HALO
    last = m // _POOL_HALO - 1
    return pl.pallas_call(
        functools.partial(_pool_kernel, seq=seq, group=group),
        grid=(m // ts,),
        in_specs=[pl.BlockSpec((_POOL_HALO, pool_width), lambda i: (jnp.maximum(i * per - 1, 0), 0)),
                  pl.BlockSpec((ts, pool_width), lambda i: (i, 0)),
                  pl.BlockSpec((_POOL_HALO, pool_width), lambda i: (jnp.minimum((i + 1) * per, last), 0))],
        out_specs=pl.BlockSpec((ts, pool_width), lambda i: (i, 0)),
        out_shape=jax.ShapeDtypeStruct((m, pool_width), _BF16),
        compiler_params=_compiler_params("parallel"),
        name="pool",
    )(u, u, u)


def _pool_proj_kernel(p_ref, w_ref, s_ref, o_ref):
    acc = jnp.dot(p_ref[...], w_ref[...], preferred_element_type=_F32)
    o_ref[...] = (acc * s_ref[...]).astype(o_ref.dtype)


def _pool_proj(pooled, w_pool, pool_scale, *, tm=1024):
    m, width = pooled.shape
    groups, group, _ = w_pool.shape
    tm = _tile(m, tm)
    return pl.pallas_call(
        _pool_proj_kernel,
        grid=(m // tm, groups),
        in_specs=[pl.BlockSpec((tm, group), lambda i, g: (i, g)),
                  pl.BlockSpec((None, group, group), lambda i, g: (g, 0, 0)),
                  pl.BlockSpec((1, group), lambda i, g: (0, g))],
        out_specs=pl.BlockSpec((tm, group), lambda i, g: (i, g)),
        out_shape=jax.ShapeDtypeStruct((m, width), _BF16),
        compiler_params=_compiler_params("parallel", "arbitrary"),
        name="pool_proj",
    )(pooled, w_pool, pool_scale.reshape(1, width))


def _dft_tables(n):
    idx = (jnp.arange(n, dtype=jnp.int32)[:, None] * jnp.arange(n, dtype=jnp.int32)[None, :]) % n
    ang = idx.astype(_F32) * (2.0 * math.pi / n)
    return jnp.cos(ang), jnp.sin(ang)


def _seq_dft_kernel(c_ref, s_ref, xc_ref, xs_ref, o_ref, acc_ref, *, norm):
    k = pl.program_id(3)

    @pl.when(k == 0)
    def _():
        acc_ref[...] = jnp.zeros_like(acc_ref)

    acc_ref[...] += (jnp.dot(c_ref[...], xc_ref[...], preferred_element_type=_F32)
                     - jnp.dot(s_ref[...], xs_ref[...], preferred_element_type=_F32))

    @pl.when(k == pl.num_programs(3) - 1)
    def _():
        o_ref[...] = (acc_ref[...] * norm).astype(o_ref.dtype)


def _seq_dft(cos_s, sin_s, xcs, batch, seq, width, norm, *, tm=1024, tn=1024, tk=1024):
    tm, tn, tk = _tile(seq, tm), _tile(width, tn), _tile(seq, tk)
    nj = width // tn
    rows_k = seq // tk
    rows_m = seq // tm
    return pl.pallas_call(
        functools.partial(_seq_dft_kernel, norm=norm),
        grid=(batch, rows_m, nj, rows_k),
        in_specs=[pl.BlockSpec((tm, tk), lambda b, i, j, k: (i, k)),
                  pl.BlockSpec((tm, tk), lambda b, i, j, k: (i, k)),
                  pl.BlockSpec((tk, tn), lambda b, i, j, k: (b * rows_k + k, j)),
                  pl.BlockSpec((tk, tn), lambda b, i, j, k: (b * rows_k + k, nj + j))],
        out_specs=pl.BlockSpec((tm, tn), lambda b, i, j, k: (b * rows_m + i, j)),
        out_shape=jax.ShapeDtypeStruct((batch * seq, width), _BF16),
        scratch_shapes=[pltpu.VMEM((tm, tn), _F32)],
        compiler_params=_compiler_params("parallel", "parallel", "parallel", "arbitrary"),
        name="seq_dft",
    )(cos_s, sin_s, xcs, xcs)


_LOG2E = math.log2(math.e)
_NT_DIMS = (((1,), (1,)), ((), ()))


def _diff_attn_kernel(slope_ref, q_ref, k_ref, v_ref, lq1_ref, lk1_ref, lq2_ref, lk2_ref, gsub_ref,
                      o_ref, s_ref, p_ref, alpha_ref, rq_ref, rt_ref, sub_ref, mx_ref, m_ref, l_ref,
                      acc_ref, *, lambda_init, tk):
    h = pl.program_id(1)
    qi = pl.program_id(2)
    tq, dv = q_ref.shape
    dh = dv // 2
    n_chunks = k_ref.shape[0] // tk
    n_col = tk // _LANES
    n_row = tq // _SUBLANES
    assert n_chunks % 2 == 0
    slope = slope_ref[h]
    c_own = (qi * tq) // tk

    rq_ref[...] = slope * lax.broadcasted_iota(jnp.int32, (tq, _LANES), 0).astype(_F32)
    ramp_k = [slope * (lax.broadcasted_iota(jnp.int32, (_SUBLANES, _LANES), 1) + j * _LANES).astype(_F32)
              for j in range(n_col)]

    m_ref[...] = jnp.full_like(m_ref, _NEG_BIG)
    l_ref[...] = jnp.zeros_like(l_ref)
    acc_ref[...] = jnp.zeros_like(acc_ref)

    def chunk_of(j):
        rest = jnp.where(j - 1 >= c_own, j, j - 1)
        return jnp.where(j == 0, c_own, rest)

    def scores(c, slot):
        start = pl.multiple_of(c * tk, tk)
        for half in range(2):
            cols = pl.ds(half * dh, dh)
            s_ref[slot, half] = lax.dot_general(q_ref[:, cols], k_ref[pl.ds(start, tk), cols],
                                                _NT_DIMS, preferred_element_type=_F32)

    def values(c, slot):
        start = pl.multiple_of(c * tk, tk)
        v = v_ref[pl.ds(start, tk), :]
        for half in range(2):
            alpha = alpha_ref[slot, half]
            acc_ref[half] = (jnp.concatenate([alpha] * (dv // _LANES), axis=1) * acc_ref[half]
                             + jnp.dot(p_ref[slot, half], v, preferred_element_type=_F32))

    def softmax(c, slot, own):
        shift = slope * (qi * tq - c * tk).astype(_F32)
        if own:
            rt_ref[...] = rq_ref[...] + shift
            col = ramp_k
        else:
            sign = jnp.where(c < c_own, 1.0, -1.0).astype(_F32)
            rt_ref[...] = sign * (rq_ref[...] + shift)
            col = [sign * r for r in ramp_k]

        def biased(half, g, j, row_term):
            s = s_ref[slot, half, g * _SUBLANES:(g + 1) * _SUBLANES, j * _LANES:(j + 1) * _LANES]
            if own:
                return s - jnp.abs(row_term - col[j])
            return s + col[j]

        for half in range(2):
            for g in range(n_row):
                rows = slice(g * _SUBLANES, (g + 1) * _SUBLANES)
                row_term = rt_ref[rows] if own else None
                mx = biased(half, g, 0, row_term)
                for j in range(1, n_col):
                    mx = jnp.maximum(mx, biased(half, g, j, row_term))
                mx_ref[rows] = mx
            row_max = jnp.max(mx_ref[...], axis=-1, keepdims=True)
            m_old = m_ref[half]
            m_new = jnp.maximum(m_old, row_max if own else row_max - rt_ref[...])
            alpha_ref[slot, half] = jnp.exp2(m_old - m_new)
            m_ref[half] = m_new
            sub_ref[...] = m_new if own else m_new + rt_ref[...]
            for g in range(0, n_row, 2):
                pair_rows = slice(g * _SUBLANES, (g + 2) * _SUBLANES)
                rows = [slice(gg * _SUBLANES, (gg + 1) * _SUBLANES) for gg in (g, g + 1)]
                row_term = [rt_ref[r] if own else None for r in rows]
                sub = [sub_ref[r] for r in rows]
                tot = [None, None]
                for j in range(n_col):
                    ps = [jnp.exp2(biased(half, g + i, j, row_term[i]) - sub[i]) for i in range(2)]
                    tot = [p if t is None else t + p for t, p in zip(tot, ps)]
                    p_ref[slot, half, pair_rows, j * _LANES:(j + 1) * _LANES] = (
                        jnp.concatenate(ps, axis=0).astype(p_ref.dtype))
                for i in range(2):
                    l_ref[half, rows[i]] = alpha_ref[slot, half, rows[i]] * l_ref[half, rows[i]] + tot[i]

    scores(c_own, 0)
    scores(chunk_of(1), 1)
    softmax(c_own, 0, own=True)

    def pair(jj, carry):
        j = 2 * jj + 1
        scores(chunk_of(j + 1), 0)
        softmax(chunk_of(j), 1, own=False)
        values(chunk_of(j - 1), 0)
        scores(chunk_of(j + 2), 1)
        softmax(chunk_of(j + 1), 0, own=False)
        values(chunk_of(j), 1)
        return carry

    lax.fori_loop(0, (n_chunks - 2) // 2, pair, 0)
    softmax(chunk_of(n_chunks - 1), 1, own=False)
    values(chunk_of(n_chunks - 2), 0)
    values(chunk_of(n_chunks - 1), 1)

    lam = (jnp.exp(jnp.sum(lq1_ref[...] * lk1_ref[...], keepdims=True))
           - jnp.exp(jnp.sum(lq2_ref[...] * lk2_ref[...], keepdims=True)) + lambda_init)
    l0 = jnp.sum(l_ref[0], axis=-1, keepdims=True)
    l1 = jnp.sum(l_ref[1], axis=-1, keepdims=True)
    o = acc_ref[0] / l0 - lam * (acc_ref[1] / l1)
    o_ref[...] = (_rms(o, gsub_ref[...]) * (1.0 - lambda_init)).astype(o_ref.dtype)


def _diff_attn(qkv, batch, seq, lq1, lk1, lq2, lk2, g_sub, lambda_init, *, tq=512, tk=1024):
    m, d3 = qkv.shape
    d = d3 // 3
    heads = _ATTN_HEADS
    dv = d // heads
    dh = dv // 2
    tq, tk = _tile(seq, tq), _tile(seq, tk)
    assert tk % tq == 0
    nq = seq // tq
    slopes = jnp.exp2(-8.0 * jnp.arange(1, heads + 1, dtype=_F32) / heads) * _LOG2E
    lam_spec = pl.BlockSpec((1, dh), lambda b, h, i: (0, 0))
    stat = pltpu.VMEM((tq, _LANES), _F32)
    return pl.pallas_call(
        functools.partial(_diff_attn_kernel, lambda_init=lambda_init, tk=tk),
        grid=(batch, heads, nq),
        in_specs=[pl.BlockSpec(memory_space=pltpu.SMEM),
                  pl.BlockSpec((tq, dv), lambda b, h, i: (b * nq + i, h)),
                  pl.BlockSpec((seq, dv), lambda b, h, i: (b, heads + h)),
                  pl.BlockSpec((seq, dv), lambda b, h, i: (b, 2 * heads + h)),
                  lam_spec, lam_spec, lam_spec, lam_spec,
                  pl.BlockSpec((1, dv), lambda b, h, i: (0, 0))],
        out_specs=pl.BlockSpec((tq, dv), lambda b, h, i: (b * nq + i, h)),
        out_shape=jax.ShapeDtypeStruct((m, d), _BF16),
        scratch_shapes=[pltpu.VMEM((2, 2, tq, tk), _F32),
                        pltpu.VMEM((2, 2, tq, tk), _BF16),
                        pltpu.VMEM((2, 2, tq, _LANES), _F32),
                        stat, stat, stat, stat,
                        pltpu.VMEM((2, tq, _LANES), _F32),
                        pltpu.VMEM((2, tq, _LANES), _F32),
                        pltpu.VMEM((2, tq, dv), _F32)],
        compiler_params=_compiler_params("parallel", "parallel", "parallel"),
        name="diff_attn",
    )(slopes, qkv, qkv, qkv, lq1.reshape(1, dh), lk1.reshape(1, dh), lq2.reshape(1, dh),
      lk2.reshape(1, dh), g_sub.reshape(1, dv))


def _ffn(h, wg, wu, wd):
    return _mm(_gate_up(h, wg, wu), wd, _F32, tm=1024, tn=512)


def _pool_fourier(h, batch, seq, w_in, w_pool, pool_scale, w_fourier, w_out, dft):
    cos_s, sin_s, chan_dft = dft
    pool_width = pool_scale.shape[0]
    four_width = w_fourier.shape[0]
    u = _mm(h, w_in, _F32)
    a = _pool_proj(_pool(u, seq, pool_width), w_pool, pool_scale)
    xcs = _mm(u, chan_dft, _BF16, a_col_block=(pool_width // four_width, four_width))
    head_dim = four_width // _FOURIER_HEADS
    f = _seq_dft(cos_s, sin_s, xcs, batch, seq, four_width, 1.0 / math.sqrt(seq * head_dim))
    f = _mm(f, w_fourier, _BF16)
    return _mm_cat(a, f, w_out)


def kernel(x, norm_pre, norm_post, w_ffn_gate, w_ffn_up, w_ffn_down, w_mix_in, w_pool, pool_scale,
           w_fourier, w_mix_out, w_qkv, w_attn_out, lambda_q1, lambda_k1, lambda_q2, lambda_k2,
           subln_gain):
    batch, seq, d = x.shape
    depth = norm_pre.shape[0]
    m = batch * seq
    x = x.reshape(m, d)
    bf = lambda w: w.astype(_BF16)

    four_width = w_fourier.shape[1]
    head_dim = four_width // _FOURIER_HEADS
    cos_s, sin_s = _dft_tables(seq)
    cos_c, sin_c = _dft_tables(head_dim)
    eye = jnp.eye(_FOURIER_HEADS, dtype=_F32)
    chan_dft = jnp.concatenate([jnp.kron(eye, cos_c), jnp.kron(eye, sin_c)], axis=1)
    dft = (bf(cos_s), bf(sin_s), bf(chan_dft))

    attn_head_dim = d // (2 * _ATTN_HEADS)

    h = _prenorm(x, norm_pre[0, 0])
    for l in range(depth):
        i = l // 2
        y = _ffn(h, bf(w_ffn_gate[l, 0]), bf(w_ffn_up[l, 0]), bf(w_ffn_down[l, 0]))
        x, h = _residual(y, x, norm_post[l, 0], norm_pre[l, 1], 0.5)
        if l % 2 == 0:
            y = _pool_fourier(h, batch, seq, bf(w_mix_in[i]), bf(w_pool[i]), pool_scale[i],
                              bf(w_fourier[i]), bf(w_mix_out[i]), dft)
        else:
            lambda_init = 0.8 - 0.6 * math.exp(-0.3 * l)
            qkv = _mm(h, bf(w_qkv[i]), _BF16, scaled_cols=d, scale=attn_head_dim ** -0.5 * _LOG2E)
            o = _diff_attn(qkv, batch, seq, lambda_q1[i], lambda_k1[i], lambda_q2[i], lambda_k2[i],
                           subln_gain[i], lambda_init)
            y = _mm(o, bf(w_attn_out[i]), _F32)
        x, h = _residual(y, x, norm_post[l, 1], norm_pre[l, 2], 1.0)
        y = _ffn(h, bf(w_ffn_gate[l, 1]), bf(w_ffn_up[l, 1]), bf(w_ffn_down[l, 1]))
        g_next = norm_pre[l + 1, 0] if l + 1 < depth else None
        x, h = _residual(y, x, norm_post[l, 2], g_next, 0.5)
    return x.reshape(batch, seq, d)
```

```python
import functools
import math

import jax
import jax.numpy as jnp
from jax import lax
from jax.experimental import pallas as pl
from jax.experimental.pallas import tpu as pltpu

_V7X_VMEM_LIMIT_BYTES = 56 * 1024 * 1024
_SUBLANES = 8
_LANES = 128

_EPS = 1e-6
_POOL_WINDOWS = (2, 4, 8, 16)
_FOURIER_HEADS = 4
_ATTN_HEADS = 16
_NEG_BIG = -1e30

_BF16 = jnp.bfloat16
_F32 = jnp.float32


def _compiler_params(*semantics, flags=None):
    return pltpu.CompilerParams(dimension_semantics=semantics,
                                vmem_limit_bytes=_V7X_VMEM_LIMIT_BYTES, flags=flags)


def _tile(n, want):
    t = min(n, want)
    assert n % t == 0, (n, want)
    return t


def _rms(x, g):
    return x * lax.rsqrt(jnp.mean(x * x, axis=-1, keepdims=True) + _EPS) * g


def _prenorm_kernel(x_ref, g_ref, h_ref):
    h_ref[...] = _rms(x_ref[...], g_ref[...]).astype(h_ref.dtype)


def _prenorm(x, g):
    m, d = x.shape
    tm = _tile(m, 256)
    return pl.pallas_call(
        _prenorm_kernel,
        grid=(m // tm,),
        in_specs=[pl.BlockSpec((tm, d), lambda i: (i, 0)),
                  pl.BlockSpec((1, d), lambda i: (0, 0))],
        out_specs=pl.BlockSpec((tm, d), lambda i: (i, 0)),
        out_shape=jax.ShapeDtypeStruct((m, d), _BF16),
        compiler_params=_compiler_params("parallel"),
        name="prenorm",
    )(x, g.reshape(1, d))


def _residual_kernel(y_ref, x_ref, gpost_ref, gpre_ref, xo_ref, h_ref, *, step):
    xn = x_ref[...] + step * _rms(y_ref[...], gpost_ref[...])
    xo_ref[...] = xn
    h_ref[...] = _rms(xn, gpre_ref[...]).astype(h_ref.dtype)


def _residual_last_kernel(y_ref, x_ref, gpost_ref, xo_ref, *, step):
    xo_ref[...] = x_ref[...] + step * _rms(y_ref[...], gpost_ref[...])


def _residual(y, x, g_post, g_pre_next, step):
    m, d = x.shape
    tm = _tile(m, 256)
    row = pl.BlockSpec((tm, d), lambda i: (i, 0))
    vec = pl.BlockSpec((1, d), lambda i: (0, 0))
    if g_pre_next is None:
        return pl.pallas_call(
            functools.partial(_residual_last_kernel, step=step),
            grid=(m // tm,),
            in_specs=[row, row, vec],
            out_specs=row,
            out_shape=jax.ShapeDtypeStruct((m, d), _F32),
            compiler_params=_compiler_params("parallel"),
            name="residual_last",
        )(y, x, g_post.reshape(1, d)), None
    return pl.pallas_call(
        functools.partial(_residual_kernel, step=step),
        grid=(m // tm,),
        in_specs=[row, row, vec, vec],
        out_specs=[row, row],
        out_shape=[jax.ShapeDtypeStruct((m, d), _F32), jax.ShapeDtypeStruct((m, d), _BF16)],
        compiler_params=_compiler_params("parallel"),
        name="residual",
    )(y, x, g_post.reshape(1, d), g_pre_next.reshape(1, d))


def _weight_spec(w, block, index):
    lead = w[1]
    return pl.BlockSpec((None,) * len(lead) + block, lambda *g: lead + index(*g))


def _mm_kernel(a_ref, w_ref, o_ref, *, scaled_col_blocks, scale):
    acc = jnp.dot(a_ref[...].astype(_BF16), w_ref[...], preferred_element_type=_F32)
    if scaled_col_blocks:
        acc = acc * jnp.where(pl.program_id(1) < scaled_col_blocks, scale, 1.0)
    o_ref[...] = acc.astype(o_ref.dtype)


def _mm(a, w, out_dtype, *, tm=1024, tn=1024, a_col_block=None, scaled_cols=0, scale=1.0):
    m = a.shape[0]
    k, n = w[0].shape[-2:]
    tm, tn = _tile(m, tm), _tile(n, tn)
    a_col = 0
    if a_col_block is not None:
        a_col, width = a_col_block
        assert width == k
    else:
        assert a.shape[1] == k
    assert scaled_cols % tn == 0
    return pl.pallas_call(
        functools.partial(_mm_kernel, scaled_col_blocks=scaled_cols // tn, scale=scale),
        grid=(m // tm, n // tn),
        in_specs=[pl.BlockSpec((tm, k), lambda i, j: (i, a_col)),
                  _weight_spec(w, (k, tn), lambda i, j: (0, j))],
        out_specs=pl.BlockSpec((tm, tn), lambda i, j: (i, j)),
        out_shape=jax.ShapeDtypeStruct((m, n), out_dtype),
        compiler_params=_compiler_params("parallel", "arbitrary"),
        name="matmul",
    )(a, w[0])


def _gate_up_kernel(h_ref, wg_ref, wu_ref, o_ref):
    h = h_ref[...]
    g = jnp.dot(h, wg_ref[...], preferred_element_type=_F32)
    u = jnp.dot(h, wu_ref[...], preferred_element_type=_F32)
    o_ref[...] = (g * jax.nn.sigmoid(g) * u).astype(o_ref.dtype)


def _gate_up(h, wg, wu, *, tm=1024, tn=512):
    m, k = h.shape
    n = wg[0].shape[-1]
    tm, tn = _tile(m, tm), _tile(n, tn)
    return pl.pallas_call(
        _gate_up_kernel,
        grid=(m // tm, n // tn),
        in_specs=[pl.BlockSpec((tm, k), lambda i, j: (i, 0)),
                  _weight_spec(wg, (k, tn), lambda i, j: (0, j)),
                  _weight_spec(wu, (k, tn), lambda i, j: (0, j))],
        out_specs=pl.BlockSpec((tm, tn), lambda i, j: (i, j)),
        out_shape=jax.ShapeDtypeStruct((m, n), _BF16),
        compiler_params=_compiler_params("parallel", "arbitrary"),
        name="gate_up",
    )(h, wg[0], wu[0])


def _mm_cat_kernel(a_ref, f_ref, w_ref, o_ref):
    ka = a_ref.shape[1]
    acc = jnp.dot(a_ref[...], w_ref[:ka, :], preferred_element_type=_F32)
    acc = acc + jnp.dot(f_ref[...], w_ref[ka:, :], preferred_element_type=_F32)
    o_ref[...] = acc.astype(o_ref.dtype)


def _mm_cat(a, f, w, *, tm=1024, tn=1024):
    m, ka = a.shape
    kf = f.shape[1]
    k, n = w[0].shape[-2:]
    assert ka + kf == k
    tm, tn = _tile(m, tm), _tile(n, tn)
    return pl.pallas_call(
        _mm_cat_kernel,
        grid=(m // tm, n // tn),
        in_specs=[pl.BlockSpec((tm, ka), lambda i, j: (i, 0)),
                  pl.BlockSpec((tm, kf), lambda i, j: (i, 0)),
                  _weight_spec(w, (k, tn), lambda i, j: (0, j))],
        out_specs=pl.BlockSpec((tm, tn), lambda i, j: (i, j)),
        out_shape=jax.ShapeDtypeStruct((m, n), _F32),
        compiler_params=_compiler_params("parallel", "arbitrary"),
        name="matmul_cat",
    )(a, f, w[0])


_POOL_HALO = 8
_POOL_ROWS = 256


def _pool_kernel(prev_ref, x_ref, next_ref, o_ref, *, seq, group):
    ts = x_ref.shape[0]
    ext_rows = ts + 2 * _POOL_HALO
    t = (pl.program_id(0) * ts + lax.broadcasted_iota(jnp.int32, (ts, 1), 0)) % seq
    for g, w in enumerate(_POOL_WINDOWS):
        cols = slice(g * group, (g + 1) * group)
        x = x_ref[:, cols]
        ext = jnp.concatenate([prev_ref[:, cols], x, next_ref[:, cols]], axis=0)
        total = jnp.zeros_like(x)
        count = jnp.zeros((ts, 1), _F32)
        for d in range(-(w // 2), w - w // 2):
            valid = (t + d >= 0) & (t + d < seq)
            if d == 0:
                shifted = x
            else:
                shifted = pltpu.roll(ext, (-d) % ext_rows, 0)[_POOL_HALO:_POOL_HALO + ts]
            total = total + jnp.where(valid, shifted, 0.0)
            count = count + valid.astype(_F32)
        o_ref[:, cols] = (total / count - x).astype(o_ref.dtype)


def _pool(u, seq, pool_width):
    m = u.shape[0]
    group = pool_width // len(_POOL_WINDOWS)
    assert max(_POOL_WINDOWS) // 2 <= _POOL_HALO
    ts = _tile(seq, _POOL_ROWS)
    per = ts // _POOL_HALO
    last = m // _POOL_HALO - 1
    return pl.pallas_call(
        functools.partial(_pool_kernel, seq=seq, group=group),
        grid=(m // ts,),
        in_specs=[pl.BlockSpec((_POOL_HALO, pool_width), lambda i: (jnp.maximum(i * per - 1, 0), 0)),
                  pl.BlockSpec((ts, pool_width), lambda i: (i, 0)),
                  pl.BlockSpec((_POOL_HALO, pool_width), lambda i: (jnp.minimum((i + 1) * per, last), 0))],
        out_specs=pl.BlockSpec((ts, pool_width), lambda i: (i, 0)),
        out_shape=jax.ShapeDtypeStruct((m, pool_width), _BF16),
        compiler_params=_compiler_params("parallel"),
        name="pool",
    )(u, u, u)


def _pool_proj_kernel(p_ref, w_ref, s_ref, o_ref):
    acc = jnp.dot(p_ref[...], w_ref[...], preferred_element_type=_F32)
    o_ref[...] = (acc * s_ref[...]).astype(o_ref.dtype)


def _pool_proj(pooled, w_pool, pool_scale, *, tm=1024):
    m, width = pooled.shape
    groups, group, _ = w_pool[0].shape[-3:]
    tm = _tile(m, tm)
    return pl.pallas_call(
        _pool_proj_kernel,
        grid=(m // tm, groups),
        in_specs=[pl.BlockSpec((tm, group), lambda i, g: (i, g)),
                  _weight_spec(w_pool, (None, group, group), lambda i, g: (g, 0, 0)),
                  pl.BlockSpec((1, group), lambda i, g: (0, g))],
        out_specs=pl.BlockSpec((tm, group), lambda i, g: (i, g)),
        out_shape=jax.ShapeDtypeStruct((m, width), _BF16),
        compiler_params=_compiler_params("parallel", "arbitrary"),
        name="pool_proj",
    )(pooled, w_pool[0], pool_scale.reshape(1, width))


def _dft_rows(rows, n):
    idx = (rows[:, None] * jnp.arange(n, dtype=jnp.int32)[None, :]) % n
    ang = idx.astype(_F32) * (2.0 * math.pi / n)
    return jnp.cos(ang), jnp.sin(ang)


def _dft_tables(n):
    lo_rows = 64 if n % 64 == 0 and n > 64 else 1
    c_hi, s_hi = _dft_rows(jnp.arange(n // lo_rows, dtype=jnp.int32) * lo_rows, n)
    c_lo, s_lo = _dft_rows(jnp.arange(lo_rows, dtype=jnp.int32), n)
    cos = c_hi[:, None, :] * c_lo[None, :, :] - s_hi[:, None, :] * s_lo[None, :, :]
    sin = s_hi[:, None, :] * c_lo[None, :, :] + c_hi[:, None, :] * s_lo[None, :, :]
    return cos.reshape(n, n), sin.reshape(n, n)


def _seq_dft_kernel(c_ref, s_ref, xc_ref, xs_ref, o_ref, acc_ref, *, norm):
    k = pl.program_id(3)

    @pl.when(k == 0)
    def _():
        acc_ref[...] = jnp.zeros_like(acc_ref)

    acc_ref[...] += (jnp.dot(c_ref[...], xc_ref[...], preferred_element_type=_F32)
                     - jnp.dot(s_ref[...], xs_ref[...], preferred_element_type=_F32))

    @pl.when(k == pl.num_programs(3) - 1)
    def _():
        o_ref[...] = (acc_ref[...] * norm).astype(o_ref.dtype)


def _seq_dft(cos_s, sin_s, xcs, batch, seq, width, norm, *, tm=1024, tn=1024, tk=1024):
    tm, tn, tk = _tile(seq, tm), _tile(width, tn), _tile(seq, tk)
    nj = width // tn
    rows_k = seq // tk
    rows_m = seq // tm
    return pl.pallas_call(
        functools.partial(_seq_dft_kernel, norm=norm),
        grid=(batch, rows_m, nj, rows_k),
        in_specs=[pl.BlockSpec((tm, tk), lambda b, i, j, k: (i, k)),
                  pl.BlockSpec((tm, tk), lambda b, i, j, k: (i, k)),
                  pl.BlockSpec((tk, tn), lambda b, i, j, k: (b * rows_k + k, j)),
                  pl.BlockSpec((tk, tn), lambda b, i, j, k: (b * rows_k + k, nj + j))],
        out_specs=pl.BlockSpec((tm, tn), lambda b, i, j, k: (b * rows_m + i, j)),
        out_shape=jax.ShapeDtypeStruct((batch * seq, width), _BF16),
        scratch_shapes=[pltpu.VMEM((tm, tn), _F32)],
        compiler_params=_compiler_params("parallel", "parallel", "parallel", "arbitrary"),
        name="seq_dft",
    )(cos_s, sin_s, xcs, xcs)


_LOG2E = math.log2(math.e)
_NT_DIMS = (((1,), (1,)), ((), ()))


_RAMP_PARTS = 3
_SIGNS = (1.0, -1.0, 0.0)
_OWN = 2
_ATTN_ROW_BLOCK = 512


def _diff_attn_kernel(slope_ref, q_ref, k_ref, v_ref, lq1_ref, lk1_ref, lq2_ref, lk2_ref, gsub_ref,
                      o_ref, kaug_ref, qaug_ref, s_ref, p_ref, alpha_ref, rq_ref, rt_ref, sub_ref,
                      mx_ref, m_ref, l_ref, acc_ref, *, lambda_init, tk):
    h = pl.program_id(1)
    qi = pl.program_id(2)
    tq, dv = q_ref.shape
    dh = dv // 2
    seq = k_ref.shape[0]
    n_chunks = seq // tk
    n_col = tk // _LANES
    rb = min(tq, _ATTN_ROW_BLOCK)
    assert n_chunks % 2 == 0 and tq % rb == 0
    slope = slope_ref[h]
    c_own = (qi * tq) // tk

    @pl.when(qi == 0)
    def _():
        kb = min(tk, 256)

        def widen(i, carry):
            start = pl.multiple_of(i * kb, kb)
            local = (i % (tk // kb)) * kb + lax.broadcasted_iota(jnp.int32, (kb, _LANES), 0)
            rest = slope * local.astype(_F32)
            lane = lax.broadcasted_iota(jnp.int32, (kb, _LANES), 1)
            ext = jnp.zeros((kb, _LANES), _F32)
            for part in range(_RAMP_PARTS):
                term = rest.astype(_BF16).astype(_F32)
                ext = jnp.where(lane == part, term, ext)
                rest = rest - term
            for half in range(2):
                kaug_ref[half, pl.ds(start, kb), :dh] = k_ref[pl.ds(start, kb), half * dh:(half + 1) * dh]
                kaug_ref[half, pl.ds(start, kb), dh:] = ext.astype(_BF16)
            return carry

        lax.fori_loop(0, seq // kb, widen, 0)

    lane_q = lax.broadcasted_iota(jnp.int32, (tq, _LANES), 1)
    for variant, sign in enumerate(_SIGNS):
        ext = jnp.where(lane_q < _RAMP_PARTS, sign, 0.0).astype(_BF16)
        for half in range(2):
            qaug_ref[variant, half] = jnp.concatenate([q_ref[:, half * dh:(half + 1) * dh], ext], axis=1)

    rq_ref[...] = slope * lax.broadcasted_iota(jnp.int32, (tq, _LANES), 0).astype(_F32)
    ramp_k = [slope * (lax.broadcasted_iota(jnp.int32, (_SUBLANES, _LANES), 1) + j * _LANES).astype(_F32)
              for j in range(n_col)]

    m_ref[...] = jnp.full_like(m_ref, _NEG_BIG)
    l_ref[...] = jnp.zeros_like(l_ref)
    acc_ref[...] = jnp.zeros_like(acc_ref)

    def chunk_of(j):
        rest = jnp.where(j - 1 >= c_own, j, j - 1)
        return jnp.where(j == 0, c_own, rest)

    def scores(c, slot, own, r0):
        start = pl.multiple_of(c * tk, tk)
        variant = _OWN if own else jnp.where(c < c_own, 0, 1)
        for half in range(2):
            s_ref[slot, half, pl.ds(r0, rb)] = lax.dot_general(
                qaug_ref[variant, half, pl.ds(r0, rb)], kaug_ref[half, pl.ds(start, tk), :],
                _NT_DIMS, preferred_element_type=_F32)

    def values(c, slot, r0):
        start = pl.multiple_of(c * tk, tk)
        v = v_ref[pl.ds(start, tk), :]
        for half in range(2):
            alpha = alpha_ref[slot, half, pl.ds(r0, rb)]
            acc_ref[half, pl.ds(r0, rb)] = (
                jnp.concatenate([alpha] * (dv // _LANES), axis=1) * acc_ref[half, pl.ds(r0, rb)]
                + jnp.dot(p_ref[slot, half, pl.ds(r0, rb)], v, preferred_element_type=_F32))

    def softmax(c, slot, own, r0):
        block = pl.ds(r0, rb)
        shift = slope * (qi * tq - c * tk).astype(_F32)
        if own:
            rt_ref[block] = rq_ref[block] + shift
        else:
            sign = jnp.where(c < c_own, 1.0, -1.0).astype(_F32)
            rt_ref[block] = sign * (rq_ref[block] + shift)

        def group(g):
            return pl.ds(r0 + g * _SUBLANES, _SUBLANES)

        def biased(half, g, j, row_term):
            s = s_ref[slot, half, group(g), j * _LANES:(j + 1) * _LANES]
            if own:
                return s - jnp.abs(row_term - ramp_k[j])
            return s

        for half in range(2):
            for g in range(rb // _SUBLANES):
                row_term = rt_ref[group(g)] if own else None
                mx = biased(half, g, 0, row_term)
                for j in range(1, n_col):
                    mx = jnp.maximum(mx, biased(half, g, j, row_term))
                mx_ref[group(g)] = mx
            row_max = jnp.max(mx_ref[block], axis=-1, keepdims=True)
            m_old = m_ref[half, block]
            m_new = jnp.maximum(m_old, row_max if own else row_max - rt_ref[block])
            alpha_ref[slot, half, block] = jnp.exp2(m_old - m_new)
            m_ref[half, block] = m_new
            sub_ref[block] = m_new if own else m_new + rt_ref[block]
            for g in range(0, rb // _SUBLANES, 2):
                pair_rows = pl.ds(r0 + g * _SUBLANES, 2 * _SUBLANES)
                row_term = [rt_ref[group(g + i)] if own else None for i in range(2)]
                sub = [sub_ref[group(g + i)] for i in range(2)]
                tot = [None, None]
                for j in range(n_col):
                    ps = [jnp.exp2(biased(half, g + i, j, row_term[i]) - sub[i]) for i in range(2)]
                    tot = [p if t is None else t + p for t, p in zip(tot, ps)]
                    p_ref[slot, half, pair_rows, j * _LANES:(j + 1) * _LANES] = (
                        jnp.concatenate(ps, axis=0).astype(p_ref.dtype))
                for i in range(2):
                    l_ref[half, group(g + i)] = (alpha_ref[slot, half, group(g + i)] * l_ref[half, group(g + i)]
                                                 + tot[i])

    def phase(score_job=None, softmax_job=None, value_job=None):
        def block(r, carry):
            r0 = r * rb if isinstance(r, int) else pl.multiple_of(r * rb, rb)
            if score_job is not None:
                scores(*score_job, r0)
            if softmax_job is not None:
                softmax(*softmax_job, r0)
            if value_job is not None:
                values(*value_job, r0)
            return carry
        if tq == rb:
            block(0, 0)
        else:
            lax.fori_loop(0, tq // rb, block, 0)

    phase(score_job=(c_own, 0, True))
    phase(score_job=(chunk_of(1), 1, False), softmax_job=(c_own, 0, True))

    def pair(jj, carry):
        j = 2 * jj + 1
        phase((chunk_of(j + 1), 0, False), (chunk_of(j), 1, False), (chunk_of(j - 1), 0))
        phase((chunk_of(j + 2), 1, False), (chunk_of(j + 1), 0, False), (chunk_of(j), 1))
        return carry

    lax.fori_loop(0, (n_chunks - 2) // 2, pair, 0)
    phase(softmax_job=(chunk_of(n_chunks - 1), 1, False), value_job=(chunk_of(n_chunks - 2), 0))
    phase(value_job=(chunk_of(n_chunks - 1), 1))

    lam = (jnp.exp(jnp.sum(lq1_ref[...] * lk1_ref[...], keepdims=True))
           - jnp.exp(jnp.sum(lq2_ref[...] * lk2_ref[...], keepdims=True)) + lambda_init)
    l0 = jnp.sum(l_ref[0], axis=-1, keepdims=True)
    l1 = jnp.sum(l_ref[1], axis=-1, keepdims=True)
    o = acc_ref[0] / l0 - lam * (acc_ref[1] / l1)
    o_ref[...] = (_rms(o, gsub_ref[...]) * (1.0 - lambda_init)).astype(o_ref.dtype)


def _diff_attn(qkv, batch, seq, lq1, lk1, lq2, lk2, g_sub, lambda_init, *, tq=512, tk=512):
    m, d3 = qkv.shape
    d = d3 // 3
    heads = _ATTN_HEADS
    dv = d // heads
    dh = dv // 2
    tq, tk = _tile(seq, tq), _tile(seq, tk)
    assert tk % tq == 0
    nq = seq // tq
    slopes = jnp.exp2(-8.0 * jnp.arange(1, heads + 1, dtype=_F32) / heads) * _LOG2E
    lam_spec = pl.BlockSpec((1, dh), lambda b, h, i: (0, 0))
    stat = pltpu.VMEM((tq, _LANES), _F32)
    return pl.pallas_call(
        functools.partial(_diff_attn_kernel, lambda_init=lambda_init, tk=tk),
        grid=(batch, heads, nq),
        in_specs=[pl.BlockSpec(memory_space=pltpu.SMEM),
                  pl.BlockSpec((tq, dv), lambda b, h, i: (b * nq + i, h)),
                  pl.BlockSpec((seq, dv), lambda b, h, i: (b, heads + h)),
                  pl.BlockSpec((seq, dv), lambda b, h, i: (b, 2 * heads + h)),
                  lam_spec, lam_spec, lam_spec, lam_spec,
                  pl.BlockSpec((1, dv), lambda b, h, i: (0, 0))],
        out_specs=pl.BlockSpec((tq, dv), lambda b, h, i: (b * nq + i, h)),
        out_shape=jax.ShapeDtypeStruct((m, d), _BF16),
        scratch_shapes=[pltpu.VMEM((2, seq, 2 * dh), _BF16),
                        pltpu.VMEM((len(_SIGNS), 2, tq, 2 * dh), _BF16),
                        pltpu.VMEM((2, 2, tq, tk), _F32),
                        pltpu.VMEM((2, 2, tq, tk), _BF16),
                        pltpu.VMEM((2, 2, tq, _LANES), _F32),
                        stat, stat, stat, stat,
                        pltpu.VMEM((2, tq, _LANES), _F32),
                        pltpu.VMEM((2, tq, _LANES), _F32),
                        pltpu.VMEM((2, tq, dv), _F32)],
        compiler_params=_compiler_params("parallel", "parallel", "arbitrary"),
        name="diff_attn",
    )(slopes, qkv, qkv, qkv, lq1.reshape(1, dh), lk1.reshape(1, dh), lq2.reshape(1, dh),
      lk2.reshape(1, dh), g_sub.reshape(1, dv))


def _ffn(h, wg, wu, wd):
    return _mm(_gate_up(h, wg, wu), wd, _F32, tm=1024, tn=512)


def _pool_fourier(h, batch, seq, w_in, w_pool, pool_scale, w_fourier, w_out, dft):
    cos_s, sin_s, chan_dft = dft
    pool_width = pool_scale.shape[0]
    four_width = w_fourier[0].shape[-1]
    u = _mm(h, w_in, _F32)
    a = _pool_proj(_pool(u, seq, pool_width), w_pool, pool_scale)
    xcs = _mm(u, (chan_dft, ()), _BF16, a_col_block=(pool_width // four_width, four_width))
    head_dim = four_width // _FOURIER_HEADS
    f = _seq_dft(cos_s, sin_s, xcs, batch, seq, four_width, 1.0 / math.sqrt(seq * head_dim))
    f = _mm(f, w_fourier, _BF16)
    return _mm_cat(a, f, w_out)


def kernel(x, norm_pre, norm_post, w_ffn_gate, w_ffn_up, w_ffn_down, w_mix_in, w_pool, pool_scale,
           w_fourier, w_mix_out, w_qkv, w_attn_out, lambda_q1, lambda_k1, lambda_q2, lambda_k2,
           subln_gain):
    batch, seq, d = x.shape
    depth = norm_pre.shape[0]
    m = batch * seq
    x = x.reshape(m, d)
    bf = lambda w: w.astype(_BF16)

    four_width = w_fourier.shape[1]
    head_dim = four_width // _FOURIER_HEADS
    cos_s, sin_s = _dft_tables(seq)
    cos_c, sin_c = _dft_tables(head_dim)
    eye = jnp.eye(_FOURIER_HEADS, dtype=_F32)
    chan_dft = jnp.concatenate([jnp.kron(eye, cos_c), jnp.kron(eye, sin_c)], axis=1)
    dft = (bf(cos_s), bf(sin_s), bf(chan_dft))

    attn_head_dim = d // (2 * _ATTN_HEADS)
    w_gate, w_up, w_down = bf(w_ffn_gate), bf(w_ffn_up), bf(w_ffn_down)
    w_in, w_pl, w_fr, w_out = bf(w_mix_in), bf(w_pool), bf(w_fourier), bf(w_mix_out)
    w_qk, w_ao = bf(w_qkv), bf(w_attn_out)

    h = _prenorm(x, norm_pre[0, 0])
    for l in range(depth):
        i = l // 2
        y = _ffn(h, (w_gate, (l, 0)), (w_up, (l, 0)), (w_down, (l, 0)))
        x, h = _residual(y, x, norm_post[l, 0], norm_pre[l, 1], 0.5)
        if l % 2 == 0:
            y = _pool_fourier(h, batch, seq, (w_in, (i,)), (w_pl, (i,)), pool_scale[i],
                              (w_fr, (i,)), (w_out, (i,)), dft)
        else:
            lambda_init = 0.8 - 0.6 * math.exp(-0.3 * l)
            qkv = _mm(h, (w_qk, (i,)), _BF16, scaled_cols=d, scale=attn_head_dim ** -0.5 * _LOG2E)
            o = _diff_attn(qkv, batch, seq, lambda_q1[i], lambda_k1[i], lambda_q2[i], lambda_k2[i],
                           subln_gain[i], lambda_init)
            y = _mm(o, (w_ao, (i,)), _F32)
        x, h = _residual(y, x, norm_post[l, 1], norm_pre[l, 2], 1.0)
        y = _ffn(h, (w_gate, (l, 1)), (w_up, (l, 1)), (w_down, (l, 1)))
        g_next = norm_pre[l + 1, 0] if l + 1 < depth else None
        x, h = _residual(y, x, norm_post[l, 2], g_next, 0.5)
    return x.reshape(batch, seq, d)
```

```python
import functools
import math

import jax
import jax.numpy as jnp
from jax import lax
from jax.experimental import pallas as pl
from jax.experimental.pallas import tpu as pltpu

_V7X_VMEM_LIMIT_BYTES = 56 * 1024 * 1024
_SUBLANES = 8
_LANES = 128

_EPS = 1e-6
_POOL_WINDOWS = (2, 4, 8, 16)
_FOURIER_HEADS = 4
_ATTN_HEADS = 16
_NEG_BIG = -1e30

_BF16 = jnp.bfloat16
_F32 = jnp.float32


def _compiler_params(*semantics, flags=None):
    return pltpu.CompilerParams(dimension_semantics=semantics,
                                vmem_limit_bytes=_V7X_VMEM_LIMIT_BYTES, flags=flags)


def _tile(n, want):
    t = min(n, want)
    assert n % t == 0, (n, want)
    return t


def _rms(x, g):
    return x * lax.rsqrt(jnp.mean(x * x, axis=-1, keepdims=True) + _EPS) * g


def _prenorm_kernel(x_ref, g_ref, h_ref):
    h_ref[...] = _rms(x_ref[...], g_ref[...]).astype(h_ref.dtype)


def _prenorm(x, g):
    m, d = x.shape
    tm = _tile(m, 256)
    return pl.pallas_call(
        _prenorm_kernel,
        grid=(m // tm,),
        in_specs=[pl.BlockSpec((tm, d), lambda i: (i, 0)),
                  pl.BlockSpec((1, d), lambda i: (0, 0))],
        out_specs=pl.BlockSpec((tm, d), lambda i: (i, 0)),
        out_shape=jax.ShapeDtypeStruct((m, d), _BF16),
        compiler_params=_compiler_params("parallel"),
        name="prenorm",
    )(x, g.reshape(1, d))


def _residual_kernel(y_ref, x_ref, gpost_ref, gpre_ref, xo_ref, h_ref, *, step):
    xn = x_ref[...] + step * _rms(y_ref[...], gpost_ref[...])
    xo_ref[...] = xn
    h_ref[...] = _rms(xn, gpre_ref[...]).astype(h_ref.dtype)


def _residual_last_kernel(y_ref, x_ref, gpost_ref, xo_ref, *, step):
    xo_ref[...] = x_ref[...] + step * _rms(y_ref[...], gpost_ref[...])


def _residual(y, x, g_post, g_pre_next, step):
    m, d = x.shape
    tm = _tile(m, 256)
    row = pl.BlockSpec((tm, d), lambda i: (i, 0))
    vec = pl.BlockSpec((1, d), lambda i: (0, 0))
    if g_pre_next is None:
        return pl.pallas_call(
            functools.partial(_residual_last_kernel, step=step),
            grid=(m // tm,),
            in_specs=[row, row, vec],
            out_specs=row,
            out_shape=jax.ShapeDtypeStruct((m, d), _F32),
            compiler_params=_compiler_params("parallel"),
            name="residual_last",
        )(y, x, g_post.reshape(1, d)), None
    return pl.pallas_call(
        functools.partial(_residual_kernel, step=step),
        grid=(m // tm,),
        in_specs=[row, row, vec, vec],
        out_specs=[row, row],
        out_shape=[jax.ShapeDtypeStruct((m, d), _F32), jax.ShapeDtypeStruct((m, d), _BF16)],
        compiler_params=_compiler_params("parallel"),
        name="residual",
    )(y, x, g_post.reshape(1, d), g_pre_next.reshape(1, d))


def _weight_spec(w, block, index):
    lead = w[1]
    return pl.BlockSpec((None,) * len(lead) + block, lambda *g: lead + index(*g))


def _mm_kernel(a_ref, w_ref, o_ref, *, scaled_col_blocks, scale):
    acc = jnp.dot(a_ref[...].astype(_BF16), w_ref[...], preferred_element_type=_F32)
    if scaled_col_blocks:
        acc = acc * jnp.where(pl.program_id(1) < scaled_col_blocks, scale, 1.0)
    o_ref[...] = acc.astype(o_ref.dtype)


def _mm(a, w, out_dtype, *, tm=1024, tn=1024, a_col_block=None, scaled_cols=0, scale=1.0):
    m = a.shape[0]
    k, n = w[0].shape[-2:]
    tm, tn = _tile(m, tm), _tile(n, tn)
    a_col = 0
    if a_col_block is not None:
        a_col, width = a_col_block
        assert width == k
    else:
        assert a.shape[1] == k
    assert scaled_cols % tn == 0
    return pl.pallas_call(
        functools.partial(_mm_kernel, scaled_col_blocks=scaled_cols // tn, scale=scale),
        grid=(m // tm, n // tn),
        in_specs=[pl.BlockSpec((tm, k), lambda i, j: (i, a_col)),
                  _weight_spec(w, (k, tn), lambda i, j: (0, j))],
        out_specs=pl.BlockSpec((tm, tn), lambda i, j: (i, j)),
        out_shape=jax.ShapeDtypeStruct((m, n), out_dtype),
        compiler_params=_compiler_params("parallel", "arbitrary"),
        name="matmul",
    )(a, w[0])


def _gate_up_kernel(h_ref, wg_ref, wu_ref, o_ref):
    h = h_ref[...]
    g = jnp.dot(h, wg_ref[...], preferred_element_type=_F32)
    u = jnp.dot(h, wu_ref[...], preferred_element_type=_F32)
    o_ref[...] = (g * jax.nn.sigmoid(g) * u).astype(o_ref.dtype)


def _gate_up(h, wg, wu, *, tm=1024, tn=512):
    m, k = h.shape
    n = wg[0].shape[-1]
    tm, tn = _tile(m, tm), _tile(n, tn)
    return pl.pallas_call(
        _gate_up_kernel,
        grid=(m // tm, n // tn),
        in_specs=[pl.BlockSpec((tm, k), lambda i, j: (i, 0)),
                  _weight_spec(wg, (k, tn), lambda i, j: (0, j)),
                  _weight_spec(wu, (k, tn), lambda i, j: (0, j))],
        out_specs=pl.BlockSpec((tm, tn), lambda i, j: (i, j)),
        out_shape=jax.ShapeDtypeStruct((m, n), _BF16),
        compiler_params=_compiler_params("parallel", "arbitrary"),
        name="gate_up",
    )(h, wg[0], wu[0])


def _mm_cat_kernel(a_ref, f_ref, w_ref, o_ref):
    ka = a_ref.shape[1]
    acc = jnp.dot(a_ref[...], w_ref[:ka, :], preferred_element_type=_F32)
    acc = acc + jnp.dot(f_ref[...], w_ref[ka:, :], preferred_element_type=_F32)
    o_ref[...] = acc.astype(o_ref.dtype)


def _mm_cat(a, f, w, *, tm=1024, tn=1024):
    m, ka = a.shape
    kf = f.shape[1]
    k, n = w[0].shape[-2:]
    assert ka + kf == k
    tm, tn = _tile(m, tm), _tile(n, tn)
    return pl.pallas_call(
        _mm_cat_kernel,
        grid=(m // tm, n // tn),
        in_specs=[pl.BlockSpec((tm, ka), lambda i, j: (i, 0)),
                  pl.BlockSpec((tm, kf), lambda i, j: (i, 0)),
                  _weight_spec(w, (k, tn), lambda i, j: (0, j))],
        out_specs=pl.BlockSpec((tm, tn), lambda i, j: (i, j)),
        out_shape=jax.ShapeDtypeStruct((m, n), _F32),
        compiler_params=_compiler_params("parallel", "arbitrary"),
        name="matmul_cat",
    )(a, f, w[0])


_POOL_HALO = 8
_POOL_ROWS = 256


def _pool_kernel(prev_ref, x_ref, next_ref, o_ref, *, seq, group):
    ts = x_ref.shape[0]
    ext_rows = ts + 2 * _POOL_HALO
    tile = pl.program_id(0) % (seq // ts)
    t = tile * ts + lax.broadcasted_iota(jnp.int32, (ts, 1), 0)
    first, last = tile == 0, tile == seq // ts - 1
    for g, w in enumerate(_POOL_WINDOWS):
        cols = slice(g * group, (g + 1) * group)
        x = x_ref[:, cols]
        run = jnp.concatenate([jnp.where(first, 0.0, prev_ref[:, cols]), x,
                               jnp.where(last, 0.0, next_ref[:, cols])], axis=0)
        span = 1
        while span < w:
            run = run + pltpu.roll(run, ext_rows - span, 0)
            span *= 2
        total = pltpu.roll(run, w // 2, 0)[_POOL_HALO:_POOL_HALO + ts]
        lo = jnp.clip(t - w // 2, 0, seq)
        hi = jnp.clip(t - w // 2 + w, 0, seq)
        o_ref[:, cols] = (total / (hi - lo).astype(_F32) - x).astype(o_ref.dtype)


def _pool(u, seq, pool_width):
    m = u.shape[0]
    group = pool_width // len(_POOL_WINDOWS)
    assert max(_POOL_WINDOWS) // 2 <= _POOL_HALO
    ts = _tile(seq, _POOL_ROWS)
    per = ts // _POOL_HALO
    last = m // _POOL_HALO - 1
    return pl.pallas_call(
        functools.partial(_pool_kernel, seq=seq, group=group),
        grid=(m // ts,),
        in_specs=[pl.BlockSpec((_POOL_HALO, pool_width), lambda i: (jnp.maximum(i * per - 1, 0), 0)),
                  pl.BlockSpec((ts, pool_width), lambda i: (i, 0)),
                  pl.BlockSpec((_POOL_HALO, pool_width), lambda i: (jnp.minimum((i + 1) * per, last), 0))],
        out_specs=pl.BlockSpec((ts, pool_width), lambda i: (i, 0)),
        out_shape=jax.ShapeDtypeStruct((m, pool_width), _BF16),
        compiler_params=_compiler_params("parallel"),
        name="pool",
    )(u, u, u)


def _pool_proj_kernel(p_ref, w_ref, s_ref, o_ref):
    acc = jnp.dot(p_ref[...], w_ref[...], preferred_element_type=_F32)
    o_ref[...] = (acc * s_ref[...]).astype(o_ref.dtype)


def _pool_proj(pooled, w_pool, pool_scale, *, tm=1024):
    m, width = pooled.shape
    groups, group, _ = w_pool[0].shape[-3:]
    tm = _tile(m, tm)
    return pl.pallas_call(
        _pool_proj_kernel,
        grid=(m // tm, groups),
        in_specs=[pl.BlockSpec((tm, group), lambda i, g: (i, g)),
                  _weight_spec(w_pool, (None, group, group), lambda i, g: (g, 0, 0)),
                  pl.BlockSpec((1, group), lambda i, g: (0, g))],
        out_specs=pl.BlockSpec((tm, group), lambda i, g: (i, g)),
        out_shape=jax.ShapeDtypeStruct((m, width), _BF16),
        compiler_params=_compiler_params("parallel", "arbitrary"),
        name="pool_proj",
    )(pooled, w_pool[0], pool_scale.reshape(1, width))


def _dft_rows(rows, n):
    idx = (rows[:, None] * jnp.arange(n, dtype=jnp.int32)[None, :]) % n
    ang = idx.astype(_F32) * (2.0 * math.pi / n)
    return jnp.cos(ang), jnp.sin(ang)


def _dft_tables(n):
    lo_rows = 64 if n % 64 == 0 and n > 64 else 1
    c_hi, s_hi = _dft_rows(jnp.arange(n // lo_rows, dtype=jnp.int32) * lo_rows, n)
    c_lo, s_lo = _dft_rows(jnp.arange(lo_rows, dtype=jnp.int32), n)
    cos = c_hi[:, None, :] * c_lo[None, :, :] - s_hi[:, None, :] * s_lo[None, :, :]
    sin = s_hi[:, None, :] * c_lo[None, :, :] + c_hi[:, None, :] * s_lo[None, :, :]
    return cos.reshape(n, n), sin.reshape(n, n)


def _seq_dft_kernel(c_ref, s_ref, xc_ref, xs_ref, o_ref, acc_ref, *, norm):
    k = pl.program_id(3)

    @pl.when(k == 0)
    def _():
        acc_ref[...] = jnp.zeros_like(acc_ref)

    acc_ref[...] += (jnp.dot(c_ref[...], xc_ref[...], preferred_element_type=_F32)
                     - jnp.dot(s_ref[...], xs_ref[...], preferred_element_type=_F32))

    @pl.when(k == pl.num_programs(3) - 1)
    def _():
        o_ref[...] = (acc_ref[...] * norm).astype(o_ref.dtype)


def _seq_dft(cos_s, sin_s, xcs, batch, seq, width, norm, *, tm=1024, tn=1024, tk=1024):
    tm, tn, tk = _tile(seq, tm), _tile(width, tn), _tile(seq, tk)
    nj = width // tn
    rows_k = seq // tk
    rows_m = seq // tm
    return pl.pallas_call(
        functools.partial(_seq_dft_kernel, norm=norm),
        grid=(batch, rows_m, nj, rows_k),
        in_specs=[pl.BlockSpec((tm, tk), lambda b, i, j, k: (i, k)),
                  pl.BlockSpec((tm, tk), lambda b, i, j, k: (i, k)),
                  pl.BlockSpec((tk, tn), lambda b, i, j, k: (b * rows_k + k, j)),
                  pl.BlockSpec((tk, tn), lambda b, i, j, k: (b * rows_k + k, nj + j))],
        out_specs=pl.BlockSpec((tm, tn), lambda b, i, j, k: (b * rows_m + i, j)),
        out_shape=jax.ShapeDtypeStruct((batch * seq, width), _BF16),
        scratch_shapes=[pltpu.VMEM((tm, tn), _F32)],
        compiler_params=_compiler_params("parallel", "parallel", "parallel", "arbitrary"),
        name="seq_dft",
    )(cos_s, sin_s, xcs, xcs)


_LOG2E = math.log2(math.e)
_NT_DIMS = (((1,), (1,)), ((), ()))


_RAMP_PARTS = 3
_SIGNS = (1.0, -1.0, 0.0)
_OWN = 2
_ATTN_ROW_BLOCK = 512


def _diff_attn_kernel(slope_ref, q_ref, k_ref, v_ref, lq1_ref, lk1_ref, lq2_ref, lk2_ref, gsub_ref,
                      o_ref, kaug_ref, qaug_ref, s_ref, p_ref, alpha_ref, rq_ref, rt_ref, sub_ref,
                      mx_ref, m_ref, l_ref, acc_ref, *, lambda_init, tk):
    h = pl.program_id(1)
    qi = pl.program_id(2)
    tq, dv = q_ref.shape
    dh = dv // 2
    seq = k_ref.shape[0]
    n_chunks = seq // tk
    n_col = tk // _LANES
    rb = min(tq, _ATTN_ROW_BLOCK)
    assert n_chunks % 2 == 0 and tq % rb == 0
    slope = slope_ref[h]
    c_own = (qi * tq) // tk

    @pl.when(qi == 0)
    def _():
        kb = min(tk, 256)

        def widen(i, carry):
            start = pl.multiple_of(i * kb, kb)
            local = (i % (tk // kb)) * kb + lax.broadcasted_iota(jnp.int32, (dh, kb), 1)
            rest = slope * local.astype(_F32)
            row = lax.broadcasted_iota(jnp.int32, (dh, kb), 0)
            ext = jnp.zeros((dh, kb), _F32)
            for part in range(_RAMP_PARTS):
                term = rest.astype(_BF16).astype(_F32)
                ext = jnp.where(row == part, term, ext)
                rest = rest - term
            for half in range(2):
                k_half = k_ref[pl.ds(start, kb), half * dh:(half + 1) * dh]
                kaug_ref[half, :dh, pl.ds(start, kb)] = k_half.T
                kaug_ref[half, dh:, pl.ds(start, kb)] = ext.astype(_BF16)
            return carry

        lax.fori_loop(0, seq // kb, widen, 0)

    lane_q = lax.broadcasted_iota(jnp.int32, (tq, _LANES), 1)
    for variant, sign in enumerate(_SIGNS):
        ext = jnp.where(lane_q < _RAMP_PARTS, sign, 0.0).astype(_BF16)
        for half in range(2):
            qaug_ref[variant, half] = jnp.concatenate([q_ref[:, half * dh:(half + 1) * dh], ext], axis=1)

    rq_ref[...] = slope * lax.broadcasted_iota(jnp.int32, (tq, _LANES), 0).astype(_F32)
    ramp_k = [slope * (lax.broadcasted_iota(jnp.int32, (_SUBLANES, _LANES), 1) + j * _LANES).astype(_F32)
              for j in range(n_col)]

    m_ref[...] = jnp.full_like(m_ref, _NEG_BIG)
    l_ref[...] = jnp.zeros_like(l_ref)
    acc_ref[...] = jnp.zeros_like(acc_ref)

    def chunk_of(j):
        rest = jnp.where(j - 1 >= c_own, j, j - 1)
        return jnp.where(j == 0, c_own, rest)

    def scores(c, slot, own, r0):
        start = pl.multiple_of(c * tk, tk)
        variant = _OWN if own else jnp.where(c < c_own, 0, 1)
        for half in range(2):
            s_ref[slot, half, pl.ds(r0, rb)] = jnp.dot(
                qaug_ref[variant, half, pl.ds(r0, rb)], kaug_ref[half, :, pl.ds(start, tk)],
                preferred_element_type=_F32)

    def values(c, slot, r0):
        start = pl.multiple_of(c * tk, tk)
        v = v_ref[pl.ds(start, tk), :]
        for half in range(2):
            alpha = alpha_ref[slot, half, pl.ds(r0, rb)]
            acc_ref[half, pl.ds(r0, rb)] = (
                jnp.concatenate([alpha] * (dv // _LANES), axis=1) * acc_ref[half, pl.ds(r0, rb)]
                + jnp.dot(p_ref[slot, half, pl.ds(r0, rb)], v, preferred_element_type=_F32))

    def softmax(c, slot, own, r0):
        block = pl.ds(r0, rb)
        shift = slope * (qi * tq - c * tk).astype(_F32)
        if own:
            rt_ref[block] = rq_ref[block] + shift
        else:
            sign = jnp.where(c < c_own, 1.0, -1.0).astype(_F32)
            rt_ref[block] = sign * (rq_ref[block] + shift)

        def group(g):
            return pl.ds(r0 + g * _SUBLANES, _SUBLANES)

        def biased(half, g, j, row_term):
            s = s_ref[slot, half, group(g), j * _LANES:(j + 1) * _LANES]
            if own:
                return s - jnp.abs(row_term - ramp_k[j])
            return s

        for half in range(2):
            for g in range(rb // _SUBLANES):
                row_term = rt_ref[group(g)] if own else None
                mx = biased(half, g, 0, row_term)
                for j in range(1, n_col):
                    mx = jnp.maximum(mx, biased(half, g, j, row_term))
                mx_ref[group(g)] = mx
            row_max = jnp.max(mx_ref[block], axis=-1, keepdims=True)
            m_old = m_ref[half, block]
            m_new = jnp.maximum(m_old, row_max if own else row_max - rt_ref[block])
            alpha_ref[slot, half, block] = jnp.exp2(m_old - m_new)
            m_ref[half, block] = m_new
            sub_ref[block] = m_new if own else m_new + rt_ref[block]
            for g in range(0, rb // _SUBLANES, 2):
                pair_rows = pl.ds(r0 + g * _SUBLANES, 2 * _SUBLANES)
                row_term = [rt_ref[group(g + i)] if own else None for i in range(2)]
                sub = [sub_ref[group(g + i)] for i in range(2)]
                tot = [None, None]
                for j in range(n_col):
                    ps = [jnp.exp2(biased(half, g + i, j, row_term[i]) - sub[i]) for i in range(2)]
                    tot = [p if t is None else t + p for t, p in zip(tot, ps)]
                    p_ref[slot, half, pair_rows, j * _LANES:(j + 1) * _LANES] = (
                        jnp.concatenate(ps, axis=0).astype(p_ref.dtype))
                for i in range(2):
                    l_ref[half, group(g + i)] = (alpha_ref[slot, half, group(g + i)] * l_ref[half, group(g + i)]
                                                 + tot[i])

    def phase(score_job=None, softmax_job=None, value_job=None):
        def block(r, carry):
            r0 = r * rb if isinstance(r, int) else pl.multiple_of(r * rb, rb)
            if score_job is not None:
                scores(*score_job, r0)
            if softmax_job is not None:
                softmax(*softmax_job, r0)
            if value_job is not None:
                values(*value_job, r0)
            return carry
        if tq == rb:
            block(0, 0)
        else:
            lax.fori_loop(0, tq // rb, block, 0)

    phase(score_job=(c_own, 0, True))
    phase(score_job=(chunk_of(1), 1, False), softmax_job=(c_own, 0, True))

    def pair(jj, carry):
        j = 2 * jj + 1
        phase((chunk_of(j + 1), 0, False), (chunk_of(j), 1, False), (chunk_of(j - 1), 0))
        phase((chunk_of(j + 2), 1, False), (chunk_of(j + 1), 0, False), (chunk_of(j), 1))
        return carry

    lax.fori_loop(0, (n_chunks - 2) // 2, pair, 0)
    phase(softmax_job=(chunk_of(n_chunks - 1), 1, False), value_job=(chunk_of(n_chunks - 2), 0))
    phase(value_job=(chunk_of(n_chunks - 1), 1))

    lam = (jnp.exp(jnp.sum(lq1_ref[...] * lk1_ref[...], keepdims=True))
           - jnp.exp(jnp.sum(lq2_ref[...] * lk2_ref[...], keepdims=True)) + lambda_init)
    l0 = jnp.sum(l_ref[0], axis=-1, keepdims=True)
    l1 = jnp.sum(l_ref[1], axis=-1, keepdims=True)
    o = acc_ref[0] / l0 - lam * (acc_ref[1] / l1)
    o_ref[...] = (_rms(o, gsub_ref[...]) * (1.0 - lambda_init)).astype(o_ref.dtype)


def _diff_attn(qkv, batch, seq, lq1, lk1, lq2, lk2, g_sub, lambda_init, *, tq=512, tk=512):
    m, d3 = qkv.shape
    d = d3 // 3
    heads = _ATTN_HEADS
    dv = d // heads
    dh = dv // 2
    tq, tk = _tile(seq, tq), _tile(seq, tk)
    assert tk % tq == 0
    nq = seq // tq
    slopes = jnp.exp2(-8.0 * jnp.arange(1, heads + 1, dtype=_F32) / heads) * _LOG2E
    lam_spec = pl.BlockSpec((1, dh), lambda b, h, i: (0, 0))
    stat = pltpu.VMEM((tq, _LANES), _F32)
    return pl.pallas_call(
        functools.partial(_diff_attn_kernel, lambda_init=lambda_init, tk=tk),
        grid=(batch, heads, nq),
        in_specs=[pl.BlockSpec(memory_space=pltpu.SMEM),
                  pl.BlockSpec((tq, dv), lambda b, h, i: (b * nq + i, h)),
                  pl.BlockSpec((seq, dv), lambda b, h, i: (b, heads + h)),
                  pl.BlockSpec((seq, dv), lambda b, h, i: (b, 2 * heads + h)),
                  lam_spec, lam_spec, lam_spec, lam_spec,
                  pl.BlockSpec((1, dv), lambda b, h, i: (0, 0))],
        out_specs=pl.BlockSpec((tq, dv), lambda b, h, i: (b * nq + i, h)),
        out_shape=jax.ShapeDtypeStruct((m, d), _BF16),
        scratch_shapes=[pltpu.VMEM((2, 2 * dh, seq), _BF16),
                        pltpu.VMEM((len(_SIGNS), 2, tq, 2 * dh), _BF16),
                        pltpu.VMEM((2, 2, tq, tk), _F32),
                        pltpu.VMEM((2, 2, tq, tk), _BF16),
                        pltpu.VMEM((2, 2, tq, _LANES), _F32),
                        stat, stat, stat, stat,
                        pltpu.VMEM((2, tq, _LANES), _F32),
                        pltpu.VMEM((2, tq, _LANES), _F32),
                        pltpu.VMEM((2, tq, dv), _F32)],
        compiler_params=_compiler_params("parallel", "parallel", "arbitrary"),
        name="diff_attn",
    )(slopes, qkv, qkv, qkv, lq1.reshape(1, dh), lk1.reshape(1, dh), lq2.reshape(1, dh),
      lk2.reshape(1, dh), g_sub.reshape(1, dv))


def _ffn(h, wg, wu, wd):
    return _mm(_gate_up(h, wg, wu), wd, _F32, tm=1024, tn=512)


def _pool_fourier(h, batch, seq, w_in, w_pool, pool_scale, w_fourier, w_out, dft):
    cos_s, sin_s, chan_dft = dft
    pool_width = pool_scale.shape[0]
    four_width = w_fourier[0].shape[-1]
    u = _mm(h, w_in, _F32)
    a = _pool_proj(_pool(u, seq, pool_width), w_pool, pool_scale)
    xcs = _mm(u, (chan_dft, ()), _BF16, a_col_block=(pool_width // four_width, four_width))
    head_dim = four_width // _FOURIER_HEADS
    f = _seq_dft(cos_s, sin_s, xcs, batch, seq, four_width, 1.0 / math.sqrt(seq * head_dim))
    f = _mm(f, w_fourier, _BF16)
    return _mm_cat(a, f, w_out)


def kernel(x, norm_pre, norm_post, w_ffn_gate, w_ffn_up, w_ffn_down, w_mix_in, w_pool, pool_scale,
           w_fourier, w_mix_out, w_qkv, w_attn_out, lambda_q1, lambda_k1, lambda_q2, lambda_k2,
           subln_gain):
    batch, seq, d = x.shape
    depth = norm_pre.shape[0]
    m = batch * seq
    x = x.reshape(m, d)
    bf = lambda w: w.astype(_BF16)

    four_width = w_fourier.shape[1]
    head_dim = four_width // _FOURIER_HEADS
    cos_s, sin_s = _dft_tables(seq)
    cos_c, sin_c = _dft_tables(head_dim)
    eye = jnp.eye(_FOURIER_HEADS, dtype=_F32)
    chan_dft = jnp.concatenate([jnp.kron(eye, cos_c), jnp.kron(eye, sin_c)], axis=1)
    dft = (bf(cos_s), bf(sin_s), bf(chan_dft))

    attn_head_dim = d // (2 * _ATTN_HEADS)
    w_gate, w_up, w_down = bf(w_ffn_gate), bf(w_ffn_up), bf(w_ffn_down)
    w_in, w_pl, w_fr, w_out = bf(w_mix_in), bf(w_pool), bf(w_fourier), bf(w_mix_out)
    w_qk, w_ao = bf(w_qkv), bf(w_attn_out)

    h = _prenorm(x, norm_pre[0, 0])
    for l in range(depth):
        i = l // 2
        y = _ffn(h, (w_gate, (l, 0)), (w_up, (l, 0)), (w_down, (l, 0)))
        x, h = _residual(y, x, norm_post[l, 0], norm_pre[l, 1], 0.5)
        if l % 2 == 0:
            y = _pool_fourier(h, batch, seq, (w_in, (i,)), (w_pl, (i,)), pool_scale[i],
                              (w_fr, (i,)), (w_out, (i,)), dft)
        else:
            lambda_init = 0.8 - 0.6 * math.exp(-0.3 * l)
            qkv = _mm(h, (w_qk, (i,)), _BF16, scaled_cols=d, scale=attn_head_dim ** -0.5 * _LOG2E)
            o = _diff_attn(qkv, batch, seq, lambda_q1[i], lambda_k1[i], lambda_q2[i], lambda_k2[i],
                           subln_gain[i], lambda_init)
            y = _mm(o, (w_ao, (i,)), _F32)
        x, h = _residual(y, x, norm_post[l, 1], norm_pre[l, 2], 1.0)
        y = _ffn(h, (w_gate, (l, 1)), (w_up, (l, 1)), (w_down, (l, 1)))
        g_next = norm_pre[l + 1, 0] if l + 1 < depth else None
        x, h = _residual(y, x, norm_post[l, 2], g_next, 0.5)
    return x.reshape(batch, seq, d)
```

```python
import functools
import math

import jax
import jax.numpy as jnp
from jax import lax
from jax.experimental import pallas as pl
from jax.experimental.pallas import tpu as pltpu

_V7X_VMEM_LIMIT_BYTES = 56 * 1024 * 1024
_SUBLANES = 8
_LANES = 128

_EPS = 1e-6
_POOL_WINDOWS = (2, 4, 8, 16)
_FOURIER_HEADS = 4
_ATTN_HEADS = 16
_NEG_BIG = -1e30

_BF16 = jnp.bfloat16
_F32 = jnp.float32


def _compiler_params(*semantics, flags=None):
    return pltpu.CompilerParams(dimension_semantics=semantics,
                                vmem_limit_bytes=_V7X_VMEM_LIMIT_BYTES, flags=flags)


def _tile(n, want):
    t = min(n, want)
    assert n % t == 0, (n, want)
    return t


def _rms(x, g):
    return x * lax.rsqrt(jnp.mean(x * x, axis=-1, keepdims=True) + _EPS) * g


def _prenorm_kernel(x_ref, g_ref, h_ref):
    h_ref[...] = _rms(x_ref[...], g_ref[...]).astype(h_ref.dtype)


def _prenorm(x, g):
    m, d = x.shape
    tm = _tile(m, 256)
    return pl.pallas_call(
        _prenorm_kernel,
        grid=(m // tm,),
        in_specs=[pl.BlockSpec((tm, d), lambda i: (i, 0)),
                  pl.BlockSpec((1, d), lambda i: (0, 0))],
        out_specs=pl.BlockSpec((tm, d), lambda i: (i, 0)),
        out_shape=jax.ShapeDtypeStruct((m, d), _BF16),
        compiler_params=_compiler_params("parallel"),
        name="prenorm",
    )(x, g.reshape(1, d))


def _residual_kernel(y_ref, x_ref, gpost_ref, gpre_ref, xo_ref, h_ref, *, step):
    xn = x_ref[...] + step * _rms(y_ref[...], gpost_ref[...])
    xo_ref[...] = xn
    h_ref[...] = _rms(xn, gpre_ref[...]).astype(h_ref.dtype)


def _residual_last_kernel(y_ref, x_ref, gpost_ref, xo_ref, *, step):
    xo_ref[...] = x_ref[...] + step * _rms(y_ref[...], gpost_ref[...])


def _residual(y, x, g_post, g_pre_next, step):
    m, d = x.shape
    tm = _tile(m, 256)
    row = pl.BlockSpec((tm, d), lambda i: (i, 0))
    vec = pl.BlockSpec((1, d), lambda i: (0, 0))
    if g_pre_next is None:
        return pl.pallas_call(
            functools.partial(_residual_last_kernel, step=step),
            grid=(m // tm,),
            in_specs=[row, row, vec],
            out_specs=row,
            out_shape=jax.ShapeDtypeStruct((m, d), _F32),
            compiler_params=_compiler_params("parallel"),
            name="residual_last",
        )(y, x, g_post.reshape(1, d)), None
    return pl.pallas_call(
        functools.partial(_residual_kernel, step=step),
        grid=(m // tm,),
        in_specs=[row, row, vec, vec],
        out_specs=[row, row],
        out_shape=[jax.ShapeDtypeStruct((m, d), _F32), jax.ShapeDtypeStruct((m, d), _BF16)],
        compiler_params=_compiler_params("parallel"),
        name="residual",
    )(y, x, g_post.reshape(1, d), g_pre_next.reshape(1, d))


def _weight_spec(w, block, index):
    lead = w[1]
    return pl.BlockSpec((None,) * len(lead) + block, lambda *g: lead + index(*g))


def _cast_specs(cast_srcs, steps, step_of):
    ins, outs, shapes, arrays = [], [], [], []
    for arr, lead in cast_srcs:
        rows, cols = arr.shape[-2:]
        slab = rows // steps
        assert rows == slab * steps and slab % (2 * _SUBLANES) == 0, (rows, steps)
        ins.append(pl.BlockSpec((None,) * len(lead) + (slab, cols),
                                lambda *g, lead=lead: lead + (step_of(*g), 0)))
        outs.append(pl.BlockSpec((slab, cols), lambda *g: (step_of(*g), 0)))
        shapes.append(jax.ShapeDtypeStruct((rows, cols), _BF16))
        arrays.append(arr)
    return ins, outs, shapes, arrays


def _cast_slabs(refs):
    n = len(refs) // 2
    for src_ref, dst_ref in zip(refs[:n], refs[n:]):
        dst_ref[...] = src_ref[...].astype(dst_ref.dtype)


def _mm_kernel(a_ref, w_ref, *refs, n_cast, scaled_col_blocks, scale):
    o_ref = refs[n_cast]
    acc = jnp.dot(a_ref[...].astype(_BF16), w_ref[...], preferred_element_type=_F32)
    if scaled_col_blocks:
        acc = acc * jnp.where(pl.program_id(1) < scaled_col_blocks, scale, 1.0)
    o_ref[...] = acc.astype(o_ref.dtype)
    _cast_slabs(refs[:n_cast] + refs[n_cast + 1:])


def _mm(a, w, out_dtype, *, tm=1024, tn=1024, a_col_block=None, scaled_cols=0, scale=1.0, cast_srcs=()):
    m = a.shape[0]
    k, n = w[0].shape[-2:]
    tm, tn = _tile(m, tm), _tile(n, tn)
    a_col = 0
    if a_col_block is not None:
        a_col, width = a_col_block
        assert width == k
    else:
        assert a.shape[1] == k
    assert scaled_cols % tn == 0
    nj = n // tn
    c_in, c_out, c_shape, c_arr = _cast_specs(cast_srcs, (m // tm) * nj, lambda i, j: i * nj + j)
    out = pl.pallas_call(
        functools.partial(_mm_kernel, n_cast=len(c_arr), scaled_col_blocks=scaled_cols // tn, scale=scale),
        grid=(m // tm, nj),
        in_specs=[pl.BlockSpec((tm, k), lambda i, j: (i, a_col)),
                  _weight_spec(w, (k, tn), lambda i, j: (0, j))] + c_in,
        out_specs=[pl.BlockSpec((tm, tn), lambda i, j: (i, j))] + c_out,
        out_shape=[jax.ShapeDtypeStruct((m, n), out_dtype)] + c_shape,
        compiler_params=_compiler_params("parallel", "arbitrary"),
        name="matmul",
    )(a, w[0], *c_arr)
    return out if c_arr else out[0]


def _gate_up_kernel(h_ref, wg_ref, wu_ref, *refs, n_cast):
    o_ref = refs[n_cast]
    h = h_ref[...]
    g = jnp.dot(h, wg_ref[...], preferred_element_type=_F32)
    u = jnp.dot(h, wu_ref[...], preferred_element_type=_F32)
    o_ref[...] = (g * jax.nn.sigmoid(g) * u).astype(o_ref.dtype)
    _cast_slabs(refs[:n_cast] + refs[n_cast + 1:])


def _gate_up(h, wg, wu, *, tm=1024, tn=512, cast_srcs=()):
    m, k = h.shape
    n = wg[0].shape[-1]
    tm, tn = _tile(m, tm), _tile(n, tn)
    nj = n // tn
    c_in, c_out, c_shape, c_arr = _cast_specs(cast_srcs, (m // tm) * nj, lambda i, j: i * nj + j)
    out = pl.pallas_call(
        functools.partial(_gate_up_kernel, n_cast=len(c_arr)),
        grid=(m // tm, nj),
        in_specs=[pl.BlockSpec((tm, k), lambda i, j: (i, 0)),
                  _weight_spec(wg, (k, tn), lambda i, j: (0, j)),
                  _weight_spec(wu, (k, tn), lambda i, j: (0, j))] + c_in,
        out_specs=[pl.BlockSpec((tm, tn), lambda i, j: (i, j))] + c_out,
        out_shape=[jax.ShapeDtypeStruct((m, n), _BF16)] + c_shape,
        compiler_params=_compiler_params("parallel", "arbitrary"),
        name="gate_up",
    )(h, wg[0], wu[0], *c_arr)
    return out if c_arr else out[0]


def _mm_cat_kernel(a_ref, f_ref, w_ref, o_ref):
    ka = a_ref.shape[1]
    acc = jnp.dot(a_ref[...], w_ref[:ka, :], preferred_element_type=_F32)
    acc = acc + jnp.dot(f_ref[...], w_ref[ka:, :], preferred_element_type=_F32)
    o_ref[...] = acc.astype(o_ref.dtype)


def _mm_cat(a, f, w, *, tm=1024, tn=1024):
    m, ka = a.shape
    kf = f.shape[1]
    k, n = w[0].shape[-2:]
    assert ka + kf == k
    tm, tn = _tile(m, tm), _tile(n, tn)
    return pl.pallas_call(
        _mm_cat_kernel,
        grid=(m // tm, n // tn),
        in_specs=[pl.BlockSpec((tm, ka), lambda i, j: (i, 0)),
                  pl.BlockSpec((tm, kf), lambda i, j: (i, 0)),
                  _weight_spec(w, (k, tn), lambda i, j: (0, j))],
        out_specs=pl.BlockSpec((tm, tn), lambda i, j: (i, j)),
        out_shape=jax.ShapeDtypeStruct((m, n), _F32),
        compiler_params=_compiler_params("parallel", "arbitrary"),
        name="matmul_cat",
    )(a, f, w[0])


_POOL_HALO = 8
_POOL_ROWS = 256


def _pool_kernel(prev_ref, x_ref, next_ref, o_ref, *, seq, group):
    ts = x_ref.shape[0]
    ext_rows = ts + 2 * _POOL_HALO
    tile = pl.program_id(0) % (seq // ts)
    t = tile * ts + lax.broadcasted_iota(jnp.int32, (ts, 1), 0)
    first, last = tile == 0, tile == seq // ts - 1
    for g, w in enumerate(_POOL_WINDOWS):
        cols = slice(g * group, (g + 1) * group)
        x = x_ref[:, cols]
        run = jnp.concatenate([jnp.where(first, 0.0, prev_ref[:, cols]), x,
                               jnp.where(last, 0.0, next_ref[:, cols])], axis=0)
        span = 1
        while span < w:
            run = run + pltpu.roll(run, ext_rows - span, 0)
            span *= 2
        total = pltpu.roll(run, w // 2, 0)[_POOL_HALO:_POOL_HALO + ts]
        lo = jnp.clip(t - w // 2, 0, seq)
        hi = jnp.clip(t - w // 2 + w, 0, seq)
        o_ref[:, cols] = (total / (hi - lo).astype(_F32) - x).astype(o_ref.dtype)


def _pool(u, seq, pool_width):
    m = u.shape[0]
    group = pool_width // len(_POOL_WINDOWS)
    assert max(_POOL_WINDOWS) // 2 <= _POOL_HALO
    ts = _tile(seq, _POOL_ROWS)
    per = ts // _POOL_HALO
    last = m // _POOL_HALO - 1
    return pl.pallas_call(
        functools.partial(_pool_kernel, seq=seq, group=group),
        grid=(m // ts,),
        in_specs=[pl.BlockSpec((_POOL_HALO, pool_width), lambda i: (jnp.maximum(i * per - 1, 0), 0)),
                  pl.BlockSpec((ts, pool_width), lambda i: (i, 0)),
                  pl.BlockSpec((_POOL_HALO, pool_width), lambda i: (jnp.minimum((i + 1) * per, last), 0))],
        out_specs=pl.BlockSpec((ts, pool_width), lambda i: (i, 0)),
        out_shape=jax.ShapeDtypeStruct((m, pool_width), _BF16),
        compiler_params=_compiler_params("parallel"),
        name="pool",
    )(u, u, u)


def _pool_proj_kernel(p_ref, w_ref, s_ref, o_ref):
    acc = jnp.dot(p_ref[...], w_ref[...], preferred_element_type=_F32)
    o_ref[...] = (acc * s_ref[...]).astype(o_ref.dtype)


def _pool_proj(pooled, w_pool, pool_scale, *, tm=1024):
    m, width = pooled.shape
    groups, group, _ = w_pool[0].shape[-3:]
    tm = _tile(m, tm)
    return pl.pallas_call(
        _pool_proj_kernel,
        grid=(m // tm, groups),
        in_specs=[pl.BlockSpec((tm, group), lambda i, g: (i, g)),
                  _weight_spec(w_pool, (None, group, group), lambda i, g: (g, 0, 0)),
                  pl.BlockSpec((1, group), lambda i, g: (0, g))],
        out_specs=pl.BlockSpec((tm, group), lambda i, g: (i, g)),
        out_shape=jax.ShapeDtypeStruct((m, width), _BF16),
        compiler_params=_compiler_params("parallel", "arbitrary"),
        name="pool_proj",
    )(pooled, w_pool[0], pool_scale.reshape(1, width))


def _dft_rows(rows, n):
    idx = (rows[:, None] * jnp.arange(n, dtype=jnp.int32)[None, :]) % n
    ang = idx.astype(_F32) * (2.0 * math.pi / n)
    return jnp.cos(ang), jnp.sin(ang)


def _dft_tables(n):
    lo_rows = 64 if n % 64 == 0 and n > 64 else 1
    c_hi, s_hi = _dft_rows(jnp.arange(n // lo_rows, dtype=jnp.int32) * lo_rows, n)
    c_lo, s_lo = _dft_rows(jnp.arange(lo_rows, dtype=jnp.int32), n)
    cos = c_hi[:, None, :] * c_lo[None, :, :] - s_hi[:, None, :] * s_lo[None, :, :]
    sin = s_hi[:, None, :] * c_lo[None, :, :] + c_hi[:, None, :] * s_lo[None, :, :]
    return cos.reshape(n, n), sin.reshape(n, n)


def _seq_dft_kernel(c_ref, s_ref, xc_ref, xs_ref, o_ref, acc_ref, *, norm):
    k = pl.program_id(3)

    @pl.when(k == 0)
    def _():
        acc_ref[...] = jnp.zeros_like(acc_ref)

    acc_ref[...] += (jnp.dot(c_ref[...], xc_ref[...], preferred_element_type=_F32)
                     - jnp.dot(s_ref[...], xs_ref[...], preferred_element_type=_F32))

    @pl.when(k == pl.num_programs(3) - 1)
    def _():
        o_ref[...] = (acc_ref[...] * norm).astype(o_ref.dtype)


def _seq_dft(cos_s, sin_s, xcs, batch, seq, width, norm, *, tm=1024, tn=1024, tk=1024):
    tm, tn, tk = _tile(seq, tm), _tile(width, tn), _tile(seq, tk)
    nj = width // tn
    rows_k = seq // tk
    rows_m = seq // tm
    return pl.pallas_call(
        functools.partial(_seq_dft_kernel, norm=norm),
        grid=(batch, rows_m, nj, rows_k),
        in_specs=[pl.BlockSpec((tm, tk), lambda b, i, j, k: (i, k)),
                  pl.BlockSpec((tm, tk), lambda b, i, j, k: (i, k)),
                  pl.BlockSpec((tk, tn), lambda b, i, j, k: (b * rows_k + k, j)),
                  pl.BlockSpec((tk, tn), lambda b, i, j, k: (b * rows_k + k, nj + j))],
        out_specs=pl.BlockSpec((tm, tn), lambda b, i, j, k: (b * rows_m + i, j)),
        out_shape=jax.ShapeDtypeStruct((batch * seq, width), _BF16),
        scratch_shapes=[pltpu.VMEM((tm, tn), _F32)],
        compiler_params=_compiler_params("parallel", "parallel", "parallel", "arbitrary"),
        name="seq_dft",
    )(cos_s, sin_s, xcs, xcs)


_LOG2E = math.log2(math.e)
_NT_DIMS = (((1,), (1,)), ((), ()))


_RAMP_PARTS = 3
_SIGNS = (1.0, -1.0, 0.0)
_OWN = 2
_ATTN_ROW_BLOCK = 512


def _diff_attn_kernel(slope_ref, q_ref, k_ref, v_ref, lq1_ref, lk1_ref, lq2_ref, lk2_ref, gsub_ref,
                      o_ref, kaug_ref, qaug_ref, s_ref, p_ref, alpha_ref, rq_ref, rt_ref, sub_ref,
                      mx_ref, m_ref, l_ref, acc_ref, *, lambda_init, tk):
    h = pl.program_id(1)
    qi = pl.program_id(2)
    tq, dv = q_ref.shape
    dh = dv // 2
    seq = k_ref.shape[0]
    n_chunks = seq // tk
    n_col = tk // _LANES
    rb = min(tq, _ATTN_ROW_BLOCK)
    assert n_chunks % 2 == 0 and tq % rb == 0
    slope = slope_ref[h]
    c_own = (qi * tq) // tk

    @pl.when(qi == 0)
    def _():
        kb = min(tk, 256)

        def widen(i, carry):
            start = pl.multiple_of(i * kb, kb)
            local = (i % (tk // kb)) * kb + lax.broadcasted_iota(jnp.int32, (dh, kb), 1)
            rest = slope * local.astype(_F32)
            row = lax.broadcasted_iota(jnp.int32, (dh, kb), 0)
            ext = jnp.zeros((dh, kb), _F32)
            for part in range(_RAMP_PARTS):
                term = rest.astype(_BF16).astype(_F32)
                ext = jnp.where(row == part, term, ext)
                rest = rest - term
            for half in range(2):
                k_half = k_ref[pl.ds(start, kb), half * dh:(half + 1) * dh]
                kaug_ref[half, :dh, pl.ds(start, kb)] = k_half.T
                kaug_ref[half, dh:, pl.ds(start, kb)] = ext.astype(_BF16)
            return carry

        lax.fori_loop(0, seq // kb, widen, 0)

    lane_q = lax.broadcasted_iota(jnp.int32, (tq, _LANES), 1)
    for variant, sign in enumerate(_SIGNS):
        ext = jnp.where(lane_q < _RAMP_PARTS, sign, 0.0).astype(_BF16)
        for half in range(2):
            qaug_ref[variant, half] = jnp.concatenate([q_ref[:, half * dh:(half + 1) * dh], ext], axis=1)

    rq_ref[...] = slope * lax.broadcasted_iota(jnp.int32, (tq, _LANES), 0).astype(_F32)
    ramp_k = [slope * (lax.broadcasted_iota(jnp.int32, (_SUBLANES, _LANES), 1) + j * _LANES).astype(_F32)
              for j in range(n_col)]

    m_ref[...] = jnp.full_like(m_ref, _NEG_BIG)
    l_ref[...] = jnp.zeros_like(l_ref)
    acc_ref[...] = jnp.zeros_like(acc_ref)

    def chunk_of(j):
        rest = jnp.where(j - 1 >= c_own, j, j - 1)
        return jnp.where(j == 0, c_own, rest)

    def scores(c, slot, own, r0):
        start = pl.multiple_of(c * tk, tk)
        variant = _OWN if own else jnp.where(c < c_own, 0, 1)
        for half in range(2):
            s_ref[slot, half, pl.ds(r0, rb)] = jnp.dot(
                qaug_ref[variant, half, pl.ds(r0, rb)], kaug_ref[half, :, pl.ds(start, tk)],
                preferred_element_type=_F32)

    def values(c, slot, r0):
        start = pl.multiple_of(c * tk, tk)
        v = v_ref[pl.ds(start, tk), :]
        for half in range(2):
            alpha = alpha_ref[slot, half, pl.ds(r0, rb)]
            acc_ref[half, pl.ds(r0, rb)] = (
                jnp.concatenate([alpha] * (dv // _LANES), axis=1) * acc_ref[half, pl.ds(r0, rb)]
                + jnp.dot(p_ref[slot, half, pl.ds(r0, rb)], v, preferred_element_type=_F32))

    def softmax(c, slot, own, r0):
        block = pl.ds(r0, rb)
        shift = slope * (qi * tq - c * tk).astype(_F32)
        if own:
            rt_ref[block] = rq_ref[block] + shift
        else:
            sign = jnp.where(c < c_own, 1.0, -1.0).astype(_F32)
            rt_ref[block] = sign * (rq_ref[block] + shift)

        def group(g):
            return pl.ds(r0 + g * _SUBLANES, _SUBLANES)

        def biased(half, g, j, row_term):
            s = s_ref[slot, half, group(g), j * _LANES:(j + 1) * _LANES]
            if own:
                return s - jnp.abs(row_term - ramp_k[j])
            return s

        for half in range(2):
            for g in range(rb // _SUBLANES):
                row_term = rt_ref[group(g)] if own else None
                mx = biased(half, g, 0, row_term)
                for j in range(1, n_col):
                    mx = jnp.maximum(mx, biased(half, g, j, row_term))
                mx_ref[group(g)] = mx
            row_max = jnp.max(mx_ref[block], axis=-1, keepdims=True)
            m_old = m_ref[half, block]
            m_new = jnp.maximum(m_old, row_max if own else row_max - rt_ref[block])
            alpha_ref[slot, half, block] = jnp.exp2(m_old - m_new)
            m_ref[half, block] = m_new
            sub_ref[block] = m_new if own else m_new + rt_ref[block]
            for g in range(0, rb // _SUBLANES, 2):
                pair_rows = pl.ds(r0 + g * _SUBLANES, 2 * _SUBLANES)
                row_term = [rt_ref[group(g + i)] if own else None for i in range(2)]
                sub = [sub_ref[group(g + i)] for i in range(2)]
                tot = [None, None]
                for j in range(n_col):
                    ps = [jnp.exp2(biased(half, g + i, j, row_term[i]) - sub[i]) for i in range(2)]
                    tot = [p if t is None else t + p for t, p in zip(tot, ps)]
                    p_ref[slot, half, pair_rows, j * _LANES:(j + 1) * _LANES] = (
                        jnp.concatenate(ps, axis=0).astype(p_ref.dtype))
                for i in range(2):
                    l_ref[half, group(g + i)] = (alpha_ref[slot, half, group(g + i)] * l_ref[half, group(g + i)]
                                                 + tot[i])

    def phase(score_job=None, softmax_job=None, value_job=None):
        def block(r, carry):
            r0 = r * rb if isinstance(r, int) else pl.multiple_of(r * rb, rb)
            if score_job is not None:
                scores(*score_job, r0)
            if softmax_job is not None:
                softmax(*softmax_job, r0)
            if value_job is not None:
                values(*value_job, r0)
            return carry
        if tq == rb:
            block(0, 0)
        else:
            lax.fori_loop(0, tq // rb, block, 0)

    phase(score_job=(c_own, 0, True))
    phase(score_job=(chunk_of(1), 1, False), softmax_job=(c_own, 0, True))

    def pair(jj, carry):
        j = 2 * jj + 1
        phase((chunk_of(j + 1), 0, False), (chunk_of(j), 1, False), (chunk_of(j - 1), 0))
        phase((chunk_of(j + 2), 1, False), (chunk_of(j + 1), 0, False), (chunk_of(j), 1))
        return carry

    lax.fori_loop(0, (n_chunks - 2) // 2, pair, 0)
    phase(softmax_job=(chunk_of(n_chunks - 1), 1, False), value_job=(chunk_of(n_chunks - 2), 0))
    phase(value_job=(chunk_of(n_chunks - 1), 1))

    lam = (jnp.exp(jnp.sum(lq1_ref[...] * lk1_ref[...], keepdims=True))
           - jnp.exp(jnp.sum(lq2_ref[...] * lk2_ref[...], keepdims=True)) + lambda_init)
    l0 = jnp.sum(l_ref[0], axis=-1, keepdims=True)
    l1 = jnp.sum(l_ref[1], axis=-1, keepdims=True)
    o = acc_ref[0] / l0 - lam * (acc_ref[1] / l1)
    o_ref[...] = (_rms(o, gsub_ref[...]) * (1.0 - lambda_init)).astype(o_ref.dtype)


def _diff_attn(qkv, batch, seq, lq1, lk1, lq2, lk2, g_sub, lambda_init, *, tq=512, tk=512):
    m, d3 = qkv.shape
    d = d3 // 3
    heads = _ATTN_HEADS
    dv = d // heads
    dh = dv // 2
    tq, tk = _tile(seq, tq), _tile(seq, tk)
    assert tk % tq == 0
    nq = seq // tq
    slopes = jnp.exp2(-8.0 * jnp.arange(1, heads + 1, dtype=_F32) / heads) * _LOG2E
    lam_spec = pl.BlockSpec((1, dh), lambda b, h, i: (0, 0))
    stat = pltpu.VMEM((tq, _LANES), _F32)
    return pl.pallas_call(
        functools.partial(_diff_attn_kernel, lambda_init=lambda_init, tk=tk),
        grid=(batch, heads, nq),
        in_specs=[pl.BlockSpec(memory_space=pltpu.SMEM),
                  pl.BlockSpec((tq, dv), lambda b, h, i: (b * nq + i, h)),
                  pl.BlockSpec((seq, dv), lambda b, h, i: (b, heads + h)),
                  pl.BlockSpec((seq, dv), lambda b, h, i: (b, 2 * heads + h)),
                  lam_spec, lam_spec, lam_spec, lam_spec,
                  pl.BlockSpec((1, dv), lambda b, h, i: (0, 0))],
        out_specs=pl.BlockSpec((tq, dv), lambda b, h, i: (b * nq + i, h)),
        out_shape=jax.ShapeDtypeStruct((m, d), _BF16),
        scratch_shapes=[pltpu.VMEM((2, 2 * dh, seq), _BF16),
                        pltpu.VMEM((len(_SIGNS), 2, tq, 2 * dh), _BF16),
                        pltpu.VMEM((2, 2, tq, tk), _F32),
                        pltpu.VMEM((2, 2, tq, tk), _BF16),
                        pltpu.VMEM((2, 2, tq, _LANES), _F32),
                        stat, stat, stat, stat,
                        pltpu.VMEM((2, tq, _LANES), _F32),
                        pltpu.VMEM((2, tq, _LANES), _F32),
                        pltpu.VMEM((2, tq, dv), _F32)],
        compiler_params=_compiler_params("parallel", "parallel", "arbitrary"),
        name="diff_attn",
    )(slopes, qkv, qkv, qkv, lq1.reshape(1, dh), lk1.reshape(1, dh), lq2.reshape(1, dh),
      lk2.reshape(1, dh), g_sub.reshape(1, dv))


def _ffn(h, wg, wu, wd_f32, next_f32):
    a, wd = _gate_up(h, wg, wu, cast_srcs=(wd_f32,))
    y, *cast = _mm(a, (wd, ()), _F32, tm=1024, tn=512, cast_srcs=next_f32) if next_f32 else [
        _mm(a, (wd, ()), _F32, tm=1024, tn=512)]
    return y, [(w, ()) for w in cast]


def _pool_fourier(h, batch, seq, w_in, w_pool, pool_scale, w_fourier, w_out, dft):
    cos_s, sin_s, chan_dft = dft
    pool_width = pool_scale.shape[0]
    four_width = w_fourier[0].shape[-1]
    u = _mm(h, w_in, _F32)
    a = _pool_proj(_pool(u, seq, pool_width), w_pool, pool_scale)
    xcs = _mm(u, (chan_dft, ()), _BF16, a_col_block=(pool_width // four_width, four_width))
    head_dim = four_width // _FOURIER_HEADS
    f = _seq_dft(cos_s, sin_s, xcs, batch, seq, four_width, 1.0 / math.sqrt(seq * head_dim))
    f = _mm(f, w_fourier, _BF16)
    return _mm_cat(a, f, w_out)


def kernel(x, norm_pre, norm_post, w_ffn_gate, w_ffn_up, w_ffn_down, w_mix_in, w_pool, pool_scale,
           w_fourier, w_mix_out, w_qkv, w_attn_out, lambda_q1, lambda_k1, lambda_q2, lambda_k2,
           subln_gain):
    batch, seq, d = x.shape
    depth = norm_pre.shape[0]
    m = batch * seq
    x = x.reshape(m, d)
    bf = lambda w: w.astype(_BF16)

    four_width = w_fourier.shape[1]
    head_dim = four_width // _FOURIER_HEADS
    cos_s, sin_s = _dft_tables(seq)
    cos_c, sin_c = _dft_tables(head_dim)
    eye = jnp.eye(_FOURIER_HEADS, dtype=_F32)
    chan_dft = jnp.concatenate([jnp.kron(eye, cos_c), jnp.kron(eye, sin_c)], axis=1)
    dft = (bf(cos_s), bf(sin_s), bf(chan_dft))

    attn_head_dim = d // (2 * _ATTN_HEADS)
    w_pl, w_fr = bf(w_pool), bf(w_fourier)
    wg, wu = (bf(w_ffn_gate[0, 0]), ()), (bf(w_ffn_up[0, 0]), ())

    h = _prenorm(x, norm_pre[0, 0])
    for l in range(depth):
        i = l // 2
        mixer = ([(w_mix_in, (i,)), (w_mix_out, (i,))] if l % 2 == 0 else
                 [(w_qkv, (i,)), (w_attn_out, (i,))])
        y, (wg, wu, w_first, w_last) = _ffn(h, wg, wu, (w_ffn_down, (l, 0)),
                                            [(w_ffn_gate, (l, 1)), (w_ffn_up, (l, 1))] + mixer)
        x, h = _residual(y, x, norm_post[l, 0], norm_pre[l, 1], 0.5)
        if l % 2 == 0:
            y = _pool_fourier(h, batch, seq, w_first, (w_pl, (i,)), pool_scale[i],
                              (w_fr, (i,)), w_last, dft)
        else:
            lambda_init = 0.8 - 0.6 * math.exp(-0.3 * l)
            qkv = _mm(h, w_first, _BF16, scaled_cols=d, scale=attn_head_dim ** -0.5 * _LOG2E)
            o = _diff_attn(qkv, batch, seq, lambda_q1[i], lambda_k1[i], lambda_q2[i], lambda_k2[i],
                           subln_gain[i], lambda_init)
            y = _mm(o, w_last, _F32)
        x, h = _residual(y, x, norm_post[l, 1], norm_pre[l, 2], 1.0)
        following = [(w_ffn_gate, (l + 1, 0)), (w_ffn_up, (l + 1, 0))] if l + 1 < depth else []
        y, cast = _ffn(h, wg, wu, (w_ffn_down, (l, 1)), following)
        if cast:
            wg, wu = cast
        g_next = norm_pre[l + 1, 0] if l + 1 < depth else None
        x, h = _residual(y, x, norm_post[l, 2], g_next, 0.5)
    return x.reshape(batch, seq, d)
```

```python
import functools
import math

import jax
import jax.numpy as jnp
from jax import lax
from jax.experimental import pallas as pl
from jax.experimental.pallas import tpu as pltpu

_V7X_VMEM_LIMIT_BYTES = 56 * 1024 * 1024
_SUBLANES = 8
_LANES = 128

_EPS = 1e-6
_POOL_WINDOWS = (2, 4, 8, 16)
_FOURIER_HEADS = 4
_ATTN_HEADS = 16
_NEG_BIG = -1e30

_BF16 = jnp.bfloat16
_F32 = jnp.float32


def _compiler_params(*semantics, flags=None):
    return pltpu.CompilerParams(dimension_semantics=semantics,
                                vmem_limit_bytes=_V7X_VMEM_LIMIT_BYTES, flags=flags)


def _tile(n, want):
    t = min(n, want)
    assert n % t == 0, (n, want)
    return t


def _rms(x, g):
    return x * lax.rsqrt(jnp.mean(x * x, axis=-1, keepdims=True) + _EPS) * g


_NORM_ROWS = 256


def _prenorm_kernel(x_ref, g_ref, h_ref):
    h_ref[...] = _rms(x_ref[...], g_ref[...]).astype(h_ref.dtype)


def _prenorm(x, g):
    m, d = x.shape
    tm = _tile(m, _NORM_ROWS)
    return pl.pallas_call(
        _prenorm_kernel,
        grid=(m // tm,),
        in_specs=[pl.BlockSpec((tm, d), lambda i: (i, 0)),
                  pl.BlockSpec((1, d), lambda i: (0, 0))],
        out_specs=pl.BlockSpec((tm, d), lambda i: (i, 0)),
        out_shape=jax.ShapeDtypeStruct((m, d), _BF16),
        compiler_params=_compiler_params("parallel"),
        name="prenorm",
    )(x, g.reshape(1, d))


def _residual_kernel(y_ref, x_ref, gpost_ref, gpre_ref, xo_ref, h_ref, *, step):
    xn = x_ref[...] + step * _rms(y_ref[...], gpost_ref[...])
    xo_ref[...] = xn
    h_ref[...] = _rms(xn, gpre_ref[...]).astype(h_ref.dtype)


def _residual_last_kernel(y_ref, x_ref, gpost_ref, xo_ref, *, step):
    xo_ref[...] = x_ref[...] + step * _rms(y_ref[...], gpost_ref[...])


def _residual(y, x, g_post, g_pre_next, step):
    m, d = x.shape
    tm = _tile(m, _NORM_ROWS)
    row = pl.BlockSpec((tm, d), lambda i: (i, 0))
    vec = pl.BlockSpec((1, d), lambda i: (0, 0))
    if g_pre_next is None:
        return pl.pallas_call(
            functools.partial(_residual_last_kernel, step=step),
            grid=(m // tm,),
            in_specs=[row, row, vec],
            out_specs=row,
            out_shape=jax.ShapeDtypeStruct((m, d), _F32),
            compiler_params=_compiler_params("parallel"),
            name="residual_last",
        )(y, x, g_post.reshape(1, d)), None
    return pl.pallas_call(
        functools.partial(_residual_kernel, step=step),
        grid=(m // tm,),
        in_specs=[row, row, vec, vec],
        out_specs=[row, row],
        out_shape=[jax.ShapeDtypeStruct((m, d), _F32), jax.ShapeDtypeStruct((m, d), _BF16)],
        compiler_params=_compiler_params("parallel"),
        name="residual",
    )(y, x, g_post.reshape(1, d), g_pre_next.reshape(1, d))


def _weight_spec(w, block, index):
    lead = w[1]
    return pl.BlockSpec((None,) * len(lead) + block, lambda *g: lead + index(*g))


def _cast_specs(cast_srcs, steps, step_of):
    ins, outs, shapes, arrays = [], [], [], []
    for arr, lead in cast_srcs:
        rows, cols = arr.shape[-2:]
        slab = rows // steps
        assert rows == slab * steps and slab % (2 * _SUBLANES) == 0, (rows, steps)
        ins.append(pl.BlockSpec((None,) * len(lead) + (slab, cols),
                                lambda *g, lead=lead: lead + (step_of(*g), 0)))
        outs.append(pl.BlockSpec((slab, cols), lambda *g: (step_of(*g), 0)))
        shapes.append(jax.ShapeDtypeStruct((rows, cols), _BF16))
        arrays.append(arr)
    return ins, outs, shapes, arrays


def _cast_slabs(refs):
    n = len(refs) // 2
    for src_ref, dst_ref in zip(refs[:n], refs[n:]):
        dst_ref[...] = src_ref[...].astype(dst_ref.dtype)


def _mm_kernel(a_ref, w_ref, *refs, n_cast, scaled_col_blocks, scale):
    o_ref = refs[n_cast]
    acc = jnp.dot(a_ref[...].astype(_BF16), w_ref[...], preferred_element_type=_F32)
    if scaled_col_blocks:
        acc = acc * jnp.where(pl.program_id(1) < scaled_col_blocks, scale, 1.0)
    o_ref[...] = acc.astype(o_ref.dtype)
    _cast_slabs(refs[:n_cast] + refs[n_cast + 1:])


def _mm(a, w, out_dtype, *, tm=1024, tn=1024, a_col_block=None, scaled_cols=0, scale=1.0, cast_srcs=()):
    m = a.shape[0]
    k, n = w[0].shape[-2:]
    tm, tn = _tile(m, tm), _tile(n, tn)
    a_col = 0
    if a_col_block is not None:
        a_col, width = a_col_block
        assert width == k
    else:
        assert a.shape[1] == k
    assert scaled_cols % tn == 0
    nj = n // tn
    c_in, c_out, c_shape, c_arr = _cast_specs(cast_srcs, (m // tm) * nj, lambda i, j: i * nj + j)
    out = pl.pallas_call(
        functools.partial(_mm_kernel, n_cast=len(c_arr), scaled_col_blocks=scaled_cols // tn, scale=scale),
        grid=(m // tm, nj),
        in_specs=[pl.BlockSpec((tm, k), lambda i, j: (i, a_col)),
                  _weight_spec(w, (k, tn), lambda i, j: (0, j))] + c_in,
        out_specs=[pl.BlockSpec((tm, tn), lambda i, j: (i, j))] + c_out,
        out_shape=[jax.ShapeDtypeStruct((m, n), out_dtype)] + c_shape,
        compiler_params=_compiler_params("parallel", "arbitrary"),
        name="matmul",
    )(a, w[0], *c_arr)
    return out if c_arr else out[0]


def _gate_up_kernel(h_ref, wg_ref, wu_ref, *refs, n_cast):
    o_ref = refs[n_cast]
    h = h_ref[...]
    g = jnp.dot(h, wg_ref[...], preferred_element_type=_F32)
    u = jnp.dot(h, wu_ref[...], preferred_element_type=_F32)
    o_ref[...] = (g * jax.nn.sigmoid(g) * u).astype(o_ref.dtype)
    _cast_slabs(refs[:n_cast] + refs[n_cast + 1:])


def _gate_up(h, wg, wu, *, tm=1024, tn=512, cast_srcs=()):
    m, k = h.shape
    n = wg[0].shape[-1]
    tm, tn = _tile(m, tm), _tile(n, tn)
    nj = n // tn
    c_in, c_out, c_shape, c_arr = _cast_specs(cast_srcs, (m // tm) * nj, lambda i, j: i * nj + j)
    out = pl.pallas_call(
        functools.partial(_gate_up_kernel, n_cast=len(c_arr)),
        grid=(m // tm, nj),
        in_specs=[pl.BlockSpec((tm, k), lambda i, j: (i, 0)),
                  _weight_spec(wg, (k, tn), lambda i, j: (0, j)),
                  _weight_spec(wu, (k, tn), lambda i, j: (0, j))] + c_in,
        out_specs=[pl.BlockSpec((tm, tn), lambda i, j: (i, j))] + c_out,
        out_shape=[jax.ShapeDtypeStruct((m, n), _BF16)] + c_shape,
        compiler_params=_compiler_params("parallel", "arbitrary"),
        name="gate_up",
    )(h, wg[0], wu[0], *c_arr)
    return out if c_arr else out[0]


def _mm_cat_kernel(a_ref, f_ref, w_ref, o_ref):
    ka = a_ref.shape[1]
    acc = jnp.dot(a_ref[...], w_ref[:ka, :], preferred_element_type=_F32)
    acc = acc + jnp.dot(f_ref[...], w_ref[ka:, :], preferred_element_type=_F32)
    o_ref[...] = acc.astype(o_ref.dtype)


def _mm_cat(a, f, w, *, tm=1024, tn=1024):
    m, ka = a.shape
    kf = f.shape[1]
    k, n = w[0].shape[-2:]
    assert ka + kf == k
    tm, tn = _tile(m, tm), _tile(n, tn)
    return pl.pallas_call(
        _mm_cat_kernel,
        grid=(m // tm, n // tn),
        in_specs=[pl.BlockSpec((tm, ka), lambda i, j: (i, 0)),
                  pl.BlockSpec((tm, kf), lambda i, j: (i, 0)),
                  _weight_spec(w, (k, tn), lambda i, j: (0, j))],
        out_specs=pl.BlockSpec((tm, tn), lambda i, j: (i, j)),
        out_shape=jax.ShapeDtypeStruct((m, n), _F32),
        compiler_params=_compiler_params("parallel", "arbitrary"),
        name="matmul_cat",
    )(a, f, w[0])


_POOL_HALO = 8
_POOL_ROWS = 256


def _pool_kernel(prev_ref, x_ref, next_ref, o_ref, *, seq, group):
    ts = x_ref.shape[0]
    ext_rows = ts + 2 * _POOL_HALO
    tile = pl.program_id(0) % (seq // ts)
    t = tile * ts + lax.broadcasted_iota(jnp.int32, (ts, 1), 0)
    first, last = tile == 0, tile == seq // ts - 1
    for g, w in enumerate(_POOL_WINDOWS):
        cols = slice(g * group, (g + 1) * group)
        x = x_ref[:, cols]
        run = jnp.concatenate([jnp.where(first, 0.0, prev_ref[:, cols]), x,
                               jnp.where(last, 0.0, next_ref[:, cols])], axis=0)
        span = 1
        while span < w:
            run = run + pltpu.roll(run, ext_rows - span, 0)
            span *= 2
        total = pltpu.roll(run, w // 2, 0)[_POOL_HALO:_POOL_HALO + ts]
        lo = jnp.clip(t - w // 2, 0, seq)
        hi = jnp.clip(t - w // 2 + w, 0, seq)
        o_ref[:, cols] = (total / (hi - lo).astype(_F32) - x).astype(o_ref.dtype)


def _pool(u, seq, pool_width):
    m = u.shape[0]
    group = pool_width // len(_POOL_WINDOWS)
    assert max(_POOL_WINDOWS) // 2 <= _POOL_HALO
    ts = _tile(seq, _POOL_ROWS)
    per = ts // _POOL_HALO
    last = m // _POOL_HALO - 1
    return pl.pallas_call(
        functools.partial(_pool_kernel, seq=seq, group=group),
        grid=(m // ts,),
        in_specs=[pl.BlockSpec((_POOL_HALO, pool_width), lambda i: (jnp.maximum(i * per - 1, 0), 0)),
                  pl.BlockSpec((ts, pool_width), lambda i: (i, 0)),
                  pl.BlockSpec((_POOL_HALO, pool_width), lambda i: (jnp.minimum((i + 1) * per, last), 0))],
        out_specs=pl.BlockSpec((ts, pool_width), lambda i: (i, 0)),
        out_shape=jax.ShapeDtypeStruct((m, pool_width), _BF16),
        compiler_params=_compiler_params("parallel"),
        name="pool",
    )(u, u, u)


def _pool_proj_kernel(p_ref, w_ref, s_ref, o_ref):
    acc = jnp.dot(p_ref[...], w_ref[...], preferred_element_type=_F32)
    o_ref[...] = (acc * s_ref[...]).astype(o_ref.dtype)


def _pool_proj(pooled, w_pool, pool_scale, *, tm=1024):
    m, width = pooled.shape
    groups, group, _ = w_pool[0].shape[-3:]
    tm = _tile(m, tm)
    return pl.pallas_call(
        _pool_proj_kernel,
        grid=(m // tm, groups),
        in_specs=[pl.BlockSpec((tm, group), lambda i, g: (i, g)),
                  _weight_spec(w_pool, (None, group, group), lambda i, g: (g, 0, 0)),
                  pl.BlockSpec((1, group), lambda i, g: (0, g))],
        out_specs=pl.BlockSpec((tm, group), lambda i, g: (i, g)),
        out_shape=jax.ShapeDtypeStruct((m, width), _BF16),
        compiler_params=_compiler_params("parallel", "arbitrary"),
        name="pool_proj",
    )(pooled, w_pool[0], pool_scale.reshape(1, width))


_DFT_LO_ROWS = 64


def _dft_rows(rows, n):
    idx = (rows[:, None] * jnp.arange(n, dtype=jnp.int32)[None, :]) % n
    ang = idx.astype(_F32) * (2.0 * math.pi / n)
    return jnp.cos(ang), jnp.sin(ang)


def _dft_tables(n):
    lo_rows = _DFT_LO_ROWS if n % _DFT_LO_ROWS == 0 and n > _DFT_LO_ROWS else 1
    c_hi, s_hi = _dft_rows(jnp.arange(n // lo_rows, dtype=jnp.int32) * lo_rows, n)
    c_lo, s_lo = _dft_rows(jnp.arange(lo_rows, dtype=jnp.int32), n)
    cos = c_hi[:, None, :] * c_lo[None, :, :] - s_hi[:, None, :] * s_lo[None, :, :]
    sin = s_hi[:, None, :] * c_lo[None, :, :] + c_hi[:, None, :] * s_lo[None, :, :]
    return cos.reshape(n, n), sin.reshape(n, n)


def _seq_dft_kernel(c_ref, s_ref, xc_ref, xs_ref, o_ref, acc_ref, *, norm):
    k = pl.program_id(3)

    @pl.when(k == 0)
    def _():
        acc_ref[...] = jnp.zeros_like(acc_ref)

    acc_ref[...] += (jnp.dot(c_ref[...], xc_ref[...], preferred_element_type=_F32)
                     - jnp.dot(s_ref[...], xs_ref[...], preferred_element_type=_F32))

    @pl.when(k == pl.num_programs(3) - 1)
    def _():
        o_ref[...] = (acc_ref[...] * norm).astype(o_ref.dtype)


def _seq_dft(cos_s, sin_s, xcs, batch, seq, width, norm, *, tm=1024, tn=1024, tk=1024):
    tm, tn, tk = _tile(seq, tm), _tile(width, tn), _tile(seq, tk)
    nj = width // tn
    rows_k = seq // tk
    rows_m = seq // tm
    return pl.pallas_call(
        functools.partial(_seq_dft_kernel, norm=norm),
        grid=(batch, rows_m, nj, rows_k),
        in_specs=[pl.BlockSpec((tm, tk), lambda b, i, j, k: (i, k)),
                  pl.BlockSpec((tm, tk), lambda b, i, j, k: (i, k)),
                  pl.BlockSpec((tk, tn), lambda b, i, j, k: (b * rows_k + k, j)),
                  pl.BlockSpec((tk, tn), lambda b, i, j, k: (b * rows_k + k, nj + j))],
        out_specs=pl.BlockSpec((tm, tn), lambda b, i, j, k: (b * rows_m + i, j)),
        out_shape=jax.ShapeDtypeStruct((batch * seq, width), _BF16),
        scratch_shapes=[pltpu.VMEM((tm, tn), _F32)],
        compiler_params=_compiler_params("parallel", "parallel", "parallel", "arbitrary"),
        name="seq_dft",
    )(cos_s, sin_s, xcs, xcs)


_LOG2E = math.log2(math.e)
_RAMP_PARTS = 3
_SIGNS = (1.0, -1.0, 0.0)
_OWN = 2
_WIDEN_KEYS = 256


def _diff_attn_kernel(slope_ref, q_ref, k_ref, v_ref, lq1_ref, lk1_ref, lq2_ref, lk2_ref, gsub_ref,
                      o_ref, kaug_ref, qaug_ref, s_ref, p_ref, alpha_ref, rq_ref, rt_ref, sub_ref,
                      mx_ref, m_ref, l_ref, acc_ref, *, lambda_init, tk):
    h = pl.program_id(1)
    qi = pl.program_id(2)
    tq, dv = q_ref.shape
    dh = dv // 2
    seq = k_ref.shape[0]
    n_chunks = seq // tk
    n_col = tk // _LANES
    n_row = tq // _SUBLANES
    assert n_chunks % 2 == 0
    slope = slope_ref[h]
    c_own = (qi * tq) // tk

    @pl.when(qi == 0)
    def _():
        kb = min(tk, _WIDEN_KEYS)

        def widen(i, carry):
            start = pl.multiple_of(i * kb, kb)
            local = (i % (tk // kb)) * kb + lax.broadcasted_iota(jnp.int32, (dh, kb), 1)
            rest = slope * local.astype(_F32)
            row = lax.broadcasted_iota(jnp.int32, (dh, kb), 0)
            ext = jnp.zeros((dh, kb), _F32)
            for part in range(_RAMP_PARTS):
                term = rest.astype(_BF16).astype(_F32)
                ext = jnp.where(row == part, term, ext)
                rest = rest - term
            for half in range(2):
                k_half = k_ref[pl.ds(start, kb), half * dh:(half + 1) * dh]
                kaug_ref[half, :dh, pl.ds(start, kb)] = k_half.T
                kaug_ref[half, dh:, pl.ds(start, kb)] = ext.astype(_BF16)
            return carry

        lax.fori_loop(0, seq // kb, widen, 0)

    lane_q = lax.broadcasted_iota(jnp.int32, (tq, _LANES), 1)
    for variant, sign in enumerate(_SIGNS):
        ext = jnp.where(lane_q < _RAMP_PARTS, sign, 0.0).astype(_BF16)
        for half in range(2):
            qaug_ref[variant, half] = jnp.concatenate([q_ref[:, half * dh:(half + 1) * dh], ext], axis=1)

    rq_ref[...] = slope * lax.broadcasted_iota(jnp.int32, (tq, _LANES), 0).astype(_F32)
    ramp_k = [slope * (lax.broadcasted_iota(jnp.int32, (_SUBLANES, _LANES), 1) + j * _LANES).astype(_F32)
              for j in range(n_col)]

    m_ref[...] = jnp.full_like(m_ref, _NEG_BIG)
    l_ref[...] = jnp.zeros_like(l_ref)
    acc_ref[...] = jnp.zeros_like(acc_ref)

    def chunk_of(j):
        rest = jnp.where(j - 1 >= c_own, j, j - 1)
        return jnp.where(j == 0, c_own, rest)

    def group(g):
        return slice(g * _SUBLANES, (g + 1) * _SUBLANES)

    def scores(c, slot, own):
        start = pl.multiple_of(c * tk, tk)
        variant = _OWN if own else jnp.where(c < c_own, 0, 1)
        for half in range(2):
            s_ref[slot, half] = jnp.dot(qaug_ref[variant, half], kaug_ref[half, :, pl.ds(start, tk)],
                                        preferred_element_type=_F32)

    def values(c, slot):
        start = pl.multiple_of(c * tk, tk)
        v = v_ref[pl.ds(start, tk), :]
        for half in range(2):
            alpha = alpha_ref[slot, half]
            acc_ref[half] = (jnp.concatenate([alpha] * (dv // _LANES), axis=1) * acc_ref[half]
                             + jnp.dot(p_ref[slot, half], v, preferred_element_type=_F32))

    def softmax(c, slot, own):
        shift = slope * (qi * tq - c * tk).astype(_F32)
        if own:
            rt_ref[...] = rq_ref[...] + shift
        else:
            sign = jnp.where(c < c_own, 1.0, -1.0).astype(_F32)
            rt_ref[...] = sign * (rq_ref[...] + shift)

        def biased(half, g, j, row_term):
            s = s_ref[slot, half, group(g), j * _LANES:(j + 1) * _LANES]
            if own:
                return s - jnp.abs(row_term - ramp_k[j])
            return s

        for half in range(2):
            for g in range(n_row):
                row_term = rt_ref[group(g)] if own else None
                mx = biased(half, g, 0, row_term)
                for j in range(1, n_col):
                    mx = jnp.maximum(mx, biased(half, g, j, row_term))
                mx_ref[group(g)] = mx
            row_max = jnp.max(mx_ref[...], axis=-1, keepdims=True)
            m_old = m_ref[half]
            m_new = jnp.maximum(m_old, row_max if own else row_max - rt_ref[...])
            alpha_ref[slot, half] = jnp.exp2(m_old - m_new)
            m_ref[half] = m_new
            sub_ref[...] = m_new if own else m_new + rt_ref[...]
            for g in range(0, n_row, 2):
                pair_rows = slice(g * _SUBLANES, (g + 2) * _SUBLANES)
                row_term = [rt_ref[group(g + i)] if own else None for i in range(2)]
                sub = [sub_ref[group(g + i)] for i in range(2)]
                tot = [None, None]
                for j in range(n_col):
                    ps = [jnp.exp2(biased(half, g + i, j, row_term[i]) - sub[i]) for i in range(2)]
                    tot = [p if t is None else t + p for t, p in zip(tot, ps)]
                    p_ref[slot, half, pair_rows, j * _LANES:(j + 1) * _LANES] = (
                        jnp.concatenate(ps, axis=0).astype(p_ref.dtype))
                for i in range(2):
                    l_ref[half, group(g + i)] = (alpha_ref[slot, half, group(g + i)] * l_ref[half, group(g + i)]
                                                 + tot[i])

    def phase(score_job=None, softmax_job=None, value_job=None):
        if score_job is not None:
            scores(*score_job)
        if softmax_job is not None:
            softmax(*softmax_job)
        if value_job is not None:
            values(*value_job)

    phase(score_job=(c_own, 0, True))
    phase(score_job=(chunk_of(1), 1, False), softmax_job=(c_own, 0, True))

    def pair(jj, carry):
        j = 2 * jj + 1
        phase((chunk_of(j + 1), 0, False), (chunk_of(j), 1, False), (chunk_of(j - 1), 0))
        phase((chunk_of(j + 2), 1, False), (chunk_of(j + 1), 0, False), (chunk_of(j), 1))
        return carry

    lax.fori_loop(0, (n_chunks - 2) // 2, pair, 0)
    phase(softmax_job=(chunk_of(n_chunks - 1), 1, False), value_job=(chunk_of(n_chunks - 2), 0))
    phase(value_job=(chunk_of(n_chunks - 1), 1))

    lam = (jnp.exp(jnp.sum(lq1_ref[...] * lk1_ref[...], keepdims=True))
           - jnp.exp(jnp.sum(lq2_ref[...] * lk2_ref[...], keepdims=True)) + lambda_init)
    l0 = jnp.sum(l_ref[0], axis=-1, keepdims=True)
    l1 = jnp.sum(l_ref[1], axis=-1, keepdims=True)
    o = acc_ref[0] / l0 - lam * (acc_ref[1] / l1)
    o_ref[...] = (_rms(o, gsub_ref[...]) * (1.0 - lambda_init)).astype(o_ref.dtype)


def _diff_attn(qkv, batch, seq, lq1, lk1, lq2, lk2, g_sub, lambda_init, *, tq=512, tk=512):
    m, d3 = qkv.shape
    d = d3 // 3
    heads = _ATTN_HEADS
    dv = d // heads
    dh = dv // 2
    tq, tk = _tile(seq, tq), _tile(seq, tk)
    assert tk % tq == 0
    nq = seq // tq
    slopes = jnp.exp2(-8.0 * jnp.arange(1, heads + 1, dtype=_F32) / heads) * _LOG2E
    lam_spec = pl.BlockSpec((1, dh), lambda b, h, i: (0, 0))
    stat = pltpu.VMEM((tq, _LANES), _F32)
    return pl.pallas_call(
        functools.partial(_diff_attn_kernel, lambda_init=lambda_init, tk=tk),
        grid=(batch, heads, nq),
        in_specs=[pl.BlockSpec(memory_space=pltpu.SMEM),
                  pl.BlockSpec((tq, dv), lambda b, h, i: (b * nq + i, h)),
                  pl.BlockSpec((seq, dv), lambda b, h, i: (b, heads + h)),
                  pl.BlockSpec((seq, dv), lambda b, h, i: (b, 2 * heads + h)),
                  lam_spec, lam_spec, lam_spec, lam_spec,
                  pl.BlockSpec((1, dv), lambda b, h, i: (0, 0))],
        out_specs=pl.BlockSpec((tq, dv), lambda b, h, i: (b * nq + i, h)),
        out_shape=jax.ShapeDtypeStruct((m, d), _BF16),
        scratch_shapes=[pltpu.VMEM((2, 2 * dh, seq), _BF16),
                        pltpu.VMEM((len(_SIGNS), 2, tq, 2 * dh), _BF16),
                        pltpu.VMEM((2, 2, tq, tk), _F32),
                        pltpu.VMEM((2, 2, tq, tk), _BF16),
                        pltpu.VMEM((2, 2, tq, _LANES), _F32),
                        stat, stat, stat, stat,
                        pltpu.VMEM((2, tq, _LANES), _F32),
                        pltpu.VMEM((2, tq, _LANES), _F32),
                        pltpu.VMEM((2, tq, dv), _F32)],
        compiler_params=_compiler_params("parallel", "parallel", "arbitrary"),
        name="diff_attn",
    )(slopes, qkv, qkv, qkv, lq1.reshape(1, dh), lk1.reshape(1, dh), lq2.reshape(1, dh),
      lk2.reshape(1, dh), g_sub.reshape(1, dv))


def _ffn(h, wg, wu, wd_f32, next_f32):
    a, wd = _gate_up(h, wg, wu, cast_srcs=(wd_f32,))
    y, *cast = _mm(a, (wd, ()), _F32, tm=1024, tn=512, cast_srcs=next_f32) if next_f32 else [
        _mm(a, (wd, ()), _F32, tm=1024, tn=512)]
    return y, [(w, ()) for w in cast]


def _pool_fourier(h, batch, seq, w_in, w_pool, pool_scale, w_fourier, w_out, dft):
    cos_s, sin_s, chan_dft = dft
    pool_width = pool_scale.shape[0]
    four_width = w_fourier[0].shape[-1]
    u = _mm(h, w_in, _F32)
    a = _pool_proj(_pool(u, seq, pool_width), w_pool, pool_scale)
    xcs = _mm(u, (chan_dft, ()), _BF16, a_col_block=(pool_width // four_width, four_width))
    head_dim = four_width // _FOURIER_HEADS
    f = _seq_dft(cos_s, sin_s, xcs, batch, seq, four_width, 1.0 / math.sqrt(seq * head_dim))
    f = _mm(f, w_fourier, _BF16)
    return _mm_cat(a, f, w_out)


def kernel(x, norm_pre, norm_post, w_ffn_gate, w_ffn_up, w_ffn_down, w_mix_in, w_pool, pool_scale,
           w_fourier, w_mix_out, w_qkv, w_attn_out, lambda_q1, lambda_k1, lambda_q2, lambda_k2,
           subln_gain):
    batch, seq, d = x.shape
    depth = norm_pre.shape[0]
    m = batch * seq
    x = x.reshape(m, d)
    bf = lambda w: w.astype(_BF16)

    four_width = w_fourier.shape[1]
    head_dim = four_width // _FOURIER_HEADS
    cos_s, sin_s = _dft_tables(seq)
    cos_c, sin_c = _dft_tables(head_dim)
    eye = jnp.eye(_FOURIER_HEADS, dtype=_F32)
    chan_dft = jnp.concatenate([jnp.kron(eye, cos_c), jnp.kron(eye, sin_c)], axis=1)
    dft = (bf(cos_s), bf(sin_s), bf(chan_dft))

    attn_head_dim = d // (2 * _ATTN_HEADS)
    w_pl, w_fr = bf(w_pool), bf(w_fourier)
    wg, wu = (bf(w_ffn_gate[0, 0]), ()), (bf(w_ffn_up[0, 0]), ())

    h = _prenorm(x, norm_pre[0, 0])
    for l in range(depth):
        i = l // 2
        mixer = ([(w_mix_in, (i,)), (w_mix_out, (i,))] if l % 2 == 0 else
                 [(w_qkv, (i,)), (w_attn_out, (i,))])
        y, (wg, wu, w_first, w_last) = _ffn(h, wg, wu, (w_ffn_down, (l, 0)),
                                            [(w_ffn_gate, (l, 1)), (w_ffn_up, (l, 1))] + mixer)
        x, h = _residual(y, x, norm_post[l, 0], norm_pre[l, 1], 0.5)
        if l % 2 == 0:
            y = _pool_fourier(h, batch, seq, w_first, (w_pl, (i,)), pool_scale[i],
                              (w_fr, (i,)), w_last, dft)
        else:
            lambda_init = 0.8 - 0.6 * math.exp(-0.3 * l)
            qkv = _mm(h, w_first, _BF16, scaled_cols=d, scale=attn_head_dim ** -0.5 * _LOG2E)
            o = _diff_attn(qkv, batch, seq, lambda_q1[i], lambda_k1[i], lambda_q2[i], lambda_k2[i],
                           subln_gain[i], lambda_init)
            y = _mm(o, w_last, _F32)
        x, h = _residual(y, x, norm_post[l, 1], norm_pre[l, 2], 1.0)
        following = [(w_ffn_gate, (l + 1, 0)), (w_ffn_up, (l + 1, 0))] if l + 1 < depth else []
        y, cast = _ffn(h, wg, wu, (w_ffn_down, (l, 1)), following)
        if cast:
            wg, wu = cast
        g_next = norm_pre[l + 1, 0] if l + 1 < depth else None
        x, h = _residual(y, x, norm_post[l, 2], g_next, 0.5)
    return x.reshape(batch, seq, d)
```

```python
import functools
import math

import jax
import jax.numpy as jnp
from jax import lax
from jax.experimental import pallas as pl
from jax.experimental.pallas import tpu as pltpu

_V7X_VMEM_LIMIT_BYTES = 56 * 1024 * 1024
_SUBLANES = 8
_LANES = 128

_EPS = 1e-6
_POOL_WINDOWS = (2, 4, 8, 16)
_FOURIER_HEADS = 4
_ATTN_HEADS = 16
_NEG_BIG = -1e30

_BF16 = jnp.bfloat16
_F32 = jnp.float32


def _compiler_params(*semantics, flags=None):
    return pltpu.CompilerParams(dimension_semantics=semantics,
                                vmem_limit_bytes=_V7X_VMEM_LIMIT_BYTES, flags=flags)


def _tile(n, want):
    t = min(n, want)
    assert n % t == 0, (n, want)
    return t


def _rms(x, g):
    return x * lax.rsqrt(jnp.mean(x * x, axis=-1, keepdims=True) + _EPS) * g


_NORM_ROWS = 256


def _prenorm_kernel(x_ref, g_ref, h_ref):
    h_ref[...] = _rms(x_ref[...], g_ref[...]).astype(h_ref.dtype)


def _prenorm(x, g):
    m, d = x.shape
    tm = _tile(m, _NORM_ROWS)
    return pl.pallas_call(
        _prenorm_kernel,
        grid=(m // tm,),
        in_specs=[pl.BlockSpec((tm, d), lambda i: (i, 0)),
                  pl.BlockSpec((1, d), lambda i: (0, 0))],
        out_specs=pl.BlockSpec((tm, d), lambda i: (i, 0)),
        out_shape=jax.ShapeDtypeStruct((m, d), _BF16),
        compiler_params=_compiler_params("parallel"),
        name="prenorm",
    )(x, g.reshape(1, d))


def _residual_kernel(y_ref, x_ref, gpost_ref, gpre_ref, xo_ref, h_ref, *, step):
    xn = x_ref[...] + step * _rms(y_ref[...], gpost_ref[...])
    xo_ref[...] = xn
    h_ref[...] = _rms(xn, gpre_ref[...]).astype(h_ref.dtype)


def _residual_last_kernel(y_ref, x_ref, gpost_ref, xo_ref, *, step):
    xo_ref[...] = x_ref[...] + step * _rms(y_ref[...], gpost_ref[...])


def _residual(y, x, g_post, g_pre_next, step):
    m, d = x.shape
    tm = _tile(m, _NORM_ROWS)
    row = pl.BlockSpec((tm, d), lambda i: (i, 0))
    vec = pl.BlockSpec((1, d), lambda i: (0, 0))
    if g_pre_next is None:
        return pl.pallas_call(
            functools.partial(_residual_last_kernel, step=step),
            grid=(m // tm,),
            in_specs=[row, row, vec],
            out_specs=row,
            out_shape=jax.ShapeDtypeStruct((m, d), _F32),
            compiler_params=_compiler_params("parallel"),
            name="residual_last",
        )(y, x, g_post.reshape(1, d)), None
    return pl.pallas_call(
        functools.partial(_residual_kernel, step=step),
        grid=(m // tm,),
        in_specs=[row, row, vec, vec],
        out_specs=[row, row],
        out_shape=[jax.ShapeDtypeStruct((m, d), _F32), jax.ShapeDtypeStruct((m, d), _BF16)],
        compiler_params=_compiler_params("parallel"),
        name="residual",
    )(y, x, g_post.reshape(1, d), g_pre_next.reshape(1, d))


def _weight_spec(w, block, index):
    lead = w[1]
    return pl.BlockSpec((None,) * len(lead) + block, lambda *g: lead + index(*g))


def _cast_specs(cast_srcs, steps, step_of):
    ins, outs, shapes, arrays = [], [], [], []
    for arr, lead in cast_srcs:
        rows, cols = arr.shape[-2:]
        slab = rows // steps
        assert rows == slab * steps and slab % (2 * _SUBLANES) == 0, (rows, steps)
        ins.append(pl.BlockSpec((None,) * len(lead) + (slab, cols),
                                lambda *g, lead=lead: lead + (step_of(*g), 0)))
        outs.append(pl.BlockSpec((slab, cols), lambda *g: (step_of(*g), 0)))
        shapes.append(jax.ShapeDtypeStruct((rows, cols), _BF16))
        arrays.append(arr)
    return ins, outs, shapes, arrays


def _cast_slabs(refs):
    n = len(refs) // 2
    for src_ref, dst_ref in zip(refs[:n], refs[n:]):
        dst_ref[...] = src_ref[...].astype(dst_ref.dtype)


def _mm_kernel(a_ref, w_ref, *refs, n_cast, scaled_col_blocks, scale):
    o_ref = refs[n_cast]
    acc = jnp.dot(a_ref[...].astype(_BF16), w_ref[...], preferred_element_type=_F32)
    if scaled_col_blocks:
        acc = acc * jnp.where(pl.program_id(1) < scaled_col_blocks, scale, 1.0)
    o_ref[...] = acc.astype(o_ref.dtype)
    _cast_slabs(refs[:n_cast] + refs[n_cast + 1:])


def _mm(a, w, out_dtype, *, tm=1024, tn=1024, a_col_block=None, scaled_cols=0, scale=1.0, cast_srcs=()):
    m = a.shape[0]
    k, n = w[0].shape[-2:]
    tm, tn = _tile(m, tm), _tile(n, tn)
    a_col = 0
    if a_col_block is not None:
        a_col, width = a_col_block
        assert width == k
    else:
        assert a.shape[1] == k
    assert scaled_cols % tn == 0
    nj = n // tn
    c_in, c_out, c_shape, c_arr = _cast_specs(cast_srcs, (m // tm) * nj, lambda i, j: i * nj + j)
    out = pl.pallas_call(
        functools.partial(_mm_kernel, n_cast=len(c_arr), scaled_col_blocks=scaled_cols // tn, scale=scale),
        grid=(m // tm, nj),
        in_specs=[pl.BlockSpec((tm, k), lambda i, j: (i, a_col)),
                  _weight_spec(w, (k, tn), lambda i, j: (0, j))] + c_in,
        out_specs=[pl.BlockSpec((tm, tn), lambda i, j: (i, j))] + c_out,
        out_shape=[jax.ShapeDtypeStruct((m, n), out_dtype)] + c_shape,
        compiler_params=_compiler_params("parallel", "arbitrary"),
        name="matmul",
    )(a, w[0], *c_arr)
    return out if c_arr else out[0]


def _gate_up_kernel(h_ref, wg_ref, wu_ref, *refs, n_cast):
    o_ref = refs[n_cast]
    h = h_ref[...]
    g = jnp.dot(h, wg_ref[...], preferred_element_type=_F32)
    u = jnp.dot(h, wu_ref[...], preferred_element_type=_F32)
    o_ref[...] = (g * jax.nn.sigmoid(g) * u).astype(o_ref.dtype)
    _cast_slabs(refs[:n_cast] + refs[n_cast + 1:])


def _gate_up(h, wg, wu, *, tm=1024, tn=512, cast_srcs=()):
    m, k = h.shape
    n = wg[0].shape[-1]
    tm, tn = _tile(m, tm), _tile(n, tn)
    nj = n // tn
    c_in, c_out, c_shape, c_arr = _cast_specs(cast_srcs, (m // tm) * nj, lambda i, j: i * nj + j)
    out = pl.pallas_call(
        functools.partial(_gate_up_kernel, n_cast=len(c_arr)),
        grid=(m // tm, nj),
        in_specs=[pl.BlockSpec((tm, k), lambda i, j: (i, 0)),
                  _weight_spec(wg, (k, tn), lambda i, j: (0, j)),
                  _weight_spec(wu, (k, tn), lambda i, j: (0, j))] + c_in,
        out_specs=[pl.BlockSpec((tm, tn), lambda i, j: (i, j))] + c_out,
        out_shape=[jax.ShapeDtypeStruct((m, n), _BF16)] + c_shape,
        compiler_params=_compiler_params("parallel", "arbitrary"),
        name="gate_up",
    )(h, wg[0], wu[0], *c_arr)
    return out if c_arr else out[0]


def _mm_cat_kernel(a_ref, f_ref, w_ref, o_ref):
    ka = a_ref.shape[1]
    acc = jnp.dot(a_ref[...], w_ref[:ka, :], preferred_element_type=_F32)
    acc = acc + jnp.dot(f_ref[...], w_ref[ka:, :], preferred_element_type=_F32)
    o_ref[...] = acc.astype(o_ref.dtype)


def _mm_cat(a, f, w, *, tm=1024, tn=1024):
    m, ka = a.shape
    kf = f.shape[1]
    k, n = w[0].shape[-2:]
    assert ka + kf == k
    tm, tn = _tile(m, tm), _tile(n, tn)
    return pl.pallas_call(
        _mm_cat_kernel,
        grid=(m // tm, n // tn),
        in_specs=[pl.BlockSpec((tm, ka), lambda i, j: (i, 0)),
                  pl.BlockSpec((tm, kf), lambda i, j: (i, 0)),
                  _weight_spec(w, (k, tn), lambda i, j: (0, j))],
        out_specs=pl.BlockSpec((tm, tn), lambda i, j: (i, j)),
        out_shape=jax.ShapeDtypeStruct((m, n), _F32),
        compiler_params=_compiler_params("parallel", "arbitrary"),
        name="matmul_cat",
    )(a, f, w[0])


_POOL_HALO = 8
_POOL_ROWS = 256


def _pool_kernel(prev_ref, x_ref, next_ref, o_ref, *, seq, group):
    ts = x_ref.shape[0]
    ext_rows = ts + 2 * _POOL_HALO
    tile = pl.program_id(0) % (seq // ts)
    t = tile * ts + lax.broadcasted_iota(jnp.int32, (ts, 1), 0)
    first, last = tile == 0, tile == seq // ts - 1
    for g, w in enumerate(_POOL_WINDOWS):
        cols = slice(g * group, (g + 1) * group)
        x = x_ref[:, cols]
        run = jnp.concatenate([jnp.where(first, 0.0, prev_ref[:, cols]), x,
                               jnp.where(last, 0.0, next_ref[:, cols])], axis=0)
        span = 1
        while span < w:
            run = run + pltpu.roll(run, ext_rows - span, 0)
            span *= 2
        total = pltpu.roll(run, w // 2, 0)[_POOL_HALO:_POOL_HALO + ts]
        lo = jnp.clip(t - w // 2, 0, seq)
        hi = jnp.clip(t - w // 2 + w, 0, seq)
        o_ref[:, cols] = (total / (hi - lo).astype(_F32) - x).astype(o_ref.dtype)


def _pool(u, seq, pool_width):
    m = u.shape[0]
    group = pool_width // len(_POOL_WINDOWS)
    assert max(_POOL_WINDOWS) // 2 <= _POOL_HALO
    ts = _tile(seq, _POOL_ROWS)
    per = ts // _POOL_HALO
    last = m // _POOL_HALO - 1
    return pl.pallas_call(
        functools.partial(_pool_kernel, seq=seq, group=group),
        grid=(m // ts,),
        in_specs=[pl.BlockSpec((_POOL_HALO, pool_width), lambda i: (jnp.maximum(i * per - 1, 0), 0)),
                  pl.BlockSpec((ts, pool_width), lambda i: (i, 0)),
                  pl.BlockSpec((_POOL_HALO, pool_width), lambda i: (jnp.minimum((i + 1) * per, last), 0))],
        out_specs=pl.BlockSpec((ts, pool_width), lambda i: (i, 0)),
        out_shape=jax.ShapeDtypeStruct((m, pool_width), _BF16),
        compiler_params=_compiler_params("parallel"),
        name="pool",
    )(u, u, u)


def _pool_proj_kernel(p_ref, w_ref, s_ref, o_ref):
    acc = jnp.dot(p_ref[...], w_ref[...], preferred_element_type=_F32)
    o_ref[...] = (acc * s_ref[...]).astype(o_ref.dtype)


def _pool_proj(pooled, w_pool, pool_scale, *, tm=1024):
    m, width = pooled.shape
    groups, group, _ = w_pool[0].shape[-3:]
    tm = _tile(m, tm)
    return pl.pallas_call(
        _pool_proj_kernel,
        grid=(m // tm, groups),
        in_specs=[pl.BlockSpec((tm, group), lambda i, g: (i, g)),
                  _weight_spec(w_pool, (None, group, group), lambda i, g: (g, 0, 0)),
                  pl.BlockSpec((1, group), lambda i, g: (0, g))],
        out_specs=pl.BlockSpec((tm, group), lambda i, g: (i, g)),
        out_shape=jax.ShapeDtypeStruct((m, width), _BF16),
        compiler_params=_compiler_params("parallel", "arbitrary"),
        name="pool_proj",
    )(pooled, w_pool[0], pool_scale.reshape(1, width))


_DFT_LO_ROWS = 64


def _dft_rows(rows, n):
    idx = (rows[:, None] * jnp.arange(n, dtype=jnp.int32)[None, :]) % n
    ang = idx.astype(_F32) * (2.0 * math.pi / n)
    return jnp.cos(ang), jnp.sin(ang)


def _dft_tables(n):
    lo_rows = _DFT_LO_ROWS if n % _DFT_LO_ROWS == 0 and n > _DFT_LO_ROWS else 1
    c_hi, s_hi = _dft_rows(jnp.arange(n // lo_rows, dtype=jnp.int32) * lo_rows, n)
    c_lo, s_lo = _dft_rows(jnp.arange(lo_rows, dtype=jnp.int32), n)
    cos = c_hi[:, None, :] * c_lo[None, :, :] - s_hi[:, None, :] * s_lo[None, :, :]
    sin = s_hi[:, None, :] * c_lo[None, :, :] + c_hi[:, None, :] * s_lo[None, :, :]
    return cos.reshape(n, n), sin.reshape(n, n)


def _seq_dft_kernel(c_ref, s_ref, xc_ref, xs_ref, o_ref, acc_ref, *, norm):
    k = pl.program_id(3)

    @pl.when(k == 0)
    def _():
        acc_ref[...] = jnp.zeros_like(acc_ref)

    acc_ref[...] += (jnp.dot(c_ref[...], xc_ref[...], preferred_element_type=_F32)
                     - jnp.dot(s_ref[...], xs_ref[...], preferred_element_type=_F32))

    @pl.when(k == pl.num_programs(3) - 1)
    def _():
        o_ref[...] = (acc_ref[...] * norm).astype(o_ref.dtype)


def _seq_dft(cos_s, sin_s, xcs, batch, seq, width, norm, *, tm=1024, tn=1024, tk=1024):
    tm, tn, tk = _tile(seq, tm), _tile(width, tn), _tile(seq, tk)
    nj = width // tn
    rows_k = seq // tk
    rows_m = seq // tm
    return pl.pallas_call(
        functools.partial(_seq_dft_kernel, norm=norm),
        grid=(batch, rows_m, nj, rows_k),
        in_specs=[pl.BlockSpec((tm, tk), lambda b, i, j, k: (i, k)),
                  pl.BlockSpec((tm, tk), lambda b, i, j, k: (i, k)),
                  pl.BlockSpec((tk, tn), lambda b, i, j, k: (b * rows_k + k, j)),
                  pl.BlockSpec((tk, tn), lambda b, i, j, k: (b * rows_k + k, nj + j))],
        out_specs=pl.BlockSpec((tm, tn), lambda b, i, j, k: (b * rows_m + i, j)),
        out_shape=jax.ShapeDtypeStruct((batch * seq, width), _BF16),
        scratch_shapes=[pltpu.VMEM((tm, tn), _F32)],
        compiler_params=_compiler_params("parallel", "parallel", "parallel", "arbitrary"),
        name="seq_dft",
    )(cos_s, sin_s, xcs, xcs)


_LOG2E = math.log2(math.e)
_RAMP_PARTS = 3
_SIGNS = (1.0, -1.0, 0.0)
_OWN = 2
_WIDEN_KEYS = 256


def _diff_attn_kernel(slope_ref, q_ref, k_ref, v_ref, lq1_ref, lk1_ref, lq2_ref, lk2_ref, gsub_ref,
                      o_ref, kaug_ref, qaug_ref, s_ref, p_ref, alpha_ref, rq_ref, rt_ref, sub_ref,
                      mx_ref, m_ref, l_ref, acc_ref, *, lambda_init, tk):
    h = pl.program_id(1)
    qi = pl.program_id(2)
    tq, dv = q_ref.shape
    dh = dv // 2
    seq = k_ref.shape[0]
    n_chunks = seq // tk
    n_col = tk // _LANES
    n_row = tq // _SUBLANES
    assert n_chunks % 2 == 0
    slope = slope_ref[h]
    c_own = (qi * tq) // tk

    @pl.when(qi == 0)
    def _():
        kb = min(tk, _WIDEN_KEYS)

        def widen(i, carry):
            start = pl.multiple_of(i * kb, kb)
            local = (i % (tk // kb)) * kb + lax.broadcasted_iota(jnp.int32, (dh, kb), 1)
            rest = slope * local.astype(_F32)
            row = lax.broadcasted_iota(jnp.int32, (dh, kb), 0)
            ext = jnp.zeros((dh, kb), _F32)
            for part in range(_RAMP_PARTS):
                term = rest.astype(_BF16).astype(_F32)
                ext = jnp.where(row == part, term, ext)
                rest = rest - term
            for half in range(2):
                k_half = k_ref[pl.ds(start, kb), half * dh:(half + 1) * dh]
                kaug_ref[half, :dh, pl.ds(start, kb)] = k_half.T
                kaug_ref[half, dh:, pl.ds(start, kb)] = ext.astype(_BF16)
            return carry

        lax.fori_loop(0, seq // kb, widen, 0)

    lane_q = lax.broadcasted_iota(jnp.int32, (tq, _LANES), 1)
    for variant, sign in enumerate(_SIGNS):
        ext = jnp.where(lane_q < _RAMP_PARTS, sign, 0.0).astype(_BF16)
        for half in range(2):
            qaug_ref[variant, half] = jnp.concatenate([q_ref[:, half * dh:(half + 1) * dh], ext], axis=1)

    rq_ref[...] = slope * lax.broadcasted_iota(jnp.int32, (tq, _LANES), 0).astype(_F32)
    ramp_k = [slope * (lax.broadcasted_iota(jnp.int32, (_SUBLANES, _LANES), 1) + j * _LANES).astype(_F32)
              for j in range(n_col)]

    m_ref[...] = jnp.full_like(m_ref, _NEG_BIG)
    l_ref[...] = jnp.zeros_like(l_ref)
    acc_ref[...] = jnp.zeros_like(acc_ref)

    def chunk_of(j):
        rest = jnp.where(j - 1 >= c_own, j, j - 1)
        return jnp.where(j == 0, c_own, rest)

    def group(g):
        return slice(g * _SUBLANES, (g + 1) * _SUBLANES)

    def scores(c, slot, own):
        start = pl.multiple_of(c * tk, tk)
        variant = _OWN if own else jnp.where(c < c_own, 0, 1)
        for half in range(2):
            s_ref[slot, half] = jnp.dot(qaug_ref[variant, half], kaug_ref[half, :, pl.ds(start, tk)],
                                        preferred_element_type=_F32)

    def values(c, slot):
        start = pl.multiple_of(c * tk, tk)
        v = v_ref[pl.ds(start, tk), :]
        for half in range(2):
            alpha = alpha_ref[slot, half]
            acc_ref[half] = (jnp.concatenate([alpha] * (dv // _LANES), axis=1) * acc_ref[half]
                             + jnp.dot(p_ref[slot, half], v, preferred_element_type=_F32))

    def softmax(c, slot, own):
        shift = slope * (qi * tq - c * tk).astype(_F32)
        if own:
            rt_ref[...] = rq_ref[...] + shift
        else:
            sign = jnp.where(c < c_own, 1.0, -1.0).astype(_F32)
            rt_ref[...] = sign * (rq_ref[...] + shift)

        def biased(half, g, j, row_term):
            s = s_ref[slot, half, group(g), j * _LANES:(j + 1) * _LANES]
            if own:
                return s - jnp.abs(row_term - ramp_k[j])
            return s

        for half in range(2):
            for g in range(n_row):
                row_term = rt_ref[group(g)] if own else None
                mx = biased(half, g, 0, row_term)
                for j in range(1, n_col):
                    mx = jnp.maximum(mx, biased(half, g, j, row_term))
                mx_ref[group(g)] = mx
            row_max = jnp.max(mx_ref[...], axis=-1, keepdims=True)
            m_old = m_ref[half]
            m_new = jnp.maximum(m_old, row_max if own else row_max - rt_ref[...])
            alpha_ref[slot, half] = jnp.exp2(m_old - m_new)
            m_ref[half] = m_new
            sub_ref[...] = m_new if own else m_new + rt_ref[...]
            for g in range(0, n_row, 2):
                pair_rows = slice(g * _SUBLANES, (g + 2) * _SUBLANES)
                row_term = [rt_ref[group(g + i)] if own else None for i in range(2)]
                sub = [sub_ref[group(g + i)] for i in range(2)]
                tot = [None, None]
                for j in range(n_col):
                    ps = [jnp.exp2(biased(half, g + i, j, row_term[i]) - sub[i]) for i in range(2)]
                    tot = [p if t is None else t + p for t, p in zip(tot, ps)]
                    p_ref[slot, half, pair_rows, j * _LANES:(j + 1) * _LANES] = (
                        jnp.concatenate(ps, axis=0).astype(p_ref.dtype))
                for i in range(2):
                    l_ref[half, group(g + i)] = (alpha_ref[slot, half, group(g + i)] * l_ref[half, group(g + i)]
                                                 + tot[i])

    def phase(score_job=None, softmax_job=None, value_job=None):
        if score_job is not None:
            scores(*score_job)
        if softmax_job is not None:
            softmax(*softmax_job)
        if value_job is not None:
            values(*value_job)

    phase(score_job=(c_own, 0, True))
    phase(score_job=(chunk_of(1), 1, False), softmax_job=(c_own, 0, True))

    def pair(jj, carry):
        j = 2 * jj + 1
        phase((chunk_of(j + 1), 0, False), (chunk_of(j), 1, False), (chunk_of(j - 1), 0))
        phase((chunk_of(j + 2), 1, False), (chunk_of(j + 1), 0, False), (chunk_of(j), 1))
        return carry

    lax.fori_loop(0, (n_chunks - 2) // 2, pair, 0)
    phase(softmax_job=(chunk_of(n_chunks - 1), 1, False), value_job=(chunk_of(n_chunks - 2), 0))
    phase(value_job=(chunk_of(n_chunks - 1), 1))

    lam = (jnp.exp(jnp.sum(lq1_ref[...] * lk1_ref[...], keepdims=True))
           - jnp.exp(jnp.sum(lq2_ref[...] * lk2_ref[...], keepdims=True)) + lambda_init)
    l0 = jnp.sum(l_ref[0], axis=-1, keepdims=True)
    l1 = jnp.sum(l_ref[1], axis=-1, keepdims=True)
    o = acc_ref[0] / l0 - lam * (acc_ref[1] / l1)
    o_ref[...] = (_rms(o, gsub_ref[...]) * (1.0 - lambda_init)).astype(o_ref.dtype)


def _diff_attn(qkv, batch, seq, lq1, lk1, lq2, lk2, g_sub, lambda_init, *, tq=512, tk=1024):
    m, d3 = qkv.shape
    d = d3 // 3
    heads = _ATTN_HEADS
    dv = d // heads
    dh = dv // 2
    tq, tk = _tile(seq, tq), _tile(seq, tk)
    assert tk % tq == 0
    nq = seq // tq
    slopes = jnp.exp2(-8.0 * jnp.arange(1, heads + 1, dtype=_F32) / heads) * _LOG2E
    lam_spec = pl.BlockSpec((1, dh), lambda b, h, i: (0, 0))
    stat = pltpu.VMEM((tq, _LANES), _F32)
    return pl.pallas_call(
        functools.partial(_diff_attn_kernel, lambda_init=lambda_init, tk=tk),
        grid=(batch, heads, nq),
        in_specs=[pl.BlockSpec(memory_space=pltpu.SMEM),
                  pl.BlockSpec((tq, dv), lambda b, h, i: (b * nq + i, h)),
                  pl.BlockSpec((seq, dv), lambda b, h, i: (b, heads + h)),
                  pl.BlockSpec((seq, dv), lambda b, h, i: (b, 2 * heads + h)),
                  lam_spec, lam_spec, lam_spec, lam_spec,
                  pl.BlockSpec((1, dv), lambda b, h, i: (0, 0))],
        out_specs=pl.BlockSpec((tq, dv), lambda b, h, i: (b * nq + i, h)),
        out_shape=jax.ShapeDtypeStruct((m, d), _BF16),
        scratch_shapes=[pltpu.VMEM((2, 2 * dh, seq), _BF16),
                        pltpu.VMEM((len(_SIGNS), 2, tq, 2 * dh), _BF16),
                        pltpu.VMEM((2, 2, tq, tk), _F32),
                        pltpu.VMEM((2, 2, tq, tk), _BF16),
                        pltpu.VMEM((2, 2, tq, _LANES), _F32),
                        stat, stat, stat, stat,
                        pltpu.VMEM((2, tq, _LANES), _F32),
                        pltpu.VMEM((2, tq, _LANES), _F32),
                        pltpu.VMEM((2, tq, dv), _F32)],
        compiler_params=_compiler_params("parallel", "parallel", "arbitrary"),
        name="diff_attn",
    )(slopes, qkv, qkv, qkv, lq1.reshape(1, dh), lk1.reshape(1, dh), lq2.reshape(1, dh),
      lk2.reshape(1, dh), g_sub.reshape(1, dv))


def _ffn(h, wg, wu, wd_f32, next_f32):
    a, wd = _gate_up(h, wg, wu, cast_srcs=(wd_f32,))
    y, *cast = _mm(a, (wd, ()), _F32, tm=1024, tn=512, cast_srcs=next_f32) if next_f32 else [
        _mm(a, (wd, ()), _F32, tm=1024, tn=512)]
    return y, [(w, ()) for w in cast]


def _pool_fourier(h, batch, seq, w_in, w_pool, pool_scale, w_fourier, w_out, dft):
    cos_s, sin_s, chan_dft = dft
    pool_width = pool_scale.shape[0]
    four_width = w_fourier[0].shape[-1]
    u = _mm(h, w_in, _F32)
    a = _pool_proj(_pool(u, seq, pool_width), w_pool, pool_scale)
    xcs = _mm(u, (chan_dft, ()), _BF16, a_col_block=(pool_width // four_width, four_width))
    head_dim = four_width // _FOURIER_HEADS
    f = _seq_dft(cos_s, sin_s, xcs, batch, seq, four_width, 1.0 / math.sqrt(seq * head_dim))
    f = _mm(f, w_fourier, _BF16)
    return _mm_cat(a, f, w_out)


def kernel(x, norm_pre, norm_post, w_ffn_gate, w_ffn_up, w_ffn_down, w_mix_in, w_pool, pool_scale,
           w_fourier, w_mix_out, w_qkv, w_attn_out, lambda_q1, lambda_k1, lambda_q2, lambda_k2,
           subln_gain):
    batch, seq, d = x.shape
    depth = norm_pre.shape[0]
    m = batch * seq
    x = x.reshape(m, d)
    bf = lambda w: w.astype(_BF16)

    four_width = w_fourier.shape[1]
    head_dim = four_width // _FOURIER_HEADS
    cos_s, sin_s = _dft_tables(seq)
    cos_c, sin_c = _dft_tables(head_dim)
    eye = jnp.eye(_FOURIER_HEADS, dtype=_F32)
    chan_dft = jnp.concatenate([jnp.kron(eye, cos_c), jnp.kron(eye, sin_c)], axis=1)
    dft = (bf(cos_s), bf(sin_s), bf(chan_dft))

    attn_head_dim = d // (2 * _ATTN_HEADS)
    w_pl, w_fr = bf(w_pool), bf(w_fourier)
    wg, wu = (bf(w_ffn_gate[0, 0]), ()), (bf(w_ffn_up[0, 0]), ())

    h = _prenorm(x, norm_pre[0, 0])
    for l in range(depth):
        i = l // 2
        mixer = ([(w_mix_in, (i,)), (w_mix_out, (i,))] if l % 2 == 0 else
                 [(w_qkv, (i,)), (w_attn_out, (i,))])
        y, (wg, wu, w_first, w_last) = _ffn(h, wg, wu, (w_ffn_down, (l, 0)),
                                            [(w_ffn_gate, (l, 1)), (w_ffn_up, (l, 1))] + mixer)
        x, h = _residual(y, x, norm_post[l, 0], norm_pre[l, 1], 0.5)
        if l % 2 == 0:
            y = _pool_fourier(h, batch, seq, w_first, (w_pl, (i,)), pool_scale[i],
                              (w_fr, (i,)), w_last, dft)
        else:
            lambda_init = 0.8 - 0.6 * math.exp(-0.3 * l)
            qkv = _mm(h, w_first, _BF16, scaled_cols=d, scale=attn_head_dim ** -0.5 * _LOG2E)
            o = _diff_attn(qkv, batch, seq, lambda_q1[i], lambda_k1[i], lambda_q2[i], lambda_k2[i],
                           subln_gain[i], lambda_init)
            y = _mm(o, w_last, _F32)
        x, h = _residual(y, x, norm_post[l, 1], norm_pre[l, 2], 1.0)
        following = [(w_ffn_gate, (l + 1, 0)), (w_ffn_up, (l + 1, 0))] if l + 1 < depth else []
        y, cast = _ffn(h, wg, wu, (w_ffn_down, (l, 1)), following)
        if cast:
            wg, wu = cast
        g_next = norm_pre[l + 1, 0] if l + 1 < depth else None
        x, h = _residual(y, x, norm_post[l, 2], g_next, 0.5)
    return x.reshape(batch, seq, d)
```

```python
import functools
import math

import jax
import jax.numpy as jnp
from jax import lax
from jax.experimental import pallas as pl
from jax.experimental.pallas import tpu as pltpu

_V7X_VMEM_LIMIT_BYTES = 56 * 1024 * 1024
_SUBLANES = 8
_LANES = 128

_EPS = 1e-6
_POOL_WINDOWS = (2, 4, 8, 16)
_FOURIER_HEADS = 4
_ATTN_HEADS = 16
_NEG_BIG = -1e30

_BF16 = jnp.bfloat16
_F32 = jnp.float32


def _compiler_params(*semantics, flags=None):
    return pltpu.CompilerParams(dimension_semantics=semantics,
                                vmem_limit_bytes=_V7X_VMEM_LIMIT_BYTES, flags=flags)


def _tile(n, want):
    t = min(n, want)
    assert n % t == 0, (n, want)
    return t


def _rms(x, g):
    return x * lax.rsqrt(jnp.mean(x * x, axis=-1, keepdims=True) + _EPS) * g


_NORM_ROWS = 256


def _prenorm_kernel(x_ref, g_ref, h_ref):
    h_ref[...] = _rms(x_ref[...], g_ref[...]).astype(h_ref.dtype)


def _prenorm(x, g):
    m, d = x.shape
    tm = _tile(m, _NORM_ROWS)
    return pl.pallas_call(
        _prenorm_kernel,
        grid=(m // tm,),
        in_specs=[pl.BlockSpec((tm, d), lambda i: (i, 0)),
                  pl.BlockSpec((1, d), lambda i: (0, 0))],
        out_specs=pl.BlockSpec((tm, d), lambda i: (i, 0)),
        out_shape=jax.ShapeDtypeStruct((m, d), _BF16),
        compiler_params=_compiler_params("parallel"),
        name="prenorm",
    )(x, g.reshape(1, d))


def _residual_kernel(y_ref, x_ref, gpost_ref, gpre_ref, xo_ref, h_ref, *, step):
    xn = x_ref[...] + step * _rms(y_ref[...], gpost_ref[...])
    xo_ref[...] = xn
    h_ref[...] = _rms(xn, gpre_ref[...]).astype(h_ref.dtype)


def _residual_last_kernel(y_ref, x_ref, gpost_ref, xo_ref, *, step):
    xo_ref[...] = x_ref[...] + step * _rms(y_ref[...], gpost_ref[...])


def _residual(y, x, g_post, g_pre_next, step):
    m, d = x.shape
    tm = _tile(m, _NORM_ROWS)
    row = pl.BlockSpec((tm, d), lambda i: (i, 0))
    vec = pl.BlockSpec((1, d), lambda i: (0, 0))
    if g_pre_next is None:
        return pl.pallas_call(
            functools.partial(_residual_last_kernel, step=step),
            grid=(m // tm,),
            in_specs=[row, row, vec],
            out_specs=row,
            out_shape=jax.ShapeDtypeStruct((m, d), _F32),
            compiler_params=_compiler_params("parallel"),
            name="residual_last",
        )(y, x, g_post.reshape(1, d)), None
    return pl.pallas_call(
        functools.partial(_residual_kernel, step=step),
        grid=(m // tm,),
        in_specs=[row, row, vec, vec],
        out_specs=[row, row],
        out_shape=[jax.ShapeDtypeStruct((m, d), _F32), jax.ShapeDtypeStruct((m, d), _BF16)],
        compiler_params=_compiler_params("parallel"),
        name="residual",
    )(y, x, g_post.reshape(1, d), g_pre_next.reshape(1, d))


def _weight_spec(w, block, index):
    lead = w[1]
    return pl.BlockSpec((None,) * len(lead) + block, lambda *g: lead + index(*g))


def _cast_specs(cast_srcs, steps, step_of):
    ins, outs, shapes, arrays = [], [], [], []
    for arr, lead in cast_srcs:
        rows, cols = arr.shape[-2:]
        slab = rows // steps
        assert rows == slab * steps and slab % (2 * _SUBLANES) == 0, (rows, steps)
        ins.append(pl.BlockSpec((None,) * len(lead) + (slab, cols),
                                lambda *g, lead=lead: lead + (step_of(*g), 0)))
        outs.append(pl.BlockSpec((slab, cols), lambda *g: (step_of(*g), 0)))
        shapes.append(jax.ShapeDtypeStruct((rows, cols), _BF16))
        arrays.append(arr)
    return ins, outs, shapes, arrays


def _cast_slabs(refs):
    n = len(refs) // 2
    for src_ref, dst_ref in zip(refs[:n], refs[n:]):
        dst_ref[...] = src_ref[...].astype(dst_ref.dtype)


def _mm_kernel(a_ref, w_ref, *refs, n_cast, scaled_col_blocks, scale):
    o_ref = refs[n_cast]
    acc = jnp.dot(a_ref[...].astype(_BF16), w_ref[...], preferred_element_type=_F32)
    if scaled_col_blocks:
        acc = acc * jnp.where(pl.program_id(1) < scaled_col_blocks, scale, 1.0)
    o_ref[...] = acc.astype(o_ref.dtype)
    _cast_slabs(refs[:n_cast] + refs[n_cast + 1:])


def _mm(a, w, out_dtype, *, tm=1024, tn=1024, a_col_block=None, scaled_cols=0, scale=1.0, cast_srcs=()):
    m = a.shape[0]
    k, n = w[0].shape[-2:]
    tm, tn = _tile(m, tm), _tile(n, tn)
    a_col = 0
    if a_col_block is not None:
        a_col, width = a_col_block
        assert width == k
    else:
        assert a.shape[1] == k
    assert scaled_cols % tn == 0
    nj = n // tn
    c_in, c_out, c_shape, c_arr = _cast_specs(cast_srcs, (m // tm) * nj, lambda i, j: i * nj + j)
    out = pl.pallas_call(
        functools.partial(_mm_kernel, n_cast=len(c_arr), scaled_col_blocks=scaled_cols // tn, scale=scale),
        grid=(m // tm, nj),
        in_specs=[pl.BlockSpec((tm, k), lambda i, j: (i, a_col)),
                  _weight_spec(w, (k, tn), lambda i, j: (0, j))] + c_in,
        out_specs=[pl.BlockSpec((tm, tn), lambda i, j: (i, j))] + c_out,
        out_shape=[jax.ShapeDtypeStruct((m, n), out_dtype)] + c_shape,
        compiler_params=_compiler_params("parallel", "arbitrary"),
        name="matmul",
    )(a, w[0], *c_arr)
    return out if c_arr else out[0]


def _gate_up_kernel(h_ref, wg_ref, wu_ref, *refs, n_cast):
    o_ref = refs[n_cast]
    h = h_ref[...]
    g = jnp.dot(h, wg_ref[...], preferred_element_type=_F32)
    u = jnp.dot(h, wu_ref[...], preferred_element_type=_F32)
    o_ref[...] = (g * jax.nn.sigmoid(g) * u).astype(o_ref.dtype)
    _cast_slabs(refs[:n_cast] + refs[n_cast + 1:])


def _gate_up(h, wg, wu, *, tm=1024, tn=512, cast_srcs=()):
    m, k = h.shape
    n = wg[0].shape[-1]
    tm, tn = _tile(m, tm), _tile(n, tn)
    nj = n // tn
    c_in, c_out, c_shape, c_arr = _cast_specs(cast_srcs, (m // tm) * nj, lambda i, j: i * nj + j)
    out = pl.pallas_call(
        functools.partial(_gate_up_kernel, n_cast=len(c_arr)),
        grid=(m // tm, nj),
        in_specs=[pl.BlockSpec((tm, k), lambda i, j: (i, 0)),
                  _weight_spec(wg, (k, tn), lambda i, j: (0, j)),
                  _weight_spec(wu, (k, tn), lambda i, j: (0, j))] + c_in,
        out_specs=[pl.BlockSpec((tm, tn), lambda i, j: (i, j))] + c_out,
        out_shape=[jax.ShapeDtypeStruct((m, n), _BF16)] + c_shape,
        compiler_params=_compiler_params("parallel", "arbitrary"),
        name="gate_up",
    )(h, wg[0], wu[0], *c_arr)
    return out if c_arr else out[0]


def _mm_cat_kernel(a_ref, f_ref, w_ref, o_ref):
    ka = a_ref.shape[1]
    acc = jnp.dot(a_ref[...], w_ref[:ka, :], preferred_element_type=_F32)
    acc = acc + jnp.dot(f_ref[...], w_ref[ka:, :], preferred_element_type=_F32)
    o_ref[...] = acc.astype(o_ref.dtype)


def _mm_cat(a, f, w, *, tm=1024, tn=1024):
    m, ka = a.shape
    kf = f.shape[1]
    k, n = w[0].shape[-2:]
    assert ka + kf == k
    tm, tn = _tile(m, tm), _tile(n, tn)
    return pl.pallas_call(
        _mm_cat_kernel,
        grid=(m // tm, n // tn),
        in_specs=[pl.BlockSpec((tm, ka), lambda i, j: (i, 0)),
                  pl.BlockSpec((tm, kf), lambda i, j: (i, 0)),
                  _weight_spec(w, (k, tn), lambda i, j: (0, j))],
        out_specs=pl.BlockSpec((tm, tn), lambda i, j: (i, j)),
        out_shape=jax.ShapeDtypeStruct((m, n), _F32),
        compiler_params=_compiler_params("parallel", "arbitrary"),
        name="matmul_cat",
    )(a, f, w[0])


_POOL_HALO = 8
_POOL_ROWS = 256


def _pool_kernel(prev_ref, x_ref, next_ref, o_ref, *, seq, group):
    ts = x_ref.shape[0]
    ext_rows = ts + 2 * _POOL_HALO
    tile = pl.program_id(0) % (seq // ts)
    t = tile * ts + lax.broadcasted_iota(jnp.int32, (ts, 1), 0)
    first, last = tile == 0, tile == seq // ts - 1
    for g, w in enumerate(_POOL_WINDOWS):
        cols = slice(g * group, (g + 1) * group)
        x = x_ref[:, cols]
        run = jnp.concatenate([jnp.where(first, 0.0, prev_ref[:, cols]), x,
                               jnp.where(last, 0.0, next_ref[:, cols])], axis=0)
        span = 1
        while span < w:
            run = run + pltpu.roll(run, ext_rows - span, 0)
            span *= 2
        total = pltpu.roll(run, w // 2, 0)[_POOL_HALO:_POOL_HALO + ts]
        lo = jnp.clip(t - w // 2, 0, seq)
        hi = jnp.clip(t - w // 2 + w, 0, seq)
        o_ref[:, cols] = (total / (hi - lo).astype(_F32) - x).astype(o_ref.dtype)


def _pool(u, seq, pool_width):
    m = u.shape[0]
    group = pool_width // len(_POOL_WINDOWS)
    assert max(_POOL_WINDOWS) // 2 <= _POOL_HALO
    ts = _tile(seq, _POOL_ROWS)
    per = ts // _POOL_HALO
    last = m // _POOL_HALO - 1
    return pl.pallas_call(
        functools.partial(_pool_kernel, seq=seq, group=group),
        grid=(m // ts,),
        in_specs=[pl.BlockSpec((_POOL_HALO, pool_width), lambda i: (jnp.maximum(i * per - 1, 0), 0)),
                  pl.BlockSpec((ts, pool_width), lambda i: (i, 0)),
                  pl.BlockSpec((_POOL_HALO, pool_width), lambda i: (jnp.minimum((i + 1) * per, last), 0))],
        out_specs=pl.BlockSpec((ts, pool_width), lambda i: (i, 0)),
        out_shape=jax.ShapeDtypeStruct((m, pool_width), _BF16),
        compiler_params=_compiler_params("parallel"),
        name="pool",
    )(u, u, u)


def _pool_proj_kernel(p_ref, w_ref, s_ref, o_ref):
    acc = jnp.dot(p_ref[...], w_ref[...], preferred_element_type=_F32)
    o_ref[...] = (acc * s_ref[...]).astype(o_ref.dtype)


def _pool_proj(pooled, w_pool, pool_scale, *, tm=1024):
    m, width = pooled.shape
    groups, group, _ = w_pool[0].shape[-3:]
    tm = _tile(m, tm)
    return pl.pallas_call(
        _pool_proj_kernel,
        grid=(m // tm, groups),
        in_specs=[pl.BlockSpec((tm, group), lambda i, g: (i, g)),
                  _weight_spec(w_pool, (None, group, group), lambda i, g: (g, 0, 0)),
                  pl.BlockSpec((1, group), lambda i, g: (0, g))],
        out_specs=pl.BlockSpec((tm, group), lambda i, g: (i, g)),
        out_shape=jax.ShapeDtypeStruct((m, width), _BF16),
        compiler_params=_compiler_params("parallel", "arbitrary"),
        name="pool_proj",
    )(pooled, w_pool[0], pool_scale.reshape(1, width))


_DFT_LO_ROWS = 64


def _dft_rows(rows, n):
    idx = (rows[:, None] * jnp.arange(n, dtype=jnp.int32)[None, :]) % n
    ang = idx.astype(_F32) * (2.0 * math.pi / n)
    return jnp.cos(ang), jnp.sin(ang)


def _dft_tables(n):
    lo_rows = _DFT_LO_ROWS if n % _DFT_LO_ROWS == 0 and n > _DFT_LO_ROWS else 1
    c_hi, s_hi = _dft_rows(jnp.arange(n // lo_rows, dtype=jnp.int32) * lo_rows, n)
    c_lo, s_lo = _dft_rows(jnp.arange(lo_rows, dtype=jnp.int32), n)
    cos = c_hi[:, None, :] * c_lo[None, :, :] - s_hi[:, None, :] * s_lo[None, :, :]
    sin = s_hi[:, None, :] * c_lo[None, :, :] + c_hi[:, None, :] * s_lo[None, :, :]
    return cos.reshape(n, n), sin.reshape(n, n)


def _seq_dft_kernel(c_ref, s_ref, xc_ref, xs_ref, o_ref, acc_ref, *, norm):
    k = pl.program_id(3)

    @pl.when(k == 0)
    def _():
        acc_ref[...] = jnp.zeros_like(acc_ref)

    acc_ref[...] += (jnp.dot(c_ref[...], xc_ref[...], preferred_element_type=_F32)
                     - jnp.dot(s_ref[...], xs_ref[...], preferred_element_type=_F32))

    @pl.when(k == pl.num_programs(3) - 1)
    def _():
        o_ref[...] = (acc_ref[...] * norm).astype(o_ref.dtype)


def _seq_dft(cos_s, sin_s, xcs, batch, seq, width, norm, *, tm=1024, tn=1024, tk=1024):
    tm, tn, tk = _tile(seq, tm), _tile(width, tn), _tile(seq, tk)
    nj = width // tn
    rows_k = seq // tk
    rows_m = seq // tm
    return pl.pallas_call(
        functools.partial(_seq_dft_kernel, norm=norm),
        grid=(batch, rows_m, nj, rows_k),
        in_specs=[pl.BlockSpec((tm, tk), lambda b, i, j, k: (i, k)),
                  pl.BlockSpec((tm, tk), lambda b, i, j, k: (i, k)),
                  pl.BlockSpec((tk, tn), lambda b, i, j, k: (b * rows_k + k, j)),
                  pl.BlockSpec((tk, tn), lambda b, i, j, k: (b * rows_k + k, nj + j))],
        out_specs=pl.BlockSpec((tm, tn), lambda b, i, j, k: (b * rows_m + i, j)),
        out_shape=jax.ShapeDtypeStruct((batch * seq, width), _BF16),
        scratch_shapes=[pltpu.VMEM((tm, tn), _F32)],
        compiler_params=_compiler_params("parallel", "parallel", "parallel", "arbitrary"),
        name="seq_dft",
    )(cos_s, sin_s, xcs, xcs)


_LOG2E = math.log2(math.e)
_RAMP_PARTS = 3
_SIGNS = (1.0, -1.0, 0.0)
_OWN = 2
_WIDEN_KEYS = 256


def _diff_attn_kernel(slope_ref, q_ref, k_ref, v_ref, lq1_ref, lk1_ref, lq2_ref, lk2_ref, gsub_ref,
                      o_ref, kaug_ref, qaug_ref, s_ref, p_ref, alpha_ref, rq_ref, rt_ref, sub_ref,
                      mx_ref, m_ref, l_ref, acc_ref, *, lambda_init, tk):
    h = pl.program_id(1)
    qi = pl.program_id(2)
    tq, dv = q_ref.shape
    dh = dv // 2
    seq = k_ref.shape[0]
    n_chunks = seq // tk
    n_col = tk // _LANES
    n_row = tq // _SUBLANES
    assert n_chunks % 2 == 0
    slope = slope_ref[h]
    c_own = (qi * tq) // tk

    @pl.when(qi == 0)
    def _():
        kb = min(tk, _WIDEN_KEYS)

        def widen(i, carry):
            start = pl.multiple_of(i * kb, kb)
            local = (i % (tk // kb)) * kb + lax.broadcasted_iota(jnp.int32, (dh, kb), 1)
            rest = slope * local.astype(_F32)
            row = lax.broadcasted_iota(jnp.int32, (dh, kb), 0)
            ext = jnp.zeros((dh, kb), _F32)
            for part in range(_RAMP_PARTS):
                term = rest.astype(_BF16).astype(_F32)
                ext = jnp.where(row == part, term, ext)
                rest = rest - term
            ext = jnp.where((row >= _RAMP_PARTS) & (row < 2 * _RAMP_PARTS), 1.0, ext)
            for half in range(2):
                k_half = k_ref[pl.ds(start, kb), half * dh:(half + 1) * dh]
                kaug_ref[half, :dh, pl.ds(start, kb)] = k_half.T
                kaug_ref[half, dh:, pl.ds(start, kb)] = ext.astype(_BF16)
            return carry

        lax.fori_loop(0, seq // kb, widen, 0)

    ramp_q = slope * lax.broadcasted_iota(jnp.int32, (tq, _LANES), 0).astype(_F32)
    rq_ref[...] = ramp_q
    lane_q = lax.broadcasted_iota(jnp.int32, (tq, _LANES), 1)
    ramp_q_parts, rest = [], ramp_q
    for part in range(_RAMP_PARTS):
        ramp_q_parts.append(rest.astype(_BF16).astype(_F32))
        rest = rest - ramp_q_parts[-1]
    for variant, sign in enumerate(_SIGNS):
        ext = jnp.where(lane_q < _RAMP_PARTS, sign, 0.0)
        for part in range(_RAMP_PARTS):
            ext = jnp.where(lane_q == _RAMP_PARTS + part, -sign * ramp_q_parts[part], ext)
        for half in range(2):
            qaug_ref[variant, half] = jnp.concatenate(
                [q_ref[:, half * dh:(half + 1) * dh], ext.astype(_BF16)], axis=1)
    ramp_k = [slope * (lax.broadcasted_iota(jnp.int32, (_SUBLANES, _LANES), 1) + j * _LANES).astype(_F32)
              for j in range(n_col)]

    m_ref[...] = jnp.full_like(m_ref, _NEG_BIG)
    l_ref[...] = jnp.zeros_like(l_ref)
    acc_ref[...] = jnp.zeros_like(acc_ref)

    def chunk_of(j):
        rest = jnp.where(j - 1 >= c_own, j, j - 1)
        return jnp.where(j == 0, c_own, rest)

    def group(g):
        return slice(g * _SUBLANES, (g + 1) * _SUBLANES)

    def scores(c, slot, own):
        start = pl.multiple_of(c * tk, tk)
        variant = _OWN if own else jnp.where(c < c_own, 0, 1)
        for half in range(2):
            s_ref[slot, half] = jnp.dot(qaug_ref[variant, half], kaug_ref[half, :, pl.ds(start, tk)],
                                        preferred_element_type=_F32)

    def values(c, slot):
        start = pl.multiple_of(c * tk, tk)
        v = v_ref[pl.ds(start, tk), :]
        for half in range(2):
            alpha = alpha_ref[slot, half]
            acc_ref[half] = (jnp.concatenate([alpha] * (dv // _LANES), axis=1) * acc_ref[half]
                             + jnp.dot(p_ref[slot, half], v, preferred_element_type=_F32))

    def softmax(c, slot, own):
        shift = slope * (qi * tq - c * tk).astype(_F32)
        if own:
            rt_ref[...] = rq_ref[...] + shift
            kappa = 0.0
        else:
            kappa = jnp.where(c < c_own, -shift, shift)

        def biased(half, g, j, row_term):
            s = s_ref[slot, half, group(g), j * _LANES:(j + 1) * _LANES]
            if own:
                return s - jnp.abs(row_term - ramp_k[j])
            return s

        for half in range(2):
            if own:
                for g in range(n_row):
                    row_term = rt_ref[group(g)]
                    mx = biased(half, g, 0, row_term)
                    for j in range(1, n_col):
                        mx = jnp.maximum(mx, biased(half, g, j, row_term))
                    mx_ref[group(g)] = mx
                row_max = jnp.max(mx_ref[...], axis=-1, keepdims=True)
            else:
                row_max = jnp.max(s_ref[slot, half], axis=-1, keepdims=True)
            m_old = m_ref[half]
            m_new = jnp.maximum(m_old, row_max + kappa)
            alpha_ref[slot, half] = jnp.exp2(m_old - m_new)
            m_ref[half] = m_new
            sub_ref[...] = m_new - kappa
            for g in range(0, n_row, 2):
                pair_rows = slice(g * _SUBLANES, (g + 2) * _SUBLANES)
                row_term = [rt_ref[group(g + i)] if own else None for i in range(2)]
                sub = [sub_ref[group(g + i)] for i in range(2)]
                tot = [None, None]
                for j in range(n_col):
                    ps = [jnp.exp2(biased(half, g + i, j, row_term[i]) - sub[i]) for i in range(2)]
                    tot = [p if t is None else t + p for t, p in zip(tot, ps)]
                    p_ref[slot, half, pair_rows, j * _LANES:(j + 1) * _LANES] = (
                        jnp.concatenate(ps, axis=0).astype(p_ref.dtype))
                for i in range(2):
                    l_ref[half, group(g + i)] = (alpha_ref[slot, half, group(g + i)] * l_ref[half, group(g + i)]
                                                 + tot[i])

    def phase(score_job=None, softmax_job=None, value_job=None):
        if score_job is not None:
            scores(*score_job)
        if softmax_job is not None:
            softmax(*softmax_job)
        if value_job is not None:
            values(*value_job)

    phase(score_job=(c_own, 0, True))
    phase(score_job=(chunk_of(1), 1, False), softmax_job=(c_own, 0, True))

    def pair(jj, carry):
        j = 2 * jj + 1
        phase((chunk_of(j + 1), 0, False), (chunk_of(j), 1, False), (chunk_of(j - 1), 0))
        phase((chunk_of(j + 2), 1, False), (chunk_of(j + 1), 0, False), (chunk_of(j), 1))
        return carry

    lax.fori_loop(0, (n_chunks - 2) // 2, pair, 0)
    phase(softmax_job=(chunk_of(n_chunks - 1), 1, False), value_job=(chunk_of(n_chunks - 2), 0))
    phase(value_job=(chunk_of(n_chunks - 1), 1))

    lam = (jnp.exp(jnp.sum(lq1_ref[...] * lk1_ref[...], keepdims=True))
           - jnp.exp(jnp.sum(lq2_ref[...] * lk2_ref[...], keepdims=True)) + lambda_init)
    l0 = jnp.sum(l_ref[0], axis=-1, keepdims=True)
    l1 = jnp.sum(l_ref[1], axis=-1, keepdims=True)
    o = acc_ref[0] / l0 - lam * (acc_ref[1] / l1)
    o_ref[...] = (_rms(o, gsub_ref[...]) * (1.0 - lambda_init)).astype(o_ref.dtype)


def _diff_attn(qkv, batch, seq, lq1, lk1, lq2, lk2, g_sub, lambda_init, *, tq=512, tk=512):
    m, d3 = qkv.shape
    d = d3 // 3
    heads = _ATTN_HEADS
    dv = d // heads
    dh = dv // 2
    tq, tk = _tile(seq, tq), _tile(seq, tk)
    assert tk % tq == 0
    nq = seq // tq
    slopes = jnp.exp2(-8.0 * jnp.arange(1, heads + 1, dtype=_F32) / heads) * _LOG2E
    lam_spec = pl.BlockSpec((1, dh), lambda b, h, i: (0, 0))
    stat = pltpu.VMEM((tq, _LANES), _F32)
    return pl.pallas_call(
        functools.partial(_diff_attn_kernel, lambda_init=lambda_init, tk=tk),
        grid=(batch, heads, nq),
        in_specs=[pl.BlockSpec(memory_space=pltpu.SMEM),
                  pl.BlockSpec((tq, dv), lambda b, h, i: (b * nq + i, h)),
                  pl.BlockSpec((seq, dv), lambda b, h, i: (b, heads + h)),
                  pl.BlockSpec((seq, dv), lambda b, h, i: (b, 2 * heads + h)),
                  lam_spec, lam_spec, lam_spec, lam_spec,
                  pl.BlockSpec((1, dv), lambda b, h, i: (0, 0))],
        out_specs=pl.BlockSpec((tq, dv), lambda b, h, i: (b * nq + i, h)),
        out_shape=jax.ShapeDtypeStruct((m, d), _BF16),
        scratch_shapes=[pltpu.VMEM((2, 2 * dh, seq), _BF16),
                        pltpu.VMEM((len(_SIGNS), 2, tq, 2 * dh), _BF16),
                        pltpu.VMEM((2, 2, tq, tk), _F32),
                        pltpu.VMEM((2, 2, tq, tk), _BF16),
                        pltpu.VMEM((2, 2, tq, _LANES), _F32),
                        stat, stat, stat, stat,
                        pltpu.VMEM((2, tq, _LANES), _F32),
                        pltpu.VMEM((2, tq, _LANES), _F32),
                        pltpu.VMEM((2, tq, dv), _F32)],
        compiler_params=_compiler_params("parallel", "parallel", "arbitrary"),
        name="diff_attn",
    )(slopes, qkv, qkv, qkv, lq1.reshape(1, dh), lk1.reshape(1, dh), lq2.reshape(1, dh),
      lk2.reshape(1, dh), g_sub.reshape(1, dv))


def _ffn(h, wg, wu, wd_f32, next_f32):
    a, wd = _gate_up(h, wg, wu, cast_srcs=(wd_f32,))
    y, *cast = _mm(a, (wd, ()), _F32, tm=1024, tn=512, cast_srcs=next_f32) if next_f32 else [
        _mm(a, (wd, ()), _F32, tm=1024, tn=512)]
    return y, [(w, ()) for w in cast]


def _pool_fourier(h, batch, seq, w_in, w_pool, pool_scale, w_fourier, w_out, dft):
    cos_s, sin_s, chan_dft = dft
    pool_width = pool_scale.shape[0]
    four_width = w_fourier[0].shape[-1]
    u = _mm(h, w_in, _F32)
    a = _pool_proj(_pool(u, seq, pool_width), w_pool, pool_scale)
    xcs = _mm(u, (chan_dft, ()), _BF16, a_col_block=(pool_width // four_width, four_width))
    head_dim = four_width // _FOURIER_HEADS
    f = _seq_dft(cos_s, sin_s, xcs, batch, seq, four_width, 1.0 / math.sqrt(seq * head_dim))
    f = _mm(f, w_fourier, _BF16)
    return _mm_cat(a, f, w_out)


def kernel(x, norm_pre, norm_post, w_ffn_gate, w_ffn_up, w_ffn_down, w_mix_in, w_pool, pool_scale,
           w_fourier, w_mix_out, w_qkv, w_attn_out, lambda_q1, lambda_k1, lambda_q2, lambda_k2,
           subln_gain):
    batch, seq, d = x.shape
    depth = norm_pre.shape[0]
    m = batch * seq
    x = x.reshape(m, d)
    bf = lambda w: w.astype(_BF16)

    four_width = w_fourier.shape[1]
    head_dim = four_width // _FOURIER_HEADS
    cos_s, sin_s = _dft_tables(seq)
    cos_c, sin_c = _dft_tables(head_dim)
    eye = jnp.eye(_FOURIER_HEADS, dtype=_F32)
    chan_dft = jnp.concatenate([jnp.kron(eye, cos_c), jnp.kron(eye, sin_c)], axis=1)
    dft = (bf(cos_s), bf(sin_s), bf(chan_dft))

    attn_head_dim = d // (2 * _ATTN_HEADS)
    w_pl, w_fr = bf(w_pool), bf(w_fourier)
    wg, wu = (bf(w_ffn_gate[0, 0]), ()), (bf(w_ffn_up[0, 0]), ())

    h = _prenorm(x, norm_pre[0, 0])
    for l in range(depth):
        i = l // 2
        mixer = ([(w_mix_in, (i,)), (w_mix_out, (i,))] if l % 2 == 0 else
                 [(w_qkv, (i,)), (w_attn_out, (i,))])
        y, (wg, wu, w_first, w_last) = _ffn(h, wg, wu, (w_ffn_down, (l, 0)),
                                            [(w_ffn_gate, (l, 1)), (w_ffn_up, (l, 1))] + mixer)
        x, h = _residual(y, x, norm_post[l, 0], norm_pre[l, 1], 0.5)
        if l % 2 == 0:
            y = _pool_fourier(h, batch, seq, w_first, (w_pl, (i,)), pool_scale[i],
                              (w_fr, (i,)), w_last, dft)
        else:
            lambda_init = 0.8 - 0.6 * math.exp(-0.3 * l)
            qkv = _mm(h, w_first, _BF16, scaled_cols=d, scale=attn_head_dim ** -0.5 * _LOG2E)
            o = _diff_attn(qkv, batch, seq, lambda_q1[i], lambda_k1[i], lambda_q2[i], lambda_k2[i],
                           subln_gain[i], lambda_init)
            y = _mm(o, w_last, _F32)
        x, h = _residual(y, x, norm_post[l, 1], norm_pre[l, 2], 1.0)
        following = [(w_ffn_gate, (l + 1, 0)), (w_ffn_up, (l + 1, 0))] if l + 1 < depth else []
        y, cast = _ffn(h, wg, wu, (w_ffn_down, (l, 1)), following)
        if cast:
            wg, wu = cast
        g_next = norm_pre[l + 1, 0] if l + 1 < depth else None
        x, h = _residual(y, x, norm_post[l, 2], g_next, 0.5)
    return x.reshape(batch, seq, d)
```

```python
import functools
import math

import jax
import jax.numpy as jnp
from jax import lax
from jax.experimental import pallas as pl
from jax.experimental.pallas import tpu as pltpu

_V7X_VMEM_LIMIT_BYTES = 56 * 1024 * 1024
_SUBLANES = 8
_LANES = 128

_EPS = 1e-6
_POOL_WINDOWS = (2, 4, 8, 16)
_FOURIER_HEADS = 4
_ATTN_HEADS = 16
_NEG_BIG = -1e30

_BF16 = jnp.bfloat16
_F32 = jnp.float32


def _compiler_params(*semantics, flags=None):
    return pltpu.CompilerParams(dimension_semantics=semantics,
                                vmem_limit_bytes=_V7X_VMEM_LIMIT_BYTES, flags=flags)


def _tile(n, want):
    t = min(n, want)
    assert n % t == 0, (n, want)
    return t


def _rms(x, g):
    return x * lax.rsqrt(jnp.mean(x * x, axis=-1, keepdims=True) + _EPS) * g


_NORM_ROWS = 256


def _prenorm_kernel(x_ref, g_ref, h_ref):
    h_ref[...] = _rms(x_ref[...], g_ref[...]).astype(h_ref.dtype)


def _prenorm(x, g):
    m, d = x.shape
    tm = _tile(m, _NORM_ROWS)
    return pl.pallas_call(
        _prenorm_kernel,
        grid=(m // tm,),
        in_specs=[pl.BlockSpec((tm, d), lambda i: (i, 0)),
                  pl.BlockSpec((1, d), lambda i: (0, 0))],
        out_specs=pl.BlockSpec((tm, d), lambda i: (i, 0)),
        out_shape=jax.ShapeDtypeStruct((m, d), _BF16),
        compiler_params=_compiler_params("parallel"),
        name="prenorm",
    )(x, g.reshape(1, d))


def _residual_kernel(y_ref, x_ref, gpost_ref, gpre_ref, xo_ref, h_ref, *, step):
    xn = x_ref[...] + step * _rms(y_ref[...], gpost_ref[...])
    xo_ref[...] = xn
    h_ref[...] = _rms(xn, gpre_ref[...]).astype(h_ref.dtype)


def _residual_last_kernel(y_ref, x_ref, gpost_ref, xo_ref, *, step):
    xo_ref[...] = x_ref[...] + step * _rms(y_ref[...], gpost_ref[...])


def _residual(y, x, g_post, g_pre_next, step):
    m, d = x.shape
    tm = _tile(m, _NORM_ROWS)
    row = pl.BlockSpec((tm, d), lambda i: (i, 0))
    vec = pl.BlockSpec((1, d), lambda i: (0, 0))
    if g_pre_next is None:
        return pl.pallas_call(
            functools.partial(_residual_last_kernel, step=step),
            grid=(m // tm,),
            in_specs=[row, row, vec],
            out_specs=row,
            out_shape=jax.ShapeDtypeStruct((m, d), _F32),
            compiler_params=_compiler_params("parallel"),
            name="residual_last",
        )(y, x, g_post.reshape(1, d)), None
    return pl.pallas_call(
        functools.partial(_residual_kernel, step=step),
        grid=(m // tm,),
        in_specs=[row, row, vec, vec],
        out_specs=[row, row],
        out_shape=[jax.ShapeDtypeStruct((m, d), _F32), jax.ShapeDtypeStruct((m, d), _BF16)],
        compiler_params=_compiler_params("parallel"),
        name="residual",
    )(y, x, g_post.reshape(1, d), g_pre_next.reshape(1, d))


def _weight_spec(w, block, index):
    lead = w[1]
    return pl.BlockSpec((None,) * len(lead) + block, lambda *g: lead + index(*g))


def _cast_specs(cast_srcs, steps, step_of):
    ins, outs, shapes, arrays = [], [], [], []
    for arr, lead in cast_srcs:
        rows, cols = arr.shape[-2:]
        slab = rows // steps
        assert rows == slab * steps and slab % (2 * _SUBLANES) == 0, (rows, steps)
        ins.append(pl.BlockSpec((None,) * len(lead) + (slab, cols),
                                lambda *g, lead=lead: lead + (step_of(*g), 0)))
        outs.append(pl.BlockSpec((slab, cols), lambda *g: (step_of(*g), 0)))
        shapes.append(jax.ShapeDtypeStruct((rows, cols), _BF16))
        arrays.append(arr)
    return ins, outs, shapes, arrays


def _cast_slabs(refs):
    n = len(refs) // 2
    for src_ref, dst_ref in zip(refs[:n], refs[n:]):
        dst_ref[...] = src_ref[...].astype(dst_ref.dtype)


def _mm_kernel(a_ref, w_ref, *refs, n_cast, scaled_col_blocks, scale):
    o_ref = refs[n_cast]
    acc = jnp.dot(a_ref[...].astype(_BF16), w_ref[...], preferred_element_type=_F32)
    if scaled_col_blocks:
        acc = acc * jnp.where(pl.program_id(1) < scaled_col_blocks, scale, 1.0)
    o_ref[...] = acc.astype(o_ref.dtype)
    _cast_slabs(refs[:n_cast] + refs[n_cast + 1:])


def _mm(a, w, out_dtype, *, tm=1024, tn=1024, a_col_block=None, scaled_cols=0, scale=1.0, cast_srcs=()):
    m = a.shape[0]
    k, n = w[0].shape[-2:]
    tm, tn = _tile(m, tm), _tile(n, tn)
    a_col = 0
    if a_col_block is not None:
        a_col, width = a_col_block
        assert width == k
    else:
        assert a.shape[1] == k
    assert scaled_cols % tn == 0
    nj = n // tn
    c_in, c_out, c_shape, c_arr = _cast_specs(cast_srcs, (m // tm) * nj, lambda i, j: i * nj + j)
    out = pl.pallas_call(
        functools.partial(_mm_kernel, n_cast=len(c_arr), scaled_col_blocks=scaled_cols // tn, scale=scale),
        grid=(m // tm, nj),
        in_specs=[pl.BlockSpec((tm, k), lambda i, j: (i, a_col)),
                  _weight_spec(w, (k, tn), lambda i, j: (0, j))] + c_in,
        out_specs=[pl.BlockSpec((tm, tn), lambda i, j: (i, j))] + c_out,
        out_shape=[jax.ShapeDtypeStruct((m, n), out_dtype)] + c_shape,
        compiler_params=_compiler_params("parallel", "arbitrary"),
        name="matmul",
    )(a, w[0], *c_arr)
    return out if c_arr else out[0]


def _gate_up_kernel(h_ref, wg_ref, wu_ref, *refs, n_cast):
    o_ref = refs[n_cast]
    h = h_ref[...]
    g = jnp.dot(h, wg_ref[...], preferred_element_type=_F32)
    u = jnp.dot(h, wu_ref[...], preferred_element_type=_F32)
    o_ref[...] = (g * jax.nn.sigmoid(g) * u).astype(o_ref.dtype)
    _cast_slabs(refs[:n_cast] + refs[n_cast + 1:])


def _gate_up(h, wg, wu, *, tm=1024, tn=512, cast_srcs=()):
    m, k = h.shape
    n = wg[0].shape[-1]
    tm, tn = _tile(m, tm), _tile(n, tn)
    nj = n // tn
    c_in, c_out, c_shape, c_arr = _cast_specs(cast_srcs, (m // tm) * nj, lambda i, j: i * nj + j)
    out = pl.pallas_call(
        functools.partial(_gate_up_kernel, n_cast=len(c_arr)),
        grid=(m // tm, nj),
        in_specs=[pl.BlockSpec((tm, k), lambda i, j: (i, 0)),
                  _weight_spec(wg, (k, tn), lambda i, j: (0, j)),
                  _weight_spec(wu, (k, tn), lambda i, j: (0, j))] + c_in,
        out_specs=[pl.BlockSpec((tm, tn), lambda i, j: (i, j))] + c_out,
        out_shape=[jax.ShapeDtypeStruct((m, n), _BF16)] + c_shape,
        compiler_params=_compiler_params("parallel", "arbitrary"),
        name="gate_up",
    )(h, wg[0], wu[0], *c_arr)
    return out if c_arr else out[0]


def _mm_cat_kernel(a_ref, f_ref, w_ref, o_ref):
    ka = a_ref.shape[1]
    acc = jnp.dot(a_ref[...], w_ref[:ka, :], preferred_element_type=_F32)
    acc = acc + jnp.dot(f_ref[...], w_ref[ka:, :], preferred_element_type=_F32)
    o_ref[...] = acc.astype(o_ref.dtype)


def _mm_cat(a, f, w, *, tm=1024, tn=1024):
    m, ka = a.shape
    kf = f.shape[1]
    k, n = w[0].shape[-2:]
    assert ka + kf == k
    tm, tn = _tile(m, tm), _tile(n, tn)
    return pl.pallas_call(
        _mm_cat_kernel,
        grid=(m // tm, n // tn),
        in_specs=[pl.BlockSpec((tm, ka), lambda i, j: (i, 0)),
                  pl.BlockSpec((tm, kf), lambda i, j: (i, 0)),
                  _weight_spec(w, (k, tn), lambda i, j: (0, j))],
        out_specs=pl.BlockSpec((tm, tn), lambda i, j: (i, j)),
        out_shape=jax.ShapeDtypeStruct((m, n), _F32),
        compiler_params=_compiler_params("parallel", "arbitrary"),
        name="matmul_cat",
    )(a, f, w[0])


_POOL_HALO = 8
_POOL_ROWS = 256


def _pool_kernel(prev_ref, x_ref, next_ref, o_ref, *, seq, group):
    ts = x_ref.shape[0]
    ext_rows = ts + 2 * _POOL_HALO
    tile = pl.program_id(0) % (seq // ts)
    t = tile * ts + lax.broadcasted_iota(jnp.int32, (ts, 1), 0)
    first, last = tile == 0, tile == seq // ts - 1
    for g, w in enumerate(_POOL_WINDOWS):
        cols = slice(g * group, (g + 1) * group)
        x = x_ref[:, cols]
        run = jnp.concatenate([jnp.where(first, 0.0, prev_ref[:, cols]), x,
                               jnp.where(last, 0.0, next_ref[:, cols])], axis=0)
        span = 1
        while span < w:
            run = run + pltpu.roll(run, ext_rows - span, 0)
            span *= 2
        total = pltpu.roll(run, w // 2, 0)[_POOL_HALO:_POOL_HALO + ts]
        lo = jnp.clip(t - w // 2, 0, seq)
        hi = jnp.clip(t - w // 2 + w, 0, seq)
        o_ref[:, cols] = (total / (hi - lo).astype(_F32) - x).astype(o_ref.dtype)


def _pool(u, seq, pool_width):
    m = u.shape[0]
    group = pool_width // len(_POOL_WINDOWS)
    assert max(_POOL_WINDOWS) // 2 <= _POOL_HALO
    ts = _tile(seq, _POOL_ROWS)
    per = ts // _POOL_HALO
    last = m // _POOL_HALO - 1
    return pl.pallas_call(
        functools.partial(_pool_kernel, seq=seq, group=group),
        grid=(m // ts,),
        in_specs=[pl.BlockSpec((_POOL_HALO, pool_width), lambda i: (jnp.maximum(i * per - 1, 0), 0)),
                  pl.BlockSpec((ts, pool_width), lambda i: (i, 0)),
                  pl.BlockSpec((_POOL_HALO, pool_width), lambda i: (jnp.minimum((i + 1) * per, last), 0))],
        out_specs=pl.BlockSpec((ts, pool_width), lambda i: (i, 0)),
        out_shape=jax.ShapeDtypeStruct((m, pool_width), _BF16),
        compiler_params=_compiler_params("parallel"),
        name="pool",
    )(u, u, u)


def _pool_proj_kernel(p_ref, w_ref, s_ref, o_ref):
    acc = jnp.dot(p_ref[...], w_ref[...], preferred_element_type=_F32)
    o_ref[...] = (acc * s_ref[...]).astype(o_ref.dtype)


def _pool_proj(pooled, w_pool, pool_scale, *, tm=1024):
    m, width = pooled.shape
    groups, group, _ = w_pool[0].shape[-3:]
    tm = _tile(m, tm)
    return pl.pallas_call(
        _pool_proj_kernel,
        grid=(m // tm, groups),
        in_specs=[pl.BlockSpec((tm, group), lambda i, g: (i, g)),
                  _weight_spec(w_pool, (None, group, group), lambda i, g: (g, 0, 0)),
                  pl.BlockSpec((1, group), lambda i, g: (0, g))],
        out_specs=pl.BlockSpec((tm, group), lambda i, g: (i, g)),
        out_shape=jax.ShapeDtypeStruct((m, width), _BF16),
        compiler_params=_compiler_params("parallel", "arbitrary"),
        name="pool_proj",
    )(pooled, w_pool[0], pool_scale.reshape(1, width))


_DFT_LO_ROWS = 64


def _dft_rows(rows, n):
    idx = (rows[:, None] * jnp.arange(n, dtype=jnp.int32)[None, :]) % n
    ang = idx.astype(_F32) * (2.0 * math.pi / n)
    return jnp.cos(ang), jnp.sin(ang)


def _dft_tables(n):
    lo_rows = _DFT_LO_ROWS if n % _DFT_LO_ROWS == 0 and n > _DFT_LO_ROWS else 1
    c_hi, s_hi = _dft_rows(jnp.arange(n // lo_rows, dtype=jnp.int32) * lo_rows, n)
    c_lo, s_lo = _dft_rows(jnp.arange(lo_rows, dtype=jnp.int32), n)
    cos = c_hi[:, None, :] * c_lo[None, :, :] - s_hi[:, None, :] * s_lo[None, :, :]
    sin = s_hi[:, None, :] * c_lo[None, :, :] + c_hi[:, None, :] * s_lo[None, :, :]
    return cos.reshape(n, n), sin.reshape(n, n)


def _seq_dft_kernel(c_ref, s_ref, xc_ref, xs_ref, o_ref, acc_ref, *, norm):
    k = pl.program_id(3)

    @pl.when(k == 0)
    def _():
        acc_ref[...] = jnp.zeros_like(acc_ref)

    acc_ref[...] += (jnp.dot(c_ref[...], xc_ref[...], preferred_element_type=_F32)
                     - jnp.dot(s_ref[...], xs_ref[...], preferred_element_type=_F32))

    @pl.when(k == pl.num_programs(3) - 1)
    def _():
        o_ref[...] = (acc_ref[...] * norm).astype(o_ref.dtype)


def _seq_dft(cos_s, sin_s, xcs, batch, seq, width, norm, *, tm=1024, tn=1024, tk=1024):
    tm, tn, tk = _tile(seq, tm), _tile(width, tn), _tile(seq, tk)
    nj = width // tn
    rows_k = seq // tk
    rows_m = seq // tm
    return pl.pallas_call(
        functools.partial(_seq_dft_kernel, norm=norm),
        grid=(batch, rows_m, nj, rows_k),
        in_specs=[pl.BlockSpec((tm, tk), lambda b, i, j, k: (i, k)),
                  pl.BlockSpec((tm, tk), lambda b, i, j, k: (i, k)),
                  pl.BlockSpec((tk, tn), lambda b, i, j, k: (b * rows_k + k, j)),
                  pl.BlockSpec((tk, tn), lambda b, i, j, k: (b * rows_k + k, nj + j))],
        out_specs=pl.BlockSpec((tm, tn), lambda b, i, j, k: (b * rows_m + i, j)),
        out_shape=jax.ShapeDtypeStruct((batch * seq, width), _BF16),
        scratch_shapes=[pltpu.VMEM((tm, tn), _F32)],
        compiler_params=_compiler_params("parallel", "parallel", "parallel", "arbitrary"),
        name="seq_dft",
    )(cos_s, sin_s, xcs, xcs)


_LOG2E = math.log2(math.e)
_RAMP_PARTS = 3
_SIGNS = (1.0, -1.0, 0.0)
_OWN = 2
_WIDEN_KEYS = 256
_BIAS_ROWS = 64


def _diff_attn_kernel(slope_ref, q_ref, k_ref, v_ref, lq1_ref, lk1_ref, lq2_ref, lk2_ref, gsub_ref,
                      o_ref, kaug_ref, bias_ref, qaug_ref, s_ref, p_ref, alpha_ref, sub_ref,
                      mx_ref, m_ref, l_ref, acc_ref, *, lambda_init, tk):
    h = pl.program_id(1)
    qi = pl.program_id(2)
    tq, dv = q_ref.shape
    dh = dv // 2
    seq = k_ref.shape[0]
    n_chunks = seq // tk
    n_col = tk // _LANES
    n_row = tq // _SUBLANES
    assert n_chunks % 2 == 0 and tq == tk
    slope = slope_ref[h]
    c_own = (qi * tq) // tk

    @pl.when(qi == 0)
    def _():
        kb = min(tk, _WIDEN_KEYS)

        def widen(i, carry):
            start = pl.multiple_of(i * kb, kb)
            local = (i % (tk // kb)) * kb + lax.broadcasted_iota(jnp.int32, (dh, kb), 1)
            rest = slope * local.astype(_F32)
            row = lax.broadcasted_iota(jnp.int32, (dh, kb), 0)
            ext = jnp.zeros((dh, kb), _F32)
            for part in range(_RAMP_PARTS):
                term = rest.astype(_BF16).astype(_F32)
                ext = jnp.where(row == part, term, ext)
                rest = rest - term
            ext = jnp.where((row >= _RAMP_PARTS) & (row < 2 * _RAMP_PARTS), 1.0, ext)
            for half in range(2):
                k_half = k_ref[pl.ds(start, kb), half * dh:(half + 1) * dh]
                kaug_ref[half, :dh, pl.ds(start, kb)] = k_half.T
                kaug_ref[half, dh:, pl.ds(start, kb)] = ext.astype(_BF16)
            return carry

        lax.fori_loop(0, seq // kb, widen, 0)

        for r in range(0, tq, _BIAS_ROWS):
            dist = (lax.broadcasted_iota(jnp.int32, (_BIAS_ROWS, tk), 0) + r
                    - lax.broadcasted_iota(jnp.int32, (_BIAS_ROWS, tk), 1))
            bias_ref[r:r + _BIAS_ROWS] = -slope * jnp.abs(dist).astype(_F32)

    ramp_q = slope * lax.broadcasted_iota(jnp.int32, (tq, _LANES), 0).astype(_F32)
    lane_q = lax.broadcasted_iota(jnp.int32, (tq, _LANES), 1)
    ramp_q_parts, rest = [], ramp_q
    for part in range(_RAMP_PARTS):
        ramp_q_parts.append(rest.astype(_BF16).astype(_F32))
        rest = rest - ramp_q_parts[-1]
    for variant, sign in enumerate(_SIGNS):
        ext = jnp.where(lane_q < _RAMP_PARTS, sign, 0.0)
        for part in range(_RAMP_PARTS):
            ext = jnp.where(lane_q == _RAMP_PARTS + part, -sign * ramp_q_parts[part], ext)
        for half in range(2):
            qaug_ref[variant, half] = jnp.concatenate(
                [q_ref[:, half * dh:(half + 1) * dh], ext.astype(_BF16)], axis=1)

    m_ref[...] = jnp.full_like(m_ref, _NEG_BIG)
    l_ref[...] = jnp.zeros_like(l_ref)
    acc_ref[...] = jnp.zeros_like(acc_ref)

    def chunk_of(j):
        rest = jnp.where(j - 1 >= c_own, j, j - 1)
        return jnp.where(j == 0, c_own, rest)

    def group(g):
        return slice(g * _SUBLANES, (g + 1) * _SUBLANES)

    def scores(c, slot, own):
        start = pl.multiple_of(c * tk, tk)
        variant = _OWN if own else jnp.where(c < c_own, 0, 1)
        for half in range(2):
            s_ref[slot, half] = jnp.dot(qaug_ref[variant, half], kaug_ref[half, :, pl.ds(start, tk)],
                                        preferred_element_type=_F32)

    def values(c, slot):
        start = pl.multiple_of(c * tk, tk)
        v = v_ref[pl.ds(start, tk), :]
        for half in range(2):
            alpha = alpha_ref[slot, half]
            acc_ref[half] = (jnp.concatenate([alpha] * (dv // _LANES), axis=1) * acc_ref[half]
                             + jnp.dot(p_ref[slot, half], v, preferred_element_type=_F32))

    def softmax(c, slot, own):
        if own:
            kappa = 0.0
        else:
            shift = slope * (qi * tq - c * tk).astype(_F32)
            kappa = jnp.where(c < c_own, -shift, shift)

        def biased(half, g, j):
            lanes = slice(j * _LANES, (j + 1) * _LANES)
            s = s_ref[slot, half, group(g), lanes]
            return s + bias_ref[group(g), lanes] if own else s

        for half in range(2):
            if own:
                for g in range(n_row):
                    mx = biased(half, g, 0)
                    for j in range(1, n_col):
                        mx = jnp.maximum(mx, biased(half, g, j))
                    mx_ref[group(g)] = mx
                row_max = jnp.max(mx_ref[...], axis=-1, keepdims=True)
            else:
                row_max = jnp.max(s_ref[slot, half], axis=-1, keepdims=True)
            m_old = m_ref[half]
            m_new = jnp.maximum(m_old, row_max + kappa)
            alpha_ref[slot, half] = jnp.exp2(m_old - m_new)
            m_ref[half] = m_new
            sub_ref[...] = m_new - kappa
            for g in range(0, n_row, 2):
                pair_rows = slice(g * _SUBLANES, (g + 2) * _SUBLANES)
                sub = [sub_ref[group(g + i)] for i in range(2)]
                tot = [None, None]
                for j in range(n_col):
                    ps = [jnp.exp2(biased(half, g + i, j) - sub[i]) for i in range(2)]
                    tot = [p if t is None else t + p for t, p in zip(tot, ps)]
                    p_ref[slot, half, pair_rows, j * _LANES:(j + 1) * _LANES] = (
                        jnp.concatenate(ps, axis=0).astype(p_ref.dtype))
                for i in range(2):
                    l_ref[half, group(g + i)] = (alpha_ref[slot, half, group(g + i)] * l_ref[half, group(g + i)]
                                                 + tot[i])

    def phase(score_job=None, softmax_job=None, value_job=None):
        if score_job is not None:
            scores(*score_job)
        if softmax_job is not None:
            softmax(*softmax_job)
        if value_job is not None:
            values(*value_job)

    phase(score_job=(c_own, 0, True))
    phase(score_job=(chunk_of(1), 1, False), softmax_job=(c_own, 0, True))

    def pair(jj, carry):
        j = 2 * jj + 1
        phase((chunk_of(j + 1), 0, False), (chunk_of(j), 1, False), (chunk_of(j - 1), 0))
        phase((chunk_of(j + 2), 1, False), (chunk_of(j + 1), 0, False), (chunk_of(j), 1))
        return carry

    lax.fori_loop(0, (n_chunks - 2) // 2, pair, 0)
    phase(softmax_job=(chunk_of(n_chunks - 1), 1, False), value_job=(chunk_of(n_chunks - 2), 0))
    phase(value_job=(chunk_of(n_chunks - 1), 1))

    lam = (jnp.exp(jnp.sum(lq1_ref[...] * lk1_ref[...], keepdims=True))
           - jnp.exp(jnp.sum(lq2_ref[...] * lk2_ref[...], keepdims=True)) + lambda_init)
    l0 = jnp.sum(l_ref[0], axis=-1, keepdims=True)
    l1 = jnp.sum(l_ref[1], axis=-1, keepdims=True)
    o = acc_ref[0] / l0 - lam * (acc_ref[1] / l1)
    o_ref[...] = (_rms(o, gsub_ref[...]) * (1.0 - lambda_init)).astype(o_ref.dtype)


def _diff_attn(qkv, batch, seq, lq1, lk1, lq2, lk2, g_sub, lambda_init, *, tq=512, tk=512):
    m, d3 = qkv.shape
    d = d3 // 3
    heads = _ATTN_HEADS
    dv = d // heads
    dh = dv // 2
    tq, tk = _tile(seq, tq), _tile(seq, tk)
    assert tk == tq
    nq = seq // tq
    slopes = jnp.exp2(-8.0 * jnp.arange(1, heads + 1, dtype=_F32) / heads) * _LOG2E
    lam_spec = pl.BlockSpec((1, dh), lambda b, h, i: (0, 0))
    stat = pltpu.VMEM((tq, _LANES), _F32)
    return pl.pallas_call(
        functools.partial(_diff_attn_kernel, lambda_init=lambda_init, tk=tk),
        grid=(batch, heads, nq),
        in_specs=[pl.BlockSpec(memory_space=pltpu.SMEM),
                  pl.BlockSpec((tq, dv), lambda b, h, i: (b * nq + i, h)),
                  pl.BlockSpec((seq, dv), lambda b, h, i: (b, heads + h)),
                  pl.BlockSpec((seq, dv), lambda b, h, i: (b, 2 * heads + h)),
                  lam_spec, lam_spec, lam_spec, lam_spec,
                  pl.BlockSpec((1, dv), lambda b, h, i: (0, 0))],
        out_specs=pl.BlockSpec((tq, dv), lambda b, h, i: (b * nq + i, h)),
        out_shape=jax.ShapeDtypeStruct((m, d), _BF16),
        scratch_shapes=[pltpu.VMEM((2, 2 * dh, seq), _BF16),
                        pltpu.VMEM((tq, tk), _F32),
                        pltpu.VMEM((len(_SIGNS), 2, tq, 2 * dh), _BF16),
                        pltpu.VMEM((2, 2, tq, tk), _F32),
                        pltpu.VMEM((2, 2, tq, tk), _BF16),
                        pltpu.VMEM((2, 2, tq, _LANES), _F32),
                        stat, stat,
                        pltpu.VMEM((2, tq, _LANES), _F32),
                        pltpu.VMEM((2, tq, _LANES), _F32),
                        pltpu.VMEM((2, tq, dv), _F32)],
        compiler_params=_compiler_params("parallel", "parallel", "arbitrary"),
        name="diff_attn",
    )(slopes, qkv, qkv, qkv, lq1.reshape(1, dh), lk1.reshape(1, dh), lq2.reshape(1, dh),
      lk2.reshape(1, dh), g_sub.reshape(1, dv))


def _ffn(h, wg, wu, wd_f32, next_f32):
    a, wd = _gate_up(h, wg, wu, cast_srcs=(wd_f32,))
    y, *cast = _mm(a, (wd, ()), _F32, tm=1024, tn=512, cast_srcs=next_f32) if next_f32 else [
        _mm(a, (wd, ()), _F32, tm=1024, tn=512)]
    return y, [(w, ()) for w in cast]


def _pool_fourier(h, batch, seq, w_in, w_pool, pool_scale, w_fourier, w_out, dft):
    cos_s, sin_s, chan_dft = dft
    pool_width = pool_scale.shape[0]
    four_width = w_fourier[0].shape[-1]
    u = _mm(h, w_in, _F32)
    a = _pool_proj(_pool(u, seq, pool_width), w_pool, pool_scale)
    xcs = _mm(u, (chan_dft, ()), _BF16, a_col_block=(pool_width // four_width, four_width))
    head_dim = four_width // _FOURIER_HEADS
    f = _seq_dft(cos_s, sin_s, xcs, batch, seq, four_width, 1.0 / math.sqrt(seq * head_dim))
    f = _mm(f, w_fourier, _BF16)
    return _mm_cat(a, f, w_out)


def kernel(x, norm_pre, norm_post, w_ffn_gate, w_ffn_up, w_ffn_down, w_mix_in, w_pool, pool_scale,
           w_fourier, w_mix_out, w_qkv, w_attn_out, lambda_q1, lambda_k1, lambda_q2, lambda_k2,
           subln_gain):
    batch, seq, d = x.shape
    depth = norm_pre.shape[0]
    m = batch * seq
    x = x.reshape(m, d)
    bf = lambda w: w.astype(_BF16)

    four_width = w_fourier.shape[1]
    head_dim = four_width // _FOURIER_HEADS
    cos_s, sin_s = _dft_tables(seq)
    cos_c, sin_c = _dft_tables(head_dim)
    eye = jnp.eye(_FOURIER_HEADS, dtype=_F32)
    chan_dft = jnp.concatenate([jnp.kron(eye, cos_c), jnp.kron(eye, sin_c)], axis=1)
    dft = (bf(cos_s), bf(sin_s), bf(chan_dft))

    attn_head_dim = d // (2 * _ATTN_HEADS)
    w_pl, w_fr = bf(w_pool), bf(w_fourier)
    wg, wu = (bf(w_ffn_gate[0, 0]), ()), (bf(w_ffn_up[0, 0]), ())

    h = _prenorm(x, norm_pre[0, 0])
    for l in range(depth):
        i = l // 2
        mixer = ([(w_mix_in, (i,)), (w_mix_out, (i,))] if l % 2 == 0 else
                 [(w_qkv, (i,)), (w_attn_out, (i,))])
        y, (wg, wu, w_first, w_last) = _ffn(h, wg, wu, (w_ffn_down, (l, 0)),
                                            [(w_ffn_gate, (l, 1)), (w_ffn_up, (l, 1))] + mixer)
        x, h = _residual(y, x, norm_post[l, 0], norm_pre[l, 1], 0.5)
        if l % 2 == 0:
            y = _pool_fourier(h, batch, seq, w_first, (w_pl, (i,)), pool_scale[i],
                              (w_fr, (i,)), w_last, dft)
        else:
            lambda_init = 0.8 - 0.6 * math.exp(-0.3 * l)
            qkv = _mm(h, w_first, _BF16, scaled_cols=d, scale=attn_head_dim ** -0.5 * _LOG2E)
            o = _diff_attn(qkv, batch, seq, lambda_q1[i], lambda_k1[i], lambda_q2[i], lambda_k2[i],
                           subln_gain[i], lambda_init)
            y = _mm(o, w_last, _F32)
        x, h = _residual(y, x, norm_post[l, 1], norm_pre[l, 2], 1.0)
        following = [(w_ffn_gate, (l + 1, 0)), (w_ffn_up, (l + 1, 0))] if l + 1 < depth else []
        y, cast = _ffn(h, wg, wu, (w_ffn_down, (l, 1)), following)
        if cast:
            wg, wu = cast
        g_next = norm_pre[l + 1, 0] if l + 1 < depth else None
        x, h = _residual(y, x, norm_post[l, 2], g_next, 0.5)
    return x.reshape(batch, seq, d)
```

```python
import functools
import math

import jax
import jax.numpy as jnp
from jax import lax
from jax.experimental import pallas as pl
from jax.experimental.pallas import tpu as pltpu

_V7X_VMEM_LIMIT_BYTES = 56 * 1024 * 1024
_SUBLANES = 8
_LANES = 128

_EPS = 1e-6
_POOL_WINDOWS = (2, 4, 8, 16)
_FOURIER_HEADS = 4
_ATTN_HEADS = 16
_NEG_BIG = -1e30

_BF16 = jnp.bfloat16
_F32 = jnp.float32


def _compiler_params(*semantics, flags=None):
    return pltpu.CompilerParams(dimension_semantics=semantics,
                                vmem_limit_bytes=_V7X_VMEM_LIMIT_BYTES, flags=flags)


def _tile(n, want):
    t = min(n, want)
    assert n % t == 0, (n, want)
    return t


def _rms(x, g):
    return x * lax.rsqrt(jnp.mean(x * x, axis=-1, keepdims=True) + _EPS) * g


_NORM_ROWS = 256


def _prenorm_kernel(x_ref, g_ref, h_ref):
    h_ref[...] = _rms(x_ref[...], g_ref[...]).astype(h_ref.dtype)


def _prenorm(x, g):
    m, d = x.shape
    tm = _tile(m, _NORM_ROWS)
    return pl.pallas_call(
        _prenorm_kernel,
        grid=(m // tm,),
        in_specs=[pl.BlockSpec((tm, d), lambda i: (i, 0)),
                  pl.BlockSpec((1, d), lambda i: (0, 0))],
        out_specs=pl.BlockSpec((tm, d), lambda i: (i, 0)),
        out_shape=jax.ShapeDtypeStruct((m, d), _BF16),
        compiler_params=_compiler_params("parallel"),
        name="prenorm",
    )(x, g.reshape(1, d))


def _residual_kernel(y_ref, x_ref, gpost_ref, gpre_ref, xo_ref, h_ref, *, step):
    xn = x_ref[...] + step * _rms(y_ref[...], gpost_ref[...])
    xo_ref[...] = xn
    h_ref[...] = _rms(xn, gpre_ref[...]).astype(h_ref.dtype)


def _residual_last_kernel(y_ref, x_ref, gpost_ref, xo_ref, *, step):
    xo_ref[...] = x_ref[...] + step * _rms(y_ref[...], gpost_ref[...])


def _residual(y, x, g_post, g_pre_next, step):
    m, d = x.shape
    tm = _tile(m, _NORM_ROWS)
    row = pl.BlockSpec((tm, d), lambda i: (i, 0))
    vec = pl.BlockSpec((1, d), lambda i: (0, 0))
    if g_pre_next is None:
        return pl.pallas_call(
            functools.partial(_residual_last_kernel, step=step),
            grid=(m // tm,),
            in_specs=[row, row, vec],
            out_specs=row,
            out_shape=jax.ShapeDtypeStruct((m, d), _F32),
            compiler_params=_compiler_params("parallel"),
            name="residual_last",
        )(y, x, g_post.reshape(1, d)), None
    return pl.pallas_call(
        functools.partial(_residual_kernel, step=step),
        grid=(m // tm,),
        in_specs=[row, row, vec, vec],
        out_specs=[row, row],
        out_shape=[jax.ShapeDtypeStruct((m, d), _F32), jax.ShapeDtypeStruct((m, d), _BF16)],
        compiler_params=_compiler_params("parallel"),
        name="residual",
    )(y, x, g_post.reshape(1, d), g_pre_next.reshape(1, d))


def _weight_spec(w, block, index):
    lead = w[1]
    return pl.BlockSpec((None,) * len(lead) + block, lambda *g: lead + index(*g))


def _cast_specs(cast_srcs, steps, step_of):
    ins, outs, shapes, arrays = [], [], [], []
    for arr, lead in cast_srcs:
        rows, cols = arr.shape[-2:]
        slab = rows // steps
        assert rows == slab * steps and slab % (2 * _SUBLANES) == 0, (rows, steps)
        ins.append(pl.BlockSpec((None,) * len(lead) + (slab, cols),
                                lambda *g, lead=lead: lead + (step_of(*g), 0)))
        outs.append(pl.BlockSpec((slab, cols), lambda *g: (step_of(*g), 0)))
        shapes.append(jax.ShapeDtypeStruct((rows, cols), _BF16))
        arrays.append(arr)
    return ins, outs, shapes, arrays


def _cast_slabs(refs):
    n = len(refs) // 2
    for src_ref, dst_ref in zip(refs[:n], refs[n:]):
        dst_ref[...] = src_ref[...].astype(dst_ref.dtype)


def _mm_kernel(a_ref, w_ref, *refs, n_cast, scaled_col_blocks, scale):
    o_ref = refs[n_cast]
    acc = jnp.dot(a_ref[...].astype(_BF16), w_ref[...], preferred_element_type=_F32)
    if scaled_col_blocks:
        acc = acc * jnp.where(pl.program_id(1) < scaled_col_blocks, scale, 1.0)
    o_ref[...] = acc.astype(o_ref.dtype)
    _cast_slabs(refs[:n_cast] + refs[n_cast + 1:])


def _mm(a, w, out_dtype, *, tm=1024, tn=1024, a_col_block=None, scaled_cols=0, scale=1.0, cast_srcs=()):
    m = a.shape[0]
    k, n = w[0].shape[-2:]
    tm, tn = _tile(m, tm), _tile(n, tn)
    a_col = 0
    if a_col_block is not None:
        a_col, width = a_col_block
        assert width == k
    else:
        assert a.shape[1] == k
    assert scaled_cols % tn == 0
    nj = n // tn
    c_in, c_out, c_shape, c_arr = _cast_specs(cast_srcs, (m // tm) * nj, lambda i, j: i * nj + j)
    out = pl.pallas_call(
        functools.partial(_mm_kernel, n_cast=len(c_arr), scaled_col_blocks=scaled_cols // tn, scale=scale),
        grid=(m // tm, nj),
        in_specs=[pl.BlockSpec((tm, k), lambda i, j: (i, a_col)),
                  _weight_spec(w, (k, tn), lambda i, j: (0, j))] + c_in,
        out_specs=[pl.BlockSpec((tm, tn), lambda i, j: (i, j))] + c_out,
        out_shape=[jax.ShapeDtypeStruct((m, n), out_dtype)] + c_shape,
        compiler_params=_compiler_params("parallel", "arbitrary"),
        name="matmul",
    )(a, w[0], *c_arr)
    return out if c_arr else out[0]


def _gate_up_kernel(h_ref, wg_ref, wu_ref, *refs, n_cast):
    o_ref = refs[n_cast]
    h = h_ref[...]
    g = jnp.dot(h, wg_ref[...], preferred_element_type=_F32)
    u = jnp.dot(h, wu_ref[...], preferred_element_type=_F32)
    o_ref[...] = (g * jax.nn.sigmoid(g) * u).astype(o_ref.dtype)
    _cast_slabs(refs[:n_cast] + refs[n_cast + 1:])


def _gate_up(h, wg, wu, *, tm=1024, tn=512, cast_srcs=()):
    m, k = h.shape
    n = wg[0].shape[-1]
    tm, tn = _tile(m, tm), _tile(n, tn)
    nj = n // tn
    c_in, c_out, c_shape, c_arr = _cast_specs(cast_srcs, (m // tm) * nj, lambda i, j: i * nj + j)
    out = pl.pallas_call(
        functools.partial(_gate_up_kernel, n_cast=len(c_arr)),
        grid=(m // tm, nj),
        in_specs=[pl.BlockSpec((tm, k), lambda i, j: (i, 0)),
                  _weight_spec(wg, (k, tn), lambda i, j: (0, j)),
                  _weight_spec(wu, (k, tn), lambda i, j: (0, j))] + c_in,
        out_specs=[pl.BlockSpec((tm, tn), lambda i, j: (i, j))] + c_out,
        out_shape=[jax.ShapeDtypeStruct((m, n), _BF16)] + c_shape,
        compiler_params=_compiler_params("parallel", "arbitrary"),
        name="gate_up",
    )(h, wg[0], wu[0], *c_arr)
    return out if c_arr else out[0]


def _mm_cat_kernel(a_ref, f_ref, w_ref, o_ref):
    ka = a_ref.shape[1]
    acc = jnp.dot(a_ref[...], w_ref[:ka, :], preferred_element_type=_F32)
    acc = acc + jnp.dot(f_ref[...], w_ref[ka:, :], preferred_element_type=_F32)
    o_ref[...] = acc.astype(o_ref.dtype)


def _mm_cat(a, f, w, *, tm=1024, tn=1024):
    m, ka = a.shape
    kf = f.shape[1]
    k, n = w[0].shape[-2:]
    assert ka + kf == k
    tm, tn = _tile(m, tm), _tile(n, tn)
    return pl.pallas_call(
        _mm_cat_kernel,
        grid=(m // tm, n // tn),
        in_specs=[pl.BlockSpec((tm, ka), lambda i, j: (i, 0)),
                  pl.BlockSpec((tm, kf), lambda i, j: (i, 0)),
                  _weight_spec(w, (k, tn), lambda i, j: (0, j))],
        out_specs=pl.BlockSpec((tm, tn), lambda i, j: (i, j)),
        out_shape=jax.ShapeDtypeStruct((m, n), _F32),
        compiler_params=_compiler_params("parallel", "arbitrary"),
        name="matmul_cat",
    )(a, f, w[0])


_POOL_HALO = 8
_POOL_ROWS = 256


def _pool_kernel(prev_ref, x_ref, next_ref, o_ref, *, seq, group):
    ts = x_ref.shape[0]
    ext_rows = ts + 2 * _POOL_HALO
    tile = pl.program_id(0) % (seq // ts)
    t = tile * ts + lax.broadcasted_iota(jnp.int32, (ts, 1), 0)
    first, last = tile == 0, tile == seq // ts - 1
    for g, w in enumerate(_POOL_WINDOWS):
        cols = slice(g * group, (g + 1) * group)
        x = x_ref[:, cols]
        run = jnp.concatenate([jnp.where(first, 0.0, prev_ref[:, cols]), x,
                               jnp.where(last, 0.0, next_ref[:, cols])], axis=0)
        span = 1
        while span < w:
            run = run + pltpu.roll(run, ext_rows - span, 0)
            span *= 2
        total = pltpu.roll(run, w // 2, 0)[_POOL_HALO:_POOL_HALO + ts]
        lo = jnp.clip(t - w // 2, 0, seq)
        hi = jnp.clip(t - w // 2 + w, 0, seq)
        o_ref[:, cols] = (total / (hi - lo).astype(_F32) - x).astype(o_ref.dtype)


def _pool(u, seq, pool_width):
    m = u.shape[0]
    group = pool_width // len(_POOL_WINDOWS)
    assert max(_POOL_WINDOWS) // 2 <= _POOL_HALO
    ts = _tile(seq, _POOL_ROWS)
    per = ts // _POOL_HALO
    last = m // _POOL_HALO - 1
    return pl.pallas_call(
        functools.partial(_pool_kernel, seq=seq, group=group),
        grid=(m // ts,),
        in_specs=[pl.BlockSpec((_POOL_HALO, pool_width), lambda i: (jnp.maximum(i * per - 1, 0), 0)),
                  pl.BlockSpec((ts, pool_width), lambda i: (i, 0)),
                  pl.BlockSpec((_POOL_HALO, pool_width), lambda i: (jnp.minimum((i + 1) * per, last), 0))],
        out_specs=pl.BlockSpec((ts, pool_width), lambda i: (i, 0)),
        out_shape=jax.ShapeDtypeStruct((m, pool_width), _BF16),
        compiler_params=_compiler_params("parallel"),
        name="pool",
    )(u, u, u)


def _pool_proj_kernel(p_ref, w_ref, s_ref, o_ref):
    acc = jnp.dot(p_ref[...], w_ref[...], preferred_element_type=_F32)
    o_ref[...] = (acc * s_ref[...]).astype(o_ref.dtype)


def _pool_proj(pooled, w_pool, pool_scale, *, tm=1024):
    m, width = pooled.shape
    groups, group, _ = w_pool[0].shape[-3:]
    tm = _tile(m, tm)
    return pl.pallas_call(
        _pool_proj_kernel,
        grid=(m // tm, groups),
        in_specs=[pl.BlockSpec((tm, group), lambda i, g: (i, g)),
                  _weight_spec(w_pool, (None, group, group), lambda i, g: (g, 0, 0)),
                  pl.BlockSpec((1, group), lambda i, g: (0, g))],
        out_specs=pl.BlockSpec((tm, group), lambda i, g: (i, g)),
        out_shape=jax.ShapeDtypeStruct((m, width), _BF16),
        compiler_params=_compiler_params("parallel", "arbitrary"),
        name="pool_proj",
    )(pooled, w_pool[0], pool_scale.reshape(1, width))


_DFT_LO_ROWS = 64


def _dft_rows(rows, n):
    idx = (rows[:, None] * jnp.arange(n, dtype=jnp.int32)[None, :]) % n
    ang = idx.astype(_F32) * (2.0 * math.pi / n)
    return jnp.cos(ang), jnp.sin(ang)


def _dft_tables(n):
    lo_rows = _DFT_LO_ROWS if n % _DFT_LO_ROWS == 0 and n > _DFT_LO_ROWS else 1
    c_hi, s_hi = _dft_rows(jnp.arange(n // lo_rows, dtype=jnp.int32) * lo_rows, n)
    c_lo, s_lo = _dft_rows(jnp.arange(lo_rows, dtype=jnp.int32), n)
    cos = c_hi[:, None, :] * c_lo[None, :, :] - s_hi[:, None, :] * s_lo[None, :, :]
    sin = s_hi[:, None, :] * c_lo[None, :, :] + c_hi[:, None, :] * s_lo[None, :, :]
    return cos.reshape(n, n), sin.reshape(n, n)


def _seq_dft_kernel(c_ref, s_ref, xc_ref, xs_ref, o_ref, acc_ref, *, norm):
    k = pl.program_id(3)

    @pl.when(k == 0)
    def _():
        acc_ref[...] = jnp.zeros_like(acc_ref)

    acc_ref[...] += (jnp.dot(c_ref[...], xc_ref[...], preferred_element_type=_F32)
                     - jnp.dot(s_ref[...], xs_ref[...], preferred_element_type=_F32))

    @pl.when(k == pl.num_programs(3) - 1)
    def _():
        o_ref[...] = (acc_ref[...] * norm).astype(o_ref.dtype)


def _seq_dft(cos_s, sin_s, xcs, batch, seq, width, norm, *, tm=1024, tn=1024, tk=1024):
    tm, tn, tk = _tile(seq, tm), _tile(width, tn), _tile(seq, tk)
    nj = width // tn
    rows_k = seq // tk
    rows_m = seq // tm
    return pl.pallas_call(
        functools.partial(_seq_dft_kernel, norm=norm),
        grid=(batch, rows_m, nj, rows_k),
        in_specs=[pl.BlockSpec((tm, tk), lambda b, i, j, k: (i, k)),
                  pl.BlockSpec((tm, tk), lambda b, i, j, k: (i, k)),
                  pl.BlockSpec((tk, tn), lambda b, i, j, k: (b * rows_k + k, j)),
                  pl.BlockSpec((tk, tn), lambda b, i, j, k: (b * rows_k + k, nj + j))],
        out_specs=pl.BlockSpec((tm, tn), lambda b, i, j, k: (b * rows_m + i, j)),
        out_shape=jax.ShapeDtypeStruct((batch * seq, width), _BF16),
        scratch_shapes=[pltpu.VMEM((tm, tn), _F32)],
        compiler_params=_compiler_params("parallel", "parallel", "parallel", "arbitrary"),
        name="seq_dft",
    )(cos_s, sin_s, xcs, xcs)


_LOG2E = math.log2(math.e)
_RAMP_PARTS = 3
_SIGNS = (1.0, -1.0, 0.0)
_OWN = 2
_WIDEN_KEYS = 256
_BIAS_ROWS = 64


def _diff_attn_kernel(slope_ref, q_ref, k_ref, v_ref, lq1_ref, lk1_ref, lq2_ref, lk2_ref, gsub_ref,
                      o_ref, kaug_ref, bias_ref, qaug_ref, s_ref, p_ref, alpha_ref, sub_ref,
                      mx_ref, m_ref, l_ref, acc_ref, *, lambda_init, tk):
    h = pl.program_id(1)
    qi = pl.program_id(2)
    tq, dv = q_ref.shape
    dh = dv // 2
    seq = k_ref.shape[0]
    n_chunks = seq // tk
    n_col = tk // _LANES
    n_row = tq // _SUBLANES
    assert n_chunks % 2 == 0 and tq == tk
    slope = slope_ref[h]
    c_own = (qi * tq) // tk

    @pl.when(qi == 0)
    def _():
        kb = min(tk, _WIDEN_KEYS)

        def widen(i, carry):
            start = pl.multiple_of(i * kb, kb)
            local = (i % (tk // kb)) * kb + lax.broadcasted_iota(jnp.int32, (dh, kb), 1)
            rest = slope * local.astype(_F32)
            row = lax.broadcasted_iota(jnp.int32, (dh, kb), 0)
            ext = jnp.zeros((dh, kb), _F32)
            for part in range(_RAMP_PARTS):
                term = rest.astype(_BF16).astype(_F32)
                ext = jnp.where(row == part, term, ext)
                rest = rest - term
            ext = jnp.where((row >= _RAMP_PARTS) & (row < 2 * _RAMP_PARTS), 1.0, ext)
            for half in range(2):
                k_half = k_ref[pl.ds(start, kb), half * dh:(half + 1) * dh]
                kaug_ref[half, :dh, pl.ds(start, kb)] = k_half.T
                kaug_ref[half, dh:, pl.ds(start, kb)] = ext.astype(_BF16)
            return carry

        lax.fori_loop(0, seq // kb, widen, 0)

        for r in range(0, tq, _BIAS_ROWS):
            dist = (lax.broadcasted_iota(jnp.int32, (_BIAS_ROWS, tk), 0) + r
                    - lax.broadcasted_iota(jnp.int32, (_BIAS_ROWS, tk), 1))
            bias_ref[r:r + _BIAS_ROWS] = -slope * jnp.abs(dist).astype(_F32)

    ramp_q = slope * lax.broadcasted_iota(jnp.int32, (tq, _LANES), 0).astype(_F32)
    lane_q = lax.broadcasted_iota(jnp.int32, (tq, _LANES), 1)
    ramp_q_parts, rest = [], ramp_q
    for part in range(_RAMP_PARTS):
        ramp_q_parts.append(rest.astype(_BF16).astype(_F32))
        rest = rest - ramp_q_parts[-1]
    for variant, sign in enumerate(_SIGNS):
        ext = jnp.where(lane_q < _RAMP_PARTS, sign, 0.0)
        for part in range(_RAMP_PARTS):
            ext = jnp.where(lane_q == _RAMP_PARTS + part, -sign * ramp_q_parts[part], ext)
        for half in range(2):
            qaug_ref[variant, half] = jnp.concatenate(
                [q_ref[:, half * dh:(half + 1) * dh], ext.astype(_BF16)], axis=1)

    m_ref[...] = jnp.full_like(m_ref, _NEG_BIG)
    l_ref[...] = jnp.zeros_like(l_ref)
    acc_ref[...] = jnp.zeros_like(acc_ref)

    def chunk_of(j):
        rest = jnp.where(j - 1 >= c_own, j, j - 1)
        return jnp.where(j == 0, c_own, rest)

    def group(g):
        return slice(g * _SUBLANES, (g + 1) * _SUBLANES)

    def scores(c, slot, own):
        start = pl.multiple_of(c * tk, tk)
        variant = _OWN if own else jnp.where(c < c_own, 0, 1)
        for half in range(2):
            s_ref[slot, half] = jnp.dot(qaug_ref[variant, half], kaug_ref[half, :, pl.ds(start, tk)],
                                        preferred_element_type=_F32)

    def values(c, slot):
        start = pl.multiple_of(c * tk, tk)
        v = v_ref[pl.ds(start, tk), :]
        for half in range(2):
            alpha = alpha_ref[slot, half]
            acc_ref[half] = (jnp.concatenate([alpha] * (dv // _LANES), axis=1) * acc_ref[half]
                             + jnp.dot(p_ref[slot, half, :, :tk], v, preferred_element_type=_F32))

    def softmax(c, slot, own):
        if own:
            kappa = 0.0
        else:
            shift = slope * (qi * tq - c * tk).astype(_F32)
            kappa = jnp.where(c < c_own, -shift, shift)

        def biased(half, g, j):
            lanes = slice(j * _LANES, (j + 1) * _LANES)
            s = s_ref[slot, half, group(g), lanes]
            return s + bias_ref[group(g), lanes] if own else s

        for half in range(2):
            if own:
                for g in range(n_row):
                    mx = biased(half, g, 0)
                    for j in range(1, n_col):
                        mx = jnp.maximum(mx, biased(half, g, j))
                    mx_ref[group(g)] = mx
                row_max = jnp.max(mx_ref[...], axis=-1, keepdims=True)
            else:
                row_max = jnp.max(s_ref[slot, half], axis=-1, keepdims=True)
            m_old = m_ref[half]
            m_new = jnp.maximum(m_old, row_max + kappa)
            alpha_ref[slot, half] = jnp.exp2(m_old - m_new)
            m_ref[half] = m_new
            sub_ref[...] = m_new - kappa
            for g in range(0, n_row, 2):
                pair_rows = slice(g * _SUBLANES, (g + 2) * _SUBLANES)
                sub = [sub_ref[group(g + i)] for i in range(2)]
                tot = [None, None]
                for j in range(n_col):
                    ps = [jnp.exp2(biased(half, g + i, j) - sub[i]) for i in range(2)]
                    tot = [p if t is None else t + p for t, p in zip(tot, ps)]
                    p_ref[slot, half, pair_rows, j * _LANES:(j + 1) * _LANES] = (
                        jnp.concatenate(ps, axis=0).astype(p_ref.dtype))
                for i in range(2):
                    l_ref[half, group(g + i)] = (alpha_ref[slot, half, group(g + i)] * l_ref[half, group(g + i)]
                                                 + tot[i])

    def phase(score_job=None, softmax_job=None, value_job=None):
        if score_job is not None:
            scores(*score_job)
        if softmax_job is not None:
            softmax(*softmax_job)
        if value_job is not None:
            values(*value_job)

    phase(score_job=(c_own, 0, True))
    phase(score_job=(chunk_of(1), 1, False), softmax_job=(c_own, 0, True))

    def pair(jj, carry):
        j = 2 * jj + 1
        phase((chunk_of(j + 1), 0, False), (chunk_of(j), 1, False), (chunk_of(j - 1), 0))
        phase((chunk_of(j + 2), 1, False), (chunk_of(j + 1), 0, False), (chunk_of(j), 1))
        return carry

    lax.fori_loop(0, (n_chunks - 2) // 2, pair, 0)
    phase(softmax_job=(chunk_of(n_chunks - 1), 1, False), value_job=(chunk_of(n_chunks - 2), 0))
    phase(value_job=(chunk_of(n_chunks - 1), 1))

    lam = (jnp.exp(jnp.sum(lq1_ref[...] * lk1_ref[...], keepdims=True))
           - jnp.exp(jnp.sum(lq2_ref[...] * lk2_ref[...], keepdims=True)) + lambda_init)
    l0 = jnp.sum(l_ref[0], axis=-1, keepdims=True)
    l1 = jnp.sum(l_ref[1], axis=-1, keepdims=True)
    o = acc_ref[0] / l0 - lam * (acc_ref[1] / l1)
    o_ref[...] = (_rms(o, gsub_ref[...]) * (1.0 - lambda_init)).astype(o_ref.dtype)


def _diff_attn(qkv, batch, seq, lq1, lk1, lq2, lk2, g_sub, lambda_init, *, tq=512, tk=512):
    m, d3 = qkv.shape
    d = d3 // 3
    heads = _ATTN_HEADS
    dv = d // heads
    dh = dv // 2
    tq, tk = _tile(seq, tq), _tile(seq, tk)
    assert tk == tq
    nq = seq // tq
    slopes = jnp.exp2(-8.0 * jnp.arange(1, heads + 1, dtype=_F32) / heads) * _LOG2E
    lam_spec = pl.BlockSpec((1, dh), lambda b, h, i: (0, 0))
    stat = pltpu.VMEM((tq, _LANES), _F32)
    return pl.pallas_call(
        functools.partial(_diff_attn_kernel, lambda_init=lambda_init, tk=tk),
        grid=(batch, heads, nq),
        in_specs=[pl.BlockSpec(memory_space=pltpu.SMEM),
                  pl.BlockSpec((tq, dv), lambda b, h, i: (b * nq + i, h)),
                  pl.BlockSpec((seq, dv), lambda b, h, i: (b, heads + h)),
                  pl.BlockSpec((seq, dv), lambda b, h, i: (b, 2 * heads + h)),
                  lam_spec, lam_spec, lam_spec, lam_spec,
                  pl.BlockSpec((1, dv), lambda b, h, i: (0, 0))],
        out_specs=pl.BlockSpec((tq, dv), lambda b, h, i: (b * nq + i, h)),
        out_shape=jax.ShapeDtypeStruct((m, d), _BF16),
        scratch_shapes=[pltpu.VMEM((2, 2 * dh, seq), _BF16),
                        pltpu.VMEM((tq, tk), _F32),
                        pltpu.VMEM((len(_SIGNS), 2, tq, 2 * dh), _BF16),
                        pltpu.VMEM((2, 2, tq, tk), _F32),
                        pltpu.VMEM((2, 2, tq, tk + _LANES), _BF16),
                        pltpu.VMEM((2, 2, tq, _LANES), _F32),
                        stat, stat,
                        pltpu.VMEM((2, tq, _LANES), _F32),
                        pltpu.VMEM((2, tq, _LANES), _F32),
                        pltpu.VMEM((2, tq, dv), _F32)],
        compiler_params=_compiler_params("parallel", "parallel", "arbitrary"),
        name="diff_attn",
    )(slopes, qkv, qkv, qkv, lq1.reshape(1, dh), lk1.reshape(1, dh), lq2.reshape(1, dh),
      lk2.reshape(1, dh), g_sub.reshape(1, dv))


def _ffn(h, wg, wu, wd_f32, next_f32):
    a, wd = _gate_up(h, wg, wu, cast_srcs=(wd_f32,))
    y, *cast = _mm(a, (wd, ()), _F32, tm=1024, tn=512, cast_srcs=next_f32) if next_f32 else [
        _mm(a, (wd, ()), _F32, tm=1024, tn=512)]
    return y, [(w, ()) for w in cast]


def _pool_fourier(h, batch, seq, w_in, w_pool, pool_scale, w_fourier, w_out, dft):
    cos_s, sin_s, chan_dft = dft
    pool_width = pool_scale.shape[0]
    four_width = w_fourier[0].shape[-1]
    u = _mm(h, w_in, _F32)
    a = _pool_proj(_pool(u, seq, pool_width), w_pool, pool_scale)
    xcs = _mm(u, (chan_dft, ()), _BF16, a_col_block=(pool_width // four_width, four_width))
    head_dim = four_width // _FOURIER_HEADS
    f = _seq_dft(cos_s, sin_s, xcs, batch, seq, four_width, 1.0 / math.sqrt(seq * head_dim))
    f = _mm(f, w_fourier, _BF16)
    return _mm_cat(a, f, w_out)


def kernel(x, norm_pre, norm_post, w_ffn_gate, w_ffn_up, w_ffn_down, w_mix_in, w_pool, pool_scale,
           w_fourier, w_mix_out, w_qkv, w_attn_out, lambda_q1, lambda_k1, lambda_q2, lambda_k2,
           subln_gain):
    batch, seq, d = x.shape
    depth = norm_pre.shape[0]
    m = batch * seq
    x = x.reshape(m, d)
    bf = lambda w: w.astype(_BF16)

    four_width = w_fourier.shape[1]
    head_dim = four_width // _FOURIER_HEADS
    cos_s, sin_s = _dft_tables(seq)
    cos_c, sin_c = _dft_tables(head_dim)
    eye = jnp.eye(_FOURIER_HEADS, dtype=_F32)
    chan_dft = jnp.concatenate([jnp.kron(eye, cos_c), jnp.kron(eye, sin_c)], axis=1)
    dft = (bf(cos_s), bf(sin_s), bf(chan_dft))

    attn_head_dim = d // (2 * _ATTN_HEADS)
    w_pl, w_fr = bf(w_pool), bf(w_fourier)
    wg, wu = (bf(w_ffn_gate[0, 0]), ()), (bf(w_ffn_up[0, 0]), ())

    h = _prenorm(x, norm_pre[0, 0])
    for l in range(depth):
        i = l // 2
        mixer = ([(w_mix_in, (i,)), (w_mix_out, (i,))] if l % 2 == 0 else
                 [(w_qkv, (i,)), (w_attn_out, (i,))])
        y, (wg, wu, w_first, w_last) = _ffn(h, wg, wu, (w_ffn_down, (l, 0)),
                                            [(w_ffn_gate, (l, 1)), (w_ffn_up, (l, 1))] + mixer)
        x, h = _residual(y, x, norm_post[l, 0], norm_pre[l, 1], 0.5)
        if l % 2 == 0:
            y = _pool_fourier(h, batch, seq, w_first, (w_pl, (i,)), pool_scale[i],
                              (w_fr, (i,)), w_last, dft)
        else:
            lambda_init = 0.8 - 0.6 * math.exp(-0.3 * l)
            qkv = _mm(h, w_first, _BF16, scaled_cols=d, scale=attn_head_dim ** -0.5 * _LOG2E)
            o = _diff_attn(qkv, batch, seq, lambda_q1[i], lambda_k1[i], lambda_q2[i], lambda_k2[i],
                           subln_gain[i], lambda_init)
            y = _mm(o, w_last, _F32)
        x, h = _residual(y, x, norm_post[l, 1], norm_pre[l, 2], 1.0)
        following = [(w_ffn_gate, (l + 1, 0)), (w_ffn_up, (l + 1, 0))] if l + 1 < depth else []
        y, cast = _ffn(h, wg, wu, (w_ffn_down, (l, 1)), following)
        if cast:
            wg, wu = cast
        g_next = norm_pre[l + 1, 0] if l + 1 < depth else None
        x, h = _residual(y, x, norm_post[l, 2], g_next, 0.5)
    return x.reshape(batch, seq, d)
```

```python
import functools
import math

import jax
import jax.numpy as jnp
from jax import lax
from jax.experimental import pallas as pl
from jax.experimental.pallas import tpu as pltpu

_V7X_VMEM_LIMIT_BYTES = 56 * 1024 * 1024
_SUBLANES = 8
_LANES = 128

_EPS = 1e-6
_POOL_WINDOWS = (2, 4, 8, 16)
_FOURIER_HEADS = 4
_ATTN_HEADS = 16
_NEG_BIG = -1e30

_BF16 = jnp.bfloat16
_F32 = jnp.float32


def _compiler_params(*semantics, flags=None):
    return pltpu.CompilerParams(dimension_semantics=semantics,
                                vmem_limit_bytes=_V7X_VMEM_LIMIT_BYTES, flags=flags)


def _tile(n, want):
    t = min(n, want)
    assert n % t == 0, (n, want)
    return t


def _rms(x, g):
    return x * lax.rsqrt(jnp.mean(x * x, axis=-1, keepdims=True) + _EPS) * g


_NORM_ROWS = 256


def _prenorm_kernel(x_ref, g_ref, h_ref):
    h_ref[...] = _rms(x_ref[...], g_ref[...]).astype(h_ref.dtype)


def _prenorm(x, g):
    m, d = x.shape
    tm = _tile(m, _NORM_ROWS)
    return pl.pallas_call(
        _prenorm_kernel,
        grid=(m // tm,),
        in_specs=[pl.BlockSpec((tm, d), lambda i: (i, 0)),
                  pl.BlockSpec((1, d), lambda i: (0, 0))],
        out_specs=pl.BlockSpec((tm, d), lambda i: (i, 0)),
        out_shape=jax.ShapeDtypeStruct((m, d), _BF16),
        compiler_params=_compiler_params("parallel"),
        name="prenorm",
    )(x, g.reshape(1, d))


def _residual_kernel(y_ref, x_ref, gpost_ref, gpre_ref, xo_ref, h_ref, *, step):
    xn = x_ref[...] + step * _rms(y_ref[...], gpost_ref[...])
    xo_ref[...] = xn
    h_ref[...] = _rms(xn, gpre_ref[...]).astype(h_ref.dtype)


def _residual_last_kernel(y_ref, x_ref, gpost_ref, xo_ref, *, step):
    xo_ref[...] = x_ref[...] + step * _rms(y_ref[...], gpost_ref[...])


def _residual(y, x, g_post, g_pre_next, step):
    m, d = x.shape
    tm = _tile(m, _NORM_ROWS)
    row = pl.BlockSpec((tm, d), lambda i: (i, 0))
    vec = pl.BlockSpec((1, d), lambda i: (0, 0))
    if g_pre_next is None:
        return pl.pallas_call(
            functools.partial(_residual_last_kernel, step=step),
            grid=(m // tm,),
            in_specs=[row, row, vec],
            out_specs=row,
            out_shape=jax.ShapeDtypeStruct((m, d), _F32),
            compiler_params=_compiler_params("parallel"),
            name="residual_last",
        )(y, x, g_post.reshape(1, d)), None
    return pl.pallas_call(
        functools.partial(_residual_kernel, step=step),
        grid=(m // tm,),
        in_specs=[row, row, vec, vec],
        out_specs=[row, row],
        out_shape=[jax.ShapeDtypeStruct((m, d), _F32), jax.ShapeDtypeStruct((m, d), _BF16)],
        compiler_params=_compiler_params("parallel"),
        name="residual",
    )(y, x, g_post.reshape(1, d), g_pre_next.reshape(1, d))


def _weight_spec(w, block, index):
    lead = w[1]
    return pl.BlockSpec((None,) * len(lead) + block, lambda *g: lead + index(*g))


def _cast_specs(cast_srcs, steps, step_of):
    ins, outs, shapes, arrays = [], [], [], []
    for arr, lead in cast_srcs:
        rows, cols = arr.shape[-2:]
        slab = rows // steps
        assert rows == slab * steps and slab % (2 * _SUBLANES) == 0, (rows, steps)
        ins.append(pl.BlockSpec((None,) * len(lead) + (slab, cols),
                                lambda *g, lead=lead: lead + (step_of(*g), 0)))
        outs.append(pl.BlockSpec((slab, cols), lambda *g: (step_of(*g), 0)))
        shapes.append(jax.ShapeDtypeStruct((rows, cols), _BF16))
        arrays.append(arr)
    return ins, outs, shapes, arrays


def _cast_slabs(refs):
    n = len(refs) // 2
    for src_ref, dst_ref in zip(refs[:n], refs[n:]):
        dst_ref[...] = src_ref[...].astype(dst_ref.dtype)


def _mm_kernel(a_ref, w_ref, *refs, n_cast, scaled_col_blocks, scale):
    o_ref = refs[n_cast]
    acc = jnp.dot(a_ref[...].astype(_BF16), w_ref[...], preferred_element_type=_F32)
    if scaled_col_blocks:
        acc = acc * jnp.where(pl.program_id(1) < scaled_col_blocks, scale, 1.0)
    o_ref[...] = acc.astype(o_ref.dtype)
    _cast_slabs(refs[:n_cast] + refs[n_cast + 1:])


def _mm(a, w, out_dtype, *, tm=1024, tn=1024, a_col_block=None, scaled_cols=0, scale=1.0, cast_srcs=()):
    m = a.shape[0]
    k, n = w[0].shape[-2:]
    tm, tn = _tile(m, tm), _tile(n, tn)
    a_col = 0
    if a_col_block is not None:
        a_col, width = a_col_block
        assert width == k
    else:
        assert a.shape[1] == k
    assert scaled_cols % tn == 0
    nj = n // tn
    c_in, c_out, c_shape, c_arr = _cast_specs(cast_srcs, (m // tm) * nj, lambda i, j: i * nj + j)
    out = pl.pallas_call(
        functools.partial(_mm_kernel, n_cast=len(c_arr), scaled_col_blocks=scaled_cols // tn, scale=scale),
        grid=(m // tm, nj),
        in_specs=[pl.BlockSpec((tm, k), lambda i, j: (i, a_col)),
                  _weight_spec(w, (k, tn), lambda i, j: (0, j))] + c_in,
        out_specs=[pl.BlockSpec((tm, tn), lambda i, j: (i, j))] + c_out,
        out_shape=[jax.ShapeDtypeStruct((m, n), out_dtype)] + c_shape,
        compiler_params=_compiler_params("parallel", "arbitrary"),
        name="matmul",
    )(a, w[0], *c_arr)
    return out if c_arr else out[0]


def _gate_up_kernel(h_ref, wg_ref, wu_ref, *refs, n_cast):
    o_ref = refs[n_cast]
    h = h_ref[...]
    g = jnp.dot(h, wg_ref[...], preferred_element_type=_F32)
    u = jnp.dot(h, wu_ref[...], preferred_element_type=_F32)
    o_ref[...] = (g * jax.nn.sigmoid(g) * u).astype(o_ref.dtype)
    _cast_slabs(refs[:n_cast] + refs[n_cast + 1:])


def _gate_up(h, wg, wu, *, tm=1024, tn=768, cast_srcs=()):
    m, k = h.shape
    n = wg[0].shape[-1]
    tm, tn = _tile(m, tm), _tile(n, tn)
    nj = n // tn
    c_in, c_out, c_shape, c_arr = _cast_specs(cast_srcs, (m // tm) * nj, lambda i, j: i * nj + j)
    out = pl.pallas_call(
        functools.partial(_gate_up_kernel, n_cast=len(c_arr)),
        grid=(m // tm, nj),
        in_specs=[pl.BlockSpec((tm, k), lambda i, j: (i, 0)),
                  _weight_spec(wg, (k, tn), lambda i, j: (0, j)),
                  _weight_spec(wu, (k, tn), lambda i, j: (0, j))] + c_in,
        out_specs=[pl.BlockSpec((tm, tn), lambda i, j: (i, j))] + c_out,
        out_shape=[jax.ShapeDtypeStruct((m, n), _BF16)] + c_shape,
        compiler_params=_compiler_params("parallel", "arbitrary"),
        name="gate_up",
    )(h, wg[0], wu[0], *c_arr)
    return out if c_arr else out[0]


def _mm_cat_kernel(a_ref, f_ref, w_ref, o_ref):
    ka = a_ref.shape[1]
    acc = jnp.dot(a_ref[...], w_ref[:ka, :], preferred_element_type=_F32)
    acc = acc + jnp.dot(f_ref[...], w_ref[ka:, :], preferred_element_type=_F32)
    o_ref[...] = acc.astype(o_ref.dtype)


def _mm_cat(a, f, w, *, tm=1024, tn=1024):
    m, ka = a.shape
    kf = f.shape[1]
    k, n = w[0].shape[-2:]
    assert ka + kf == k
    tm, tn = _tile(m, tm), _tile(n, tn)
    return pl.pallas_call(
        _mm_cat_kernel,
        grid=(m // tm, n // tn),
        in_specs=[pl.BlockSpec((tm, ka), lambda i, j: (i, 0)),
                  pl.BlockSpec((tm, kf), lambda i, j: (i, 0)),
                  _weight_spec(w, (k, tn), lambda i, j: (0, j))],
        out_specs=pl.BlockSpec((tm, tn), lambda i, j: (i, j)),
        out_shape=jax.ShapeDtypeStruct((m, n), _F32),
        compiler_params=_compiler_params("parallel", "arbitrary"),
        name="matmul_cat",
    )(a, f, w[0])


_POOL_HALO = 8
_POOL_ROWS = 256


def _pool_kernel(prev_ref, x_ref, next_ref, o_ref, *, seq, group):
    ts = x_ref.shape[0]
    ext_rows = ts + 2 * _POOL_HALO
    tile = pl.program_id(0) % (seq // ts)
    t = tile * ts + lax.broadcasted_iota(jnp.int32, (ts, 1), 0)
    first, last = tile == 0, tile == seq // ts - 1
    for g, w in enumerate(_POOL_WINDOWS):
        cols = slice(g * group, (g + 1) * group)
        x = x_ref[:, cols]
        run = jnp.concatenate([jnp.where(first, 0.0, prev_ref[:, cols]), x,
                               jnp.where(last, 0.0, next_ref[:, cols])], axis=0)
        span = 1
        while span < w:
            run = run + pltpu.roll(run, ext_rows - span, 0)
            span *= 2
        total = pltpu.roll(run, w // 2, 0)[_POOL_HALO:_POOL_HALO + ts]
        lo = jnp.clip(t - w // 2, 0, seq)
        hi = jnp.clip(t - w // 2 + w, 0, seq)
        o_ref[:, cols] = (total / (hi - lo).astype(_F32) - x).astype(o_ref.dtype)


def _pool(u, seq, pool_width):
    m = u.shape[0]
    group = pool_width // len(_POOL_WINDOWS)
    assert max(_POOL_WINDOWS) // 2 <= _POOL_HALO
    ts = _tile(seq, _POOL_ROWS)
    per = ts // _POOL_HALO
    last = m // _POOL_HALO - 1
    return pl.pallas_call(
        functools.partial(_pool_kernel, seq=seq, group=group),
        grid=(m // ts,),
        in_specs=[pl.BlockSpec((_POOL_HALO, pool_width), lambda i: (jnp.maximum(i * per - 1, 0), 0)),
                  pl.BlockSpec((ts, pool_width), lambda i: (i, 0)),
                  pl.BlockSpec((_POOL_HALO, pool_width), lambda i: (jnp.minimum((i + 1) * per, last), 0))],
        out_specs=pl.BlockSpec((ts, pool_width), lambda i: (i, 0)),
        out_shape=jax.ShapeDtypeStruct((m, pool_width), _BF16),
        compiler_params=_compiler_params("parallel"),
        name="pool",
    )(u, u, u)


def _pool_proj_kernel(p_ref, w_ref, s_ref, o_ref):
    acc = jnp.dot(p_ref[...], w_ref[...], preferred_element_type=_F32)
    o_ref[...] = (acc * s_ref[...]).astype(o_ref.dtype)


def _pool_proj(pooled, w_pool, pool_scale, *, tm=1024):
    m, width = pooled.shape
    groups, group, _ = w_pool[0].shape[-3:]
    tm = _tile(m, tm)
    return pl.pallas_call(
        _pool_proj_kernel,
        grid=(m // tm, groups),
        in_specs=[pl.BlockSpec((tm, group), lambda i, g: (i, g)),
                  _weight_spec(w_pool, (None, group, group), lambda i, g: (g, 0, 0)),
                  pl.BlockSpec((1, group), lambda i, g: (0, g))],
        out_specs=pl.BlockSpec((tm, group), lambda i, g: (i, g)),
        out_shape=jax.ShapeDtypeStruct((m, width), _BF16),
        compiler_params=_compiler_params("parallel", "arbitrary"),
        name="pool_proj",
    )(pooled, w_pool[0], pool_scale.reshape(1, width))


_DFT_LO_ROWS = 64


def _dft_rows(rows, n):
    idx = (rows[:, None] * jnp.arange(n, dtype=jnp.int32)[None, :]) % n
    ang = idx.astype(_F32) * (2.0 * math.pi / n)
    return jnp.cos(ang), jnp.sin(ang)


def _dft_tables(n):
    lo_rows = _DFT_LO_ROWS if n % _DFT_LO_ROWS == 0 and n > _DFT_LO_ROWS else 1
    c_hi, s_hi = _dft_rows(jnp.arange(n // lo_rows, dtype=jnp.int32) * lo_rows, n)
    c_lo, s_lo = _dft_rows(jnp.arange(lo_rows, dtype=jnp.int32), n)
    cos = c_hi[:, None, :] * c_lo[None, :, :] - s_hi[:, None, :] * s_lo[None, :, :]
    sin = s_hi[:, None, :] * c_lo[None, :, :] + c_hi[:, None, :] * s_lo[None, :, :]
    return cos.reshape(n, n), sin.reshape(n, n)


def _seq_dft_kernel(c_ref, s_ref, xc_ref, xs_ref, o_ref, acc_ref, *, norm):
    k = pl.program_id(3)

    @pl.when(k == 0)
    def _():
        acc_ref[...] = jnp.zeros_like(acc_ref)

    acc_ref[...] += (jnp.dot(c_ref[...], xc_ref[...], preferred_element_type=_F32)
                     - jnp.dot(s_ref[...], xs_ref[...], preferred_element_type=_F32))

    @pl.when(k == pl.num_programs(3) - 1)
    def _():
        o_ref[...] = (acc_ref[...] * norm).astype(o_ref.dtype)


def _seq_dft(cos_s, sin_s, xcs, batch, seq, width, norm, *, tm=1024, tn=1024, tk=1024):
    tm, tn, tk = _tile(seq, tm), _tile(width, tn), _tile(seq, tk)
    nj = width // tn
    rows_k = seq // tk
    rows_m = seq // tm
    return pl.pallas_call(
        functools.partial(_seq_dft_kernel, norm=norm),
        grid=(batch, rows_m, nj, rows_k),
        in_specs=[pl.BlockSpec((tm, tk), lambda b, i, j, k: (i, k)),
                  pl.BlockSpec((tm, tk), lambda b, i, j, k: (i, k)),
                  pl.BlockSpec((tk, tn), lambda b, i, j, k: (b * rows_k + k, j)),
                  pl.BlockSpec((tk, tn), lambda b, i, j, k: (b * rows_k + k, nj + j))],
        out_specs=pl.BlockSpec((tm, tn), lambda b, i, j, k: (b * rows_m + i, j)),
        out_shape=jax.ShapeDtypeStruct((batch * seq, width), _BF16),
        scratch_shapes=[pltpu.VMEM((tm, tn), _F32)],
        compiler_params=_compiler_params("parallel", "parallel", "parallel", "arbitrary"),
        name="seq_dft",
    )(cos_s, sin_s, xcs, xcs)


_LOG2E = math.log2(math.e)
_RAMP_PARTS = 3
_SIGNS = (1.0, -1.0, 0.0)
_OWN = 2
_WIDEN_KEYS = 256
_BIAS_ROWS = 64


def _diff_attn_kernel(slope_ref, q_ref, k_ref, v_ref, lq1_ref, lk1_ref, lq2_ref, lk2_ref, gsub_ref,
                      o_ref, kaug_ref, bias_ref, qaug_ref, s_ref, p_ref, alpha_ref, sub_ref,
                      mx_ref, m_ref, l_ref, acc_ref, *, lambda_init, tk):
    h = pl.program_id(1)
    qi = pl.program_id(2)
    tq, dv = q_ref.shape
    dh = dv // 2
    seq = k_ref.shape[0]
    n_chunks = seq // tk
    n_col = tk // _LANES
    n_row = tq // _SUBLANES
    assert n_chunks % 2 == 0 and tq == tk
    slope = slope_ref[h]
    c_own = (qi * tq) // tk

    @pl.when(qi == 0)
    def _():
        kb = min(tk, _WIDEN_KEYS)

        def widen(i, carry):
            start = pl.multiple_of(i * kb, kb)
            local = (i % (tk // kb)) * kb + lax.broadcasted_iota(jnp.int32, (dh, kb), 1)
            rest = slope * local.astype(_F32)
            row = lax.broadcasted_iota(jnp.int32, (dh, kb), 0)
            ext = jnp.zeros((dh, kb), _F32)
            for part in range(_RAMP_PARTS):
                term = rest.astype(_BF16).astype(_F32)
                ext = jnp.where(row == part, term, ext)
                rest = rest - term
            ext = jnp.where((row >= _RAMP_PARTS) & (row < 2 * _RAMP_PARTS), 1.0, ext)
            for half in range(2):
                k_half = k_ref[pl.ds(start, kb), half * dh:(half + 1) * dh]
                kaug_ref[half, :dh, pl.ds(start, kb)] = k_half.T
                kaug_ref[half, dh:, pl.ds(start, kb)] = ext.astype(_BF16)
            return carry

        lax.fori_loop(0, seq // kb, widen, 0)

        for r in range(0, tq, _BIAS_ROWS):
            dist = (lax.broadcasted_iota(jnp.int32, (_BIAS_ROWS, tk), 0) + r
                    - lax.broadcasted_iota(jnp.int32, (_BIAS_ROWS, tk), 1))
            bias_ref[r:r + _BIAS_ROWS] = -slope * jnp.abs(dist).astype(_F32)

    ramp_q = slope * lax.broadcasted_iota(jnp.int32, (tq, _LANES), 0).astype(_F32)
    lane_q = lax.broadcasted_iota(jnp.int32, (tq, _LANES), 1)
    ramp_q_parts, rest = [], ramp_q
    for part in range(_RAMP_PARTS):
        ramp_q_parts.append(rest.astype(_BF16).astype(_F32))
        rest = rest - ramp_q_parts[-1]
    for variant, sign in enumerate(_SIGNS):
        ext = jnp.where(lane_q < _RAMP_PARTS, sign, 0.0)
        for part in range(_RAMP_PARTS):
            ext = jnp.where(lane_q == _RAMP_PARTS + part, -sign * ramp_q_parts[part], ext)
        for half in range(2):
            qaug_ref[variant, half] = jnp.concatenate(
                [q_ref[:, half * dh:(half + 1) * dh], ext.astype(_BF16)], axis=1)

    m_ref[...] = jnp.full_like(m_ref, _NEG_BIG)
    l_ref[...] = jnp.zeros_like(l_ref)
    acc_ref[...] = jnp.zeros_like(acc_ref)

    def chunk_of(j):
        rest = jnp.where(j - 1 >= c_own, j, j - 1)
        return jnp.where(j == 0, c_own, rest)

    def group(g):
        return slice(g * _SUBLANES, (g + 1) * _SUBLANES)

    def scores(c, slot, own):
        start = pl.multiple_of(c * tk, tk)
        variant = _OWN if own else jnp.where(c < c_own, 0, 1)
        for half in range(2):
            s_ref[slot, half] = jnp.dot(qaug_ref[variant, half], kaug_ref[half, :, pl.ds(start, tk)],
                                        preferred_element_type=_F32)

    def values(c, slot):
        start = pl.multiple_of(c * tk, tk)
        v = v_ref[pl.ds(start, tk), :]
        for half in range(2):
            alpha = alpha_ref[slot, half]
            acc_ref[half] = (jnp.concatenate([alpha] * (dv // _LANES), axis=1) * acc_ref[half]
                             + jnp.dot(p_ref[slot, half], v, preferred_element_type=_F32))

    def softmax(c, slot, own):
        if own:
            kappa = 0.0
        else:
            shift = slope * (qi * tq - c * tk).astype(_F32)
            kappa = jnp.where(c < c_own, -shift, shift)

        def biased(half, g, j):
            lanes = slice(j * _LANES, (j + 1) * _LANES)
            s = s_ref[slot, half, group(g), lanes]
            return s + bias_ref[group(g), lanes] if own else s

        for half in range(2):
            if own:
                for g in range(n_row):
                    mx = biased(half, g, 0)
                    for j in range(1, n_col):
                        mx = jnp.maximum(mx, biased(half, g, j))
                    mx_ref[group(g)] = mx
                row_max = jnp.max(mx_ref[...], axis=-1, keepdims=True)
            else:
                row_max = jnp.max(s_ref[slot, half], axis=-1, keepdims=True)
            m_old = m_ref[half]
            m_new = jnp.maximum(m_old, row_max + kappa)
            alpha_ref[slot, half] = jnp.exp2(m_old - m_new)
            m_ref[half] = m_new
            sub_ref[...] = m_new - kappa
            for g in range(0, n_row, 2):
                pair_rows = slice(g * _SUBLANES, (g + 2) * _SUBLANES)
                sub = [sub_ref[group(g + i)] for i in range(2)]
                tot = [None, None]
                for j in range(n_col):
                    ps = [jnp.exp2(biased(half, g + i, j) - sub[i]) for i in range(2)]
                    tot = [p if t is None else t + p for t, p in zip(tot, ps)]
                    p_ref[slot, half, pair_rows, j * _LANES:(j + 1) * _LANES] = (
                        jnp.concatenate(ps, axis=0).astype(p_ref.dtype))
                for i in range(2):
                    l_ref[half, group(g + i)] = (alpha_ref[slot, half, group(g + i)] * l_ref[half, group(g + i)]
                                                 + tot[i])

    def phase(score_job=None, softmax_job=None, value_job=None):
        if score_job is not None:
            scores(*score_job)
        if softmax_job is not None:
            softmax(*softmax_job)
        if value_job is not None:
            values(*value_job)

    phase(score_job=(c_own, 0, True))
    phase(score_job=(chunk_of(1), 1, False), softmax_job=(c_own, 0, True))

    def pair(jj, carry):
        j = 2 * jj + 1
        phase((chunk_of(j + 1), 0, False), (chunk_of(j), 1, False), (chunk_of(j - 1), 0))
        phase((chunk_of(j + 2), 1, False), (chunk_of(j + 1), 0, False), (chunk_of(j), 1))
        return carry

    lax.fori_loop(0, (n_chunks - 2) // 2, pair, 0)
    phase(softmax_job=(chunk_of(n_chunks - 1), 1, False), value_job=(chunk_of(n_chunks - 2), 0))
    phase(value_job=(chunk_of(n_chunks - 1), 1))

    lam = (jnp.exp(jnp.sum(lq1_ref[...] * lk1_ref[...], keepdims=True))
           - jnp.exp(jnp.sum(lq2_ref[...] * lk2_ref[...], keepdims=True)) + lambda_init)
    l0 = jnp.sum(l_ref[0], axis=-1, keepdims=True)
    l1 = jnp.sum(l_ref[1], axis=-1, keepdims=True)
    o = acc_ref[0] / l0 - lam * (acc_ref[1] / l1)
    o_ref[...] = (_rms(o, gsub_ref[...]) * (1.0 - lambda_init)).astype(o_ref.dtype)


def _diff_attn(qkv, batch, seq, lq1, lk1, lq2, lk2, g_sub, lambda_init, *, tq=512, tk=512):
    m, d3 = qkv.shape
    d = d3 // 3
    heads = _ATTN_HEADS
    dv = d // heads
    dh = dv // 2
    tq, tk = _tile(seq, tq), _tile(seq, tk)
    assert tk == tq
    nq = seq // tq
    slopes = jnp.exp2(-8.0 * jnp.arange(1, heads + 1, dtype=_F32) / heads) * _LOG2E
    lam_spec = pl.BlockSpec((1, dh), lambda b, h, i: (0, 0))
    stat = pltpu.VMEM((tq, _LANES), _F32)
    return pl.pallas_call(
        functools.partial(_diff_attn_kernel, lambda_init=lambda_init, tk=tk),
        grid=(batch, heads, nq),
        in_specs=[pl.BlockSpec(memory_space=pltpu.SMEM),
                  pl.BlockSpec((tq, dv), lambda b, h, i: (b * nq + i, h)),
                  pl.BlockSpec((seq, dv), lambda b, h, i: (b, heads + h)),
                  pl.BlockSpec((seq, dv), lambda b, h, i: (b, 2 * heads + h)),
                  lam_spec, lam_spec, lam_spec, lam_spec,
                  pl.BlockSpec((1, dv), lambda b, h, i: (0, 0))],
        out_specs=pl.BlockSpec((tq, dv), lambda b, h, i: (b * nq + i, h)),
        out_shape=jax.ShapeDtypeStruct((m, d), _BF16),
        scratch_shapes=[pltpu.VMEM((2, 2 * dh, seq), _BF16),
                        pltpu.VMEM((tq, tk), _F32),
                        pltpu.VMEM((len(_SIGNS), 2, tq, 2 * dh), _BF16),
                        pltpu.VMEM((2, 2, tq, tk), _F32),
                        pltpu.VMEM((2, 2, tq, tk), _BF16),
                        pltpu.VMEM((2, 2, tq, _LANES), _F32),
                        stat, stat,
                        pltpu.VMEM((2, tq, _LANES), _F32),
                        pltpu.VMEM((2, tq, _LANES), _F32),
                        pltpu.VMEM((2, tq, dv), _F32)],
        compiler_params=_compiler_params("parallel", "parallel", "arbitrary"),
        name="diff_attn",
    )(slopes, qkv, qkv, qkv, lq1.reshape(1, dh), lk1.reshape(1, dh), lq2.reshape(1, dh),
      lk2.reshape(1, dh), g_sub.reshape(1, dv))


def _ffn(h, wg, wu, wd_f32, next_f32):
    a, wd = _gate_up(h, wg, wu, cast_srcs=(wd_f32,))
    y, *cast = _mm(a, (wd, ()), _F32, tm=1024, tn=512, cast_srcs=next_f32) if next_f32 else [
        _mm(a, (wd, ()), _F32, tm=1024, tn=512)]
    return y, [(w, ()) for w in cast]


def _pool_fourier(h, batch, seq, w_in, w_pool, pool_scale, w_fourier, w_out, dft):
    cos_s, sin_s, chan_dft = dft
    pool_width = pool_scale.shape[0]
    four_width = w_fourier[0].shape[-1]
    u = _mm(h, w_in, _F32)
    a = _pool_proj(_pool(u, seq, pool_width), w_pool, pool_scale)
    xcs = _mm(u, (chan_dft, ()), _BF16, a_col_block=(pool_width // four_width, four_width))
    head_dim = four_width // _FOURIER_HEADS
    f = _seq_dft(cos_s, sin_s, xcs, batch, seq, four_width, 1.0 / math.sqrt(seq * head_dim))
    f = _mm(f, w_fourier, _BF16)
    return _mm_cat(a, f, w_out)


def kernel(x, norm_pre, norm_post, w_ffn_gate, w_ffn_up, w_ffn_down, w_mix_in, w_pool, pool_scale,
           w_fourier, w_mix_out, w_qkv, w_attn_out, lambda_q1, lambda_k1, lambda_q2, lambda_k2,
           subln_gain):
    batch, seq, d = x.shape
    depth = norm_pre.shape[0]
    m = batch * seq
    x = x.reshape(m, d)
    bf = lambda w: w.astype(_BF16)

    four_width = w_fourier.shape[1]
    head_dim = four_width // _FOURIER_HEADS
    cos_s, sin_s = _dft_tables(seq)
    cos_c, sin_c = _dft_tables(head_dim)
    eye = jnp.eye(_FOURIER_HEADS, dtype=_F32)
    chan_dft = jnp.concatenate([jnp.kron(eye, cos_c), jnp.kron(eye, sin_c)], axis=1)
    dft = (bf(cos_s), bf(sin_s), bf(chan_dft))

    attn_head_dim = d // (2 * _ATTN_HEADS)
    w_pl, w_fr = bf(w_pool), bf(w_fourier)
    wg, wu = (bf(w_ffn_gate[0, 0]), ()), (bf(w_ffn_up[0, 0]), ())

    h = _prenorm(x, norm_pre[0, 0])
    for l in range(depth):
        i = l // 2
        mixer = ([(w_mix_in, (i,)), (w_mix_out, (i,))] if l % 2 == 0 else
                 [(w_qkv, (i,)), (w_attn_out, (i,))])
        y, (wg, wu, w_first, w_last) = _ffn(h, wg, wu, (w_ffn_down, (l, 0)),
                                            [(w_ffn_gate, (l, 1)), (w_ffn_up, (l, 1))] + mixer)
        x, h = _residual(y, x, norm_post[l, 0], norm_pre[l, 1], 0.5)
        if l % 2 == 0:
            y = _pool_fourier(h, batch, seq, w_first, (w_pl, (i,)), pool_scale[i],
                              (w_fr, (i,)), w_last, dft)
        else:
            lambda_init = 0.8 - 0.6 * math.exp(-0.3 * l)
            qkv = _mm(h, w_first, _BF16, scaled_cols=d, scale=attn_head_dim ** -0.5 * _LOG2E)
            o = _diff_attn(qkv, batch, seq, lambda_q1[i], lambda_k1[i], lambda_q2[i], lambda_k2[i],
                           subln_gain[i], lambda_init)
            y = _mm(o, w_last, _F32)
        x, h = _residual(y, x, norm_post[l, 1], norm_pre[l, 2], 1.0)
        following = [(w_ffn_gate, (l + 1, 0)), (w_ffn_up, (l + 1, 0))] if l + 1 < depth else []
        y, cast = _ffn(h, wg, wu, (w_ffn_down, (l, 1)), following)
        if cast:
            wg, wu = cast
        g_next = norm_pre[l + 1, 0] if l + 1 < depth else None
        x, h = _residual(y, x, norm_post[l, 2], g_next, 0.5)
    return x.reshape(batch, seq, d)
```

```python
import functools
import math

import jax
import jax.numpy as jnp
from jax import lax
from jax.experimental import pallas as pl
from jax.experimental.pallas import tpu as pltpu

_V7X_VMEM_LIMIT_BYTES = 56 * 1024 * 1024
_SUBLANES = 8
_LANES = 128

_EPS = 1e-6
_POOL_WINDOWS = (2, 4, 8, 16)
_FOURIER_HEADS = 4
_ATTN_HEADS = 16
_NEG_BIG = -1e30

_BF16 = jnp.bfloat16
_F32 = jnp.float32


def _compiler_params(*semantics, flags=None):
    return pltpu.CompilerParams(dimension_semantics=semantics,
                                vmem_limit_bytes=_V7X_VMEM_LIMIT_BYTES, flags=flags)


def _tile(n, want):
    t = min(n, want)
    assert n % t == 0, (n, want)
    return t


def _rms(x, g):
    return x * lax.rsqrt(jnp.mean(x * x, axis=-1, keepdims=True) + _EPS) * g


_NORM_ROWS = 256


def _prenorm_kernel(x_ref, g_ref, h_ref):
    h_ref[...] = _rms(x_ref[...], g_ref[...]).astype(h_ref.dtype)


def _prenorm(x, g):
    m, d = x.shape
    tm = _tile(m, _NORM_ROWS)
    return pl.pallas_call(
        _prenorm_kernel,
        grid=(m // tm,),
        in_specs=[pl.BlockSpec((tm, d), lambda i: (i, 0)),
                  pl.BlockSpec((1, d), lambda i: (0, 0))],
        out_specs=pl.BlockSpec((tm, d), lambda i: (i, 0)),
        out_shape=jax.ShapeDtypeStruct((m, d), _BF16),
        compiler_params=_compiler_params("parallel"),
        name="prenorm",
    )(x, g.reshape(1, d))


def _residual_kernel(y_ref, x_ref, gpost_ref, gpre_ref, xo_ref, h_ref, *, step):
    xn = x_ref[...] + step * _rms(y_ref[...], gpost_ref[...])
    xo_ref[...] = xn
    h_ref[...] = _rms(xn, gpre_ref[...]).astype(h_ref.dtype)


def _residual_last_kernel(y_ref, x_ref, gpost_ref, xo_ref, *, step):
    xo_ref[...] = x_ref[...] + step * _rms(y_ref[...], gpost_ref[...])


def _residual(y, x, g_post, g_pre_next, step):
    m, d = x.shape
    tm = _tile(m, _NORM_ROWS)
    row = pl.BlockSpec((tm, d), lambda i: (i, 0))
    vec = pl.BlockSpec((1, d), lambda i: (0, 0))
    if g_pre_next is None:
        return pl.pallas_call(
            functools.partial(_residual_last_kernel, step=step),
            grid=(m // tm,),
            in_specs=[row, row, vec],
            out_specs=row,
            out_shape=jax.ShapeDtypeStruct((m, d), _F32),
            compiler_params=_compiler_params("parallel"),
            name="residual_last",
        )(y, x, g_post.reshape(1, d)), None
    return pl.pallas_call(
        functools.partial(_residual_kernel, step=step),
        grid=(m // tm,),
        in_specs=[row, row, vec, vec],
        out_specs=[row, row],
        out_shape=[jax.ShapeDtypeStruct((m, d), _F32), jax.ShapeDtypeStruct((m, d), _BF16)],
        compiler_params=_compiler_params("parallel"),
        name="residual",
    )(y, x, g_post.reshape(1, d), g_pre_next.reshape(1, d))


def _weight_spec(w, block, index):
    lead = w[1]
    return pl.BlockSpec((None,) * len(lead) + block, lambda *g: lead + index(*g))


def _cast_specs(cast_srcs, steps, step_of):
    ins, outs, shapes, arrays = [], [], [], []
    for arr, lead in cast_srcs:
        rows, cols = arr.shape[-2:]
        slab = rows // steps
        assert rows == slab * steps and slab % (2 * _SUBLANES) == 0, (rows, steps)
        ins.append(pl.BlockSpec((None,) * len(lead) + (slab, cols),
                                lambda *g, lead=lead: lead + (step_of(*g), 0)))
        outs.append(pl.BlockSpec((slab, cols), lambda *g: (step_of(*g), 0)))
        shapes.append(jax.ShapeDtypeStruct((rows, cols), _BF16))
        arrays.append(arr)
    return ins, outs, shapes, arrays


def _cast_slabs(refs):
    n = len(refs) // 2
    for src_ref, dst_ref in zip(refs[:n], refs[n:]):
        dst_ref[...] = src_ref[...].astype(dst_ref.dtype)


def _mm_kernel(a_ref, w_ref, *refs, n_cast, scaled_col_blocks, scale):
    o_ref = refs[n_cast]
    acc = jnp.dot(a_ref[...].astype(_BF16), w_ref[...], preferred_element_type=_F32)
    if scaled_col_blocks:
        acc = acc * jnp.where(pl.program_id(1) < scaled_col_blocks, scale, 1.0)
    o_ref[...] = acc.astype(o_ref.dtype)
    _cast_slabs(refs[:n_cast] + refs[n_cast + 1:])


def _mm(a, w, out_dtype, *, tm=1024, tn=1024, a_col_block=None, scaled_cols=0, scale=1.0, cast_srcs=()):
    m = a.shape[0]
    k, n = w[0].shape[-2:]
    tm, tn = _tile(m, tm), _tile(n, tn)
    a_col = 0
    if a_col_block is not None:
        a_col, width = a_col_block
        assert width == k
    else:
        assert a.shape[1] == k
    assert scaled_cols % tn == 0
    nj = n // tn
    c_in, c_out, c_shape, c_arr = _cast_specs(cast_srcs, (m // tm) * nj, lambda i, j: i * nj + j)
    out = pl.pallas_call(
        functools.partial(_mm_kernel, n_cast=len(c_arr), scaled_col_blocks=scaled_cols // tn, scale=scale),
        grid=(m // tm, nj),
        in_specs=[pl.BlockSpec((tm, k), lambda i, j: (i, a_col)),
                  _weight_spec(w, (k, tn), lambda i, j: (0, j))] + c_in,
        out_specs=[pl.BlockSpec((tm, tn), lambda i, j: (i, j))] + c_out,
        out_shape=[jax.ShapeDtypeStruct((m, n), out_dtype)] + c_shape,
        compiler_params=_compiler_params("parallel", "arbitrary"),
        name="matmul",
    )(a, w[0], *c_arr)
    return out if c_arr else out[0]


def _gate_up_kernel(h_ref, wg_ref, wu_ref, *refs, n_cast):
    o_ref = refs[n_cast]
    h = h_ref[...]
    g = jnp.dot(h, wg_ref[...], preferred_element_type=_F32)
    u = jnp.dot(h, wu_ref[...], preferred_element_type=_F32)
    o_ref[...] = (g * jax.nn.sigmoid(g) * u).astype(o_ref.dtype)
    _cast_slabs(refs[:n_cast] + refs[n_cast + 1:])


def _gate_up(h, wg, wu, *, tm=1024, tn=768, cast_srcs=()):
    m, k = h.shape
    n = wg[0].shape[-1]
    tm, tn = _tile(m, tm), _tile(n, tn)
    nj = n // tn
    c_in, c_out, c_shape, c_arr = _cast_specs(cast_srcs, (m // tm) * nj, lambda i, j: i * nj + j)
    out = pl.pallas_call(
        functools.partial(_gate_up_kernel, n_cast=len(c_arr)),
        grid=(m // tm, nj),
        in_specs=[pl.BlockSpec((tm, k), lambda i, j: (i, 0)),
                  _weight_spec(wg, (k, tn), lambda i, j: (0, j)),
                  _weight_spec(wu, (k, tn), lambda i, j: (0, j))] + c_in,
        out_specs=[pl.BlockSpec((tm, tn), lambda i, j: (i, j))] + c_out,
        out_shape=[jax.ShapeDtypeStruct((m, n), _BF16)] + c_shape,
        compiler_params=_compiler_params("parallel", "arbitrary"),
        name="gate_up",
    )(h, wg[0], wu[0], *c_arr)
    return out if c_arr else out[0]


def _mm_cat_kernel(a_ref, f_ref, w_ref, o_ref):
    ka = a_ref.shape[1]
    acc = jnp.dot(a_ref[...], w_ref[:ka, :], preferred_element_type=_F32)
    acc = acc + jnp.dot(f_ref[...], w_ref[ka:, :], preferred_element_type=_F32)
    o_ref[...] = acc.astype(o_ref.dtype)


def _mm_cat(a, f, w, *, tm=1024, tn=1024):
    m, ka = a.shape
    kf = f.shape[1]
    k, n = w[0].shape[-2:]
    assert ka + kf == k
    tm, tn = _tile(m, tm), _tile(n, tn)
    return pl.pallas_call(
        _mm_cat_kernel,
        grid=(m // tm, n // tn),
        in_specs=[pl.BlockSpec((tm, ka), lambda i, j: (i, 0)),
                  pl.BlockSpec((tm, kf), lambda i, j: (i, 0)),
                  _weight_spec(w, (k, tn), lambda i, j: (0, j))],
        out_specs=pl.BlockSpec((tm, tn), lambda i, j: (i, j)),
        out_shape=jax.ShapeDtypeStruct((m, n), _F32),
        compiler_params=_compiler_params("parallel", "arbitrary"),
        name="matmul_cat",
    )(a, f, w[0])


_POOL_HALO = 8
_POOL_ROWS = 256


def _pool_kernel(prev_ref, x_ref, next_ref, o_ref, *, seq, group):
    ts = x_ref.shape[0]
    ext_rows = ts + 2 * _POOL_HALO
    tile = pl.program_id(0) % (seq // ts)
    t = tile * ts + lax.broadcasted_iota(jnp.int32, (ts, 1), 0)
    first, last = tile == 0, tile == seq // ts - 1
    for g, w in enumerate(_POOL_WINDOWS):
        cols = slice(g * group, (g + 1) * group)
        x = x_ref[:, cols]
        run = jnp.concatenate([jnp.where(first, 0.0, prev_ref[:, cols]), x,
                               jnp.where(last, 0.0, next_ref[:, cols])], axis=0)
        span = 1
        while span < w:
            run = run + pltpu.roll(run, ext_rows - span, 0)
            span *= 2
        total = pltpu.roll(run, w // 2, 0)[_POOL_HALO:_POOL_HALO + ts]
        lo = jnp.clip(t - w // 2, 0, seq)
        hi = jnp.clip(t - w // 2 + w, 0, seq)
        o_ref[:, cols] = (total / (hi - lo).astype(_F32) - x).astype(o_ref.dtype)


def _pool(u, seq, pool_width):
    m = u.shape[0]
    group = pool_width // len(_POOL_WINDOWS)
    assert max(_POOL_WINDOWS) // 2 <= _POOL_HALO
    ts = _tile(seq, _POOL_ROWS)
    per = ts // _POOL_HALO
    last = m // _POOL_HALO - 1
    return pl.pallas_call(
        functools.partial(_pool_kernel, seq=seq, group=group),
        grid=(m // ts,),
        in_specs=[pl.BlockSpec((_POOL_HALO, pool_width), lambda i: (jnp.maximum(i * per - 1, 0), 0)),
                  pl.BlockSpec((ts, pool_width), lambda i: (i, 0)),
                  pl.BlockSpec((_POOL_HALO, pool_width), lambda i: (jnp.minimum((i + 1) * per, last), 0))],
        out_specs=pl.BlockSpec((ts, pool_width), lambda i: (i, 0)),
        out_shape=jax.ShapeDtypeStruct((m, pool_width), _BF16),
        compiler_params=_compiler_params("parallel"),
        name="pool",
    )(u, u, u)


def _pool_proj_kernel(p_ref, w_ref, s_ref, o_ref):
    acc = jnp.dot(p_ref[...], w_ref[...], preferred_element_type=_F32)
    o_ref[...] = (acc * s_ref[...]).astype(o_ref.dtype)


def _pool_proj(pooled, w_pool, pool_scale, *, tm=1024):
    m, width = pooled.shape
    groups, group, _ = w_pool[0].shape[-3:]
    tm = _tile(m, tm)
    return pl.pallas_call(
        _pool_proj_kernel,
        grid=(m // tm, groups),
        in_specs=[pl.BlockSpec((tm, group), lambda i, g: (i, g)),
                  _weight_spec(w_pool, (None, group, group), lambda i, g: (g, 0, 0)),
                  pl.BlockSpec((1, group), lambda i, g: (0, g))],
        out_specs=pl.BlockSpec((tm, group), lambda i, g: (i, g)),
        out_shape=jax.ShapeDtypeStruct((m, width), _BF16),
        compiler_params=_compiler_params("parallel", "arbitrary"),
        name="pool_proj",
    )(pooled, w_pool[0], pool_scale.reshape(1, width))


_DFT_LO_ROWS = 64


def _dft_rows(rows, n):
    idx = (rows[:, None] * jnp.arange(n, dtype=jnp.int32)[None, :]) % n
    ang = idx.astype(_F32) * (2.0 * math.pi / n)
    return jnp.cos(ang), jnp.sin(ang)


def _dft_tables(n):
    lo_rows = _DFT_LO_ROWS if n % _DFT_LO_ROWS == 0 and n > _DFT_LO_ROWS else 1
    c_hi, s_hi = _dft_rows(jnp.arange(n // lo_rows, dtype=jnp.int32) * lo_rows, n)
    c_lo, s_lo = _dft_rows(jnp.arange(lo_rows, dtype=jnp.int32), n)
    cos = c_hi[:, None, :] * c_lo[None, :, :] - s_hi[:, None, :] * s_lo[None, :, :]
    sin = s_hi[:, None, :] * c_lo[None, :, :] + c_hi[:, None, :] * s_lo[None, :, :]
    return cos.reshape(n, n), sin.reshape(n, n)


def _seq_dft_kernel(c_ref, s_ref, xc_ref, xs_ref, o_ref, acc_ref, *, norm):
    k = pl.program_id(3)

    @pl.when(k == 0)
    def _():
        acc_ref[...] = jnp.zeros_like(acc_ref)

    acc_ref[...] += (jnp.dot(c_ref[...], xc_ref[...], preferred_element_type=_F32)
                     - jnp.dot(s_ref[...], xs_ref[...], preferred_element_type=_F32))

    @pl.when(k == pl.num_programs(3) - 1)
    def _():
        o_ref[...] = (acc_ref[...] * norm).astype(o_ref.dtype)


def _seq_dft(cos_s, sin_s, xcs, batch, seq, width, norm, *, tm=1024, tn=1024, tk=1024):
    tm, tn, tk = _tile(seq, tm), _tile(width, tn), _tile(seq, tk)
    nj = width // tn
    rows_k = seq // tk
    rows_m = seq // tm
    return pl.pallas_call(
        functools.partial(_seq_dft_kernel, norm=norm),
        grid=(batch, rows_m, nj, rows_k),
        in_specs=[pl.BlockSpec((tm, tk), lambda b, i, j, k: (i, k)),
                  pl.BlockSpec((tm, tk), lambda b, i, j, k: (i, k)),
                  pl.BlockSpec((tk, tn), lambda b, i, j, k: (b * rows_k + k, j)),
                  pl.BlockSpec((tk, tn), lambda b, i, j, k: (b * rows_k + k, nj + j))],
        out_specs=pl.BlockSpec((tm, tn), lambda b, i, j, k: (b * rows_m + i, j)),
        out_shape=jax.ShapeDtypeStruct((batch * seq, width), _BF16),
        scratch_shapes=[pltpu.VMEM((tm, tn), _F32)],
        compiler_params=_compiler_params("parallel", "parallel", "parallel", "arbitrary"),
        name="seq_dft",
    )(cos_s, sin_s, xcs, xcs)


_LOG2E = math.log2(math.e)
_RAMP_PARTS = 3
_SIGNS = (1.0, -1.0, 0.0)
_OWN = 2
_WIDEN_KEYS = 256
_BIAS_ROWS = 64


def _diff_attn_kernel(slope_ref, q_ref, k_ref, v_ref, lq1_ref, lk1_ref, lq2_ref, lk2_ref, gsub_ref,
                      o_ref, kaug_ref, bias_ref, qaug_ref, s_ref, p_ref, alpha_ref, sub_ref,
                      mx_ref, m_ref, l_ref, acc_ref, *, lambda_init, tk, qi=None):
    h = pl.program_id(1)
    qi = pl.program_id(2) if qi is None else qi
    tq, dv = q_ref.shape
    dh = dv // 2
    seq = k_ref.shape[0]
    n_chunks = seq // tk
    n_col = tk // _LANES
    n_row = tq // _SUBLANES
    assert n_chunks % 2 == 0 and tq == tk
    slope = slope_ref[h]
    c_own = (qi * tq) // tk

    @pl.when(qi == 0)
    def _():
        kb = min(tk, _WIDEN_KEYS)

        def widen(i, carry):
            start = pl.multiple_of(i * kb, kb)
            local = (i % (tk // kb)) * kb + lax.broadcasted_iota(jnp.int32, (dh, kb), 1)
            rest = slope * local.astype(_F32)
            row = lax.broadcasted_iota(jnp.int32, (dh, kb), 0)
            ext = jnp.zeros((dh, kb), _F32)
            for part in range(_RAMP_PARTS):
                term = rest.astype(_BF16).astype(_F32)
                ext = jnp.where(row == part, term, ext)
                rest = rest - term
            ext = jnp.where((row >= _RAMP_PARTS) & (row < 2 * _RAMP_PARTS), 1.0, ext)
            for half in range(2):
                k_half = k_ref[pl.ds(start, kb), half * dh:(half + 1) * dh]
                kaug_ref[half, :dh, pl.ds(start, kb)] = k_half.T
                kaug_ref[half, dh:, pl.ds(start, kb)] = ext.astype(_BF16)
            return carry

        lax.fori_loop(0, seq // kb, widen, 0)

        for r in range(0, tq, _BIAS_ROWS):
            dist = (lax.broadcasted_iota(jnp.int32, (_BIAS_ROWS, tk), 0) + r
                    - lax.broadcasted_iota(jnp.int32, (_BIAS_ROWS, tk), 1))
            bias_ref[r:r + _BIAS_ROWS] = -slope * jnp.abs(dist).astype(_F32)

    ramp_q = slope * lax.broadcasted_iota(jnp.int32, (tq, _LANES), 0).astype(_F32)
    lane_q = lax.broadcasted_iota(jnp.int32, (tq, _LANES), 1)
    ramp_q_parts, rest = [], ramp_q
    for part in range(_RAMP_PARTS):
        ramp_q_parts.append(rest.astype(_BF16).astype(_F32))
        rest = rest - ramp_q_parts[-1]
    for variant, sign in enumerate(_SIGNS):
        ext = jnp.where(lane_q < _RAMP_PARTS, sign, 0.0)
        for part in range(_RAMP_PARTS):
            ext = jnp.where(lane_q == _RAMP_PARTS + part, -sign * ramp_q_parts[part], ext)
        for half in range(2):
            qaug_ref[variant, half] = jnp.concatenate(
                [q_ref[:, half * dh:(half + 1) * dh], ext.astype(_BF16)], axis=1)

    m_ref[...] = jnp.full_like(m_ref, _NEG_BIG)
    l_ref[...] = jnp.zeros_like(l_ref)
    acc_ref[...] = jnp.zeros_like(acc_ref)

    def chunk_of(j):
        rest = jnp.where(j - 1 >= c_own, j, j - 1)
        return jnp.where(j == 0, c_own, rest)

    def group(g):
        return slice(g * _SUBLANES, (g + 1) * _SUBLANES)

    def scores(c, slot, own):
        start = pl.multiple_of(c * tk, tk)
        variant = _OWN if own else jnp.where(c < c_own, 0, 1)
        for half in range(2):
            s_ref[slot, half] = jnp.dot(qaug_ref[variant, half], kaug_ref[half, :, pl.ds(start, tk)],
                                        preferred_element_type=_F32)

    def values(c, slot):
        start = pl.multiple_of(c * tk, tk)
        v = v_ref[pl.ds(start, tk), :]
        for half in range(2):
            alpha = alpha_ref[slot, half]
            acc_ref[half] = (jnp.concatenate([alpha] * (dv // _LANES), axis=1) * acc_ref[half]
                             + jnp.dot(p_ref[slot, half], v, preferred_element_type=_F32))

    def softmax(c, slot, own):
        if own:
            kappa = 0.0
        else:
            shift = slope * (qi * tq - c * tk).astype(_F32)
            kappa = jnp.where(c < c_own, -shift, shift)

        def biased(half, g, j):
            lanes = slice(j * _LANES, (j + 1) * _LANES)
            s = s_ref[slot, half, group(g), lanes]
            return s + bias_ref[group(g), lanes] if own else s

        for half in range(2):
            if own:
                for g in range(n_row):
                    mx = biased(half, g, 0)
                    for j in range(1, n_col):
                        mx = jnp.maximum(mx, biased(half, g, j))
                    mx_ref[group(g)] = mx
                row_max = jnp.max(mx_ref[...], axis=-1, keepdims=True)
            else:
                row_max = jnp.max(s_ref[slot, half], axis=-1, keepdims=True)
            m_old = m_ref[half]
            m_new = jnp.maximum(m_old, row_max + kappa)
            alpha_ref[slot, half] = jnp.exp2(m_old - m_new)
            m_ref[half] = m_new
            sub_ref[...] = m_new - kappa
            for g in range(0, n_row, 2):
                pair_rows = slice(g * _SUBLANES, (g + 2) * _SUBLANES)
                sub = [sub_ref[group(g + i)] for i in range(2)]
                tot = [None, None]
                for j in range(n_col):
                    ps = [jnp.exp2(biased(half, g + i, j) - sub[i]) for i in range(2)]
                    tot = [p if t is None else t + p for t, p in zip(tot, ps)]
                    p_ref[slot, half, pair_rows, j * _LANES:(j + 1) * _LANES] = (
                        jnp.concatenate(ps, axis=0).astype(p_ref.dtype))
                for i in range(2):
                    l_ref[half, group(g + i)] = (alpha_ref[slot, half, group(g + i)] * l_ref[half, group(g + i)]
                                                 + tot[i])

    def phase(score_job=None, softmax_job=None, value_job=None):
        if score_job is not None:
            scores(*score_job)
        if softmax_job is not None:
            softmax(*softmax_job)
        if value_job is not None:
            values(*value_job)

    phase(score_job=(c_own, 0, True))
    phase(score_job=(chunk_of(1), 1, False), softmax_job=(c_own, 0, True))

    def pair(jj, carry):
        j = 2 * jj + 1
        phase((chunk_of(j + 1), 0, False), (chunk_of(j), 1, False), (chunk_of(j - 1), 0))
        phase((chunk_of(j + 2), 1, False), (chunk_of(j + 1), 0, False), (chunk_of(j), 1))
        return carry

    lax.fori_loop(0, (n_chunks - 2) // 2, pair, 0)
    phase(softmax_job=(chunk_of(n_chunks - 1), 1, False), value_job=(chunk_of(n_chunks - 2), 0))
    phase(value_job=(chunk_of(n_chunks - 1), 1))

    lam = (jnp.exp(jnp.sum(lq1_ref[...] * lk1_ref[...], keepdims=True))
           - jnp.exp(jnp.sum(lq2_ref[...] * lk2_ref[...], keepdims=True)) + lambda_init)
    l0 = jnp.sum(l_ref[0], axis=-1, keepdims=True)
    l1 = jnp.sum(l_ref[1], axis=-1, keepdims=True)
    o = acc_ref[0] / l0 - lam * (acc_ref[1] / l1)
    o_ref[...] = (_rms(o, gsub_ref[...]) * (1.0 - lambda_init)).astype(o_ref.dtype)


_ATTN_TILES_PER_STEP = 2
_O_REF_POS = 7


def _diff_attn_step_kernel(slope_ref, q_ref, *rest, **kw):
    tq = q_ref.shape[0] // _ATTN_TILES_PER_STEP

    def tile(t, carry):
        rows = pl.ds(pl.multiple_of(t * tq, tq), tq)
        args = list(rest)
        args[_O_REF_POS] = rest[_O_REF_POS].at[rows]
        _diff_attn_kernel(slope_ref, q_ref.at[rows], *args,
                          qi=pl.program_id(2) * _ATTN_TILES_PER_STEP + t, **kw)
        return carry

    lax.fori_loop(0, _ATTN_TILES_PER_STEP, tile, 0)


def _diff_attn(qkv, batch, seq, lq1, lk1, lq2, lk2, g_sub, lambda_init, *, tq=512, tk=512):
    m, d3 = qkv.shape
    d = d3 // 3
    heads = _ATTN_HEADS
    dv = d // heads
    dh = dv // 2
    tq, tk = _tile(seq, tq), _tile(seq, tk)
    assert tk == tq
    tiles = _ATTN_TILES_PER_STEP
    nq = seq // (tiles * tq)
    assert seq % (tiles * tq) == 0
    slopes = jnp.exp2(-8.0 * jnp.arange(1, heads + 1, dtype=_F32) / heads) * _LOG2E
    lam_spec = pl.BlockSpec((1, dh), lambda b, h, i: (0, 0))
    stat = pltpu.VMEM((tq, _LANES), _F32)
    return pl.pallas_call(
        functools.partial(_diff_attn_step_kernel, lambda_init=lambda_init, tk=tk),
        grid=(batch, heads, nq),
        in_specs=[pl.BlockSpec(memory_space=pltpu.SMEM),
                  pl.BlockSpec((tiles * tq, dv), lambda b, h, i: (b * nq + i, h)),
                  pl.BlockSpec((seq, dv), lambda b, h, i: (b, heads + h)),
                  pl.BlockSpec((seq, dv), lambda b, h, i: (b, 2 * heads + h)),
                  lam_spec, lam_spec, lam_spec, lam_spec,
                  pl.BlockSpec((1, dv), lambda b, h, i: (0, 0))],
        out_specs=pl.BlockSpec((tiles * tq, dv), lambda b, h, i: (b * nq + i, h)),
        out_shape=jax.ShapeDtypeStruct((m, d), _BF16),
        scratch_shapes=[pltpu.VMEM((2, 2 * dh, seq), _BF16),
                        pltpu.VMEM((tq, tk), _F32),
                        pltpu.VMEM((len(_SIGNS), 2, tq, 2 * dh), _BF16),
                        pltpu.VMEM((2, 2, tq, tk), _F32),
                        pltpu.VMEM((2, 2, tq, tk), _BF16),
                        pltpu.VMEM((2, 2, tq, _LANES), _F32),
                        stat, stat,
                        pltpu.VMEM((2, tq, _LANES), _F32),
                        pltpu.VMEM((2, tq, _LANES), _F32),
                        pltpu.VMEM((2, tq, dv), _F32)],
        compiler_params=_compiler_params("parallel", "parallel", "arbitrary"),
        name="diff_attn",
    )(slopes, qkv, qkv, qkv, lq1.reshape(1, dh), lk1.reshape(1, dh), lq2.reshape(1, dh),
      lk2.reshape(1, dh), g_sub.reshape(1, dv))


def _ffn(h, wg, wu, wd_f32, next_f32):
    a, wd = _gate_up(h, wg, wu, cast_srcs=(wd_f32,))
    y, *cast = _mm(a, (wd, ()), _F32, tm=1024, tn=512, cast_srcs=next_f32) if next_f32 else [
        _mm(a, (wd, ()), _F32, tm=1024, tn=512)]
    return y, [(w, ()) for w in cast]


def _pool_fourier(h, batch, seq, w_in, w_pool, pool_scale, w_fourier, w_out, dft):
    cos_s, sin_s, chan_dft = dft
    pool_width = pool_scale.shape[0]
    four_width = w_fourier[0].shape[-1]
    u = _mm(h, w_in, _F32)
    a = _pool_proj(_pool(u, seq, pool_width), w_pool, pool_scale)
    xcs = _mm(u, (chan_dft, ()), _BF16, a_col_block=(pool_width // four_width, four_width))
    head_dim = four_width // _FOURIER_HEADS
    f = _seq_dft(cos_s, sin_s, xcs, batch, seq, four_width, 1.0 / math.sqrt(seq * head_dim))
    f = _mm(f, w_fourier, _BF16)
    return _mm_cat(a, f, w_out)


def kernel(x, norm_pre, norm_post, w_ffn_gate, w_ffn_up, w_ffn_down, w_mix_in, w_pool, pool_scale,
           w_fourier, w_mix_out, w_qkv, w_attn_out, lambda_q1, lambda_k1, lambda_q2, lambda_k2,
           subln_gain):
    batch, seq, d = x.shape
    depth = norm_pre.shape[0]
    m = batch * seq
    x = x.reshape(m, d)
    bf = lambda w: w.astype(_BF16)

    four_width = w_fourier.shape[1]
    head_dim = four_width // _FOURIER_HEADS
    cos_s, sin_s = _dft_tables(seq)
    cos_c, sin_c = _dft_tables(head_dim)
    eye = jnp.eye(_FOURIER_HEADS, dtype=_F32)
    chan_dft = jnp.concatenate([jnp.kron(eye, cos_c), jnp.kron(eye, sin_c)], axis=1)
    dft = (bf(cos_s), bf(sin_s), bf(chan_dft))

    attn_head_dim = d // (2 * _ATTN_HEADS)
    w_pl, w_fr = bf(w_pool), bf(w_fourier)
    wg, wu = (bf(w_ffn_gate[0, 0]), ()), (bf(w_ffn_up[0, 0]), ())

    h = _prenorm(x, norm_pre[0, 0])
    for l in range(depth):
        i = l // 2
        mixer = ([(w_mix_in, (i,)), (w_mix_out, (i,))] if l % 2 == 0 else
                 [(w_qkv, (i,)), (w_attn_out, (i,))])
        y, (wg, wu, w_first, w_last) = _ffn(h, wg, wu, (w_ffn_down, (l, 0)),
                                            [(w_ffn_gate, (l, 1)), (w_ffn_up, (l, 1))] + mixer)
        x, h = _residual(y, x, norm_post[l, 0], norm_pre[l, 1], 0.5)
        if l % 2 == 0:
            y = _pool_fourier(h, batch, seq, w_first, (w_pl, (i,)), pool_scale[i],
                              (w_fr, (i,)), w_last, dft)
        else:
            lambda_init = 0.8 - 0.6 * math.exp(-0.3 * l)
            qkv = _mm(h, w_first, _BF16, scaled_cols=d, scale=attn_head_dim ** -0.5 * _LOG2E)
            o = _diff_attn(qkv, batch, seq, lambda_q1[i], lambda_k1[i], lambda_q2[i], lambda_k2[i],
                           subln_gain[i], lambda_init)
            y = _mm(o, w_last, _F32)
        x, h = _residual(y, x, norm_post[l, 1], norm_pre[l, 2], 1.0)
        following = [(w_ffn_gate, (l + 1, 0)), (w_ffn_up, (l + 1, 0))] if l + 1 < depth else []
        y, cast = _ffn(h, wg, wu, (w_ffn_down, (l, 1)), following)
        if cast:
            wg, wu = cast
        g_next = norm_pre[l + 1, 0] if l + 1 < depth else None
        x, h = _residual(y, x, norm_post[l, 2], g_next, 0.5)
    return x.reshape(batch, seq, d)
```

```python
import functools
import math

import jax
import jax.numpy as jnp
from jax import lax
from jax.experimental import pallas as pl
from jax.experimental.pallas import tpu as pltpu

_V7X_VMEM_LIMIT_BYTES = 56 * 1024 * 1024
_SUBLANES = 8
_LANES = 128

_EPS = 1e-6
_POOL_WINDOWS = (2, 4, 8, 16)
_FOURIER_HEADS = 4
_ATTN_HEADS = 16
_NEG_BIG = -1e30

_BF16 = jnp.bfloat16
_F32 = jnp.float32


def _compiler_params(*semantics, flags=None):
    return pltpu.CompilerParams(dimension_semantics=semantics,
                                vmem_limit_bytes=_V7X_VMEM_LIMIT_BYTES, flags=flags)


def _tile(n, want):
    t = min(n, want)
    assert n % t == 0, (n, want)
    return t


def _rms(x, g):
    return x * lax.rsqrt(jnp.mean(x * x, axis=-1, keepdims=True) + _EPS) * g


_NORM_ROWS = 256


def _prenorm_kernel(x_ref, g_ref, h_ref):
    h_ref[...] = _rms(x_ref[...], g_ref[...]).astype(h_ref.dtype)


def _prenorm(x, g):
    m, d = x.shape
    tm = _tile(m, _NORM_ROWS)
    return pl.pallas_call(
        _prenorm_kernel,
        grid=(m // tm,),
        in_specs=[pl.BlockSpec((tm, d), lambda i: (i, 0)),
                  pl.BlockSpec((1, d), lambda i: (0, 0))],
        out_specs=pl.BlockSpec((tm, d), lambda i: (i, 0)),
        out_shape=jax.ShapeDtypeStruct((m, d), _BF16),
        compiler_params=_compiler_params("parallel"),
        name="prenorm",
    )(x, g.reshape(1, d))


def _residual_kernel(y_ref, x_ref, gpost_ref, gpre_ref, xo_ref, h_ref, *, step):
    xn = x_ref[...] + step * _rms(y_ref[...], gpost_ref[...])
    xo_ref[...] = xn
    h_ref[...] = _rms(xn, gpre_ref[...]).astype(h_ref.dtype)


def _residual_last_kernel(y_ref, x_ref, gpost_ref, xo_ref, *, step):
    xo_ref[...] = x_ref[...] + step * _rms(y_ref[...], gpost_ref[...])


def _residual(y, x, g_post, g_pre_next, step):
    m, d = x.shape
    tm = _tile(m, _NORM_ROWS)
    row = pl.BlockSpec((tm, d), lambda i: (i, 0))
    vec = pl.BlockSpec((1, d), lambda i: (0, 0))
    if g_pre_next is None:
        return pl.pallas_call(
            functools.partial(_residual_last_kernel, step=step),
            grid=(m // tm,),
            in_specs=[row, row, vec],
            out_specs=row,
            out_shape=jax.ShapeDtypeStruct((m, d), _F32),
            compiler_params=_compiler_params("parallel"),
            name="residual_last",
        )(y, x, g_post.reshape(1, d)), None
    return pl.pallas_call(
        functools.partial(_residual_kernel, step=step),
        grid=(m // tm,),
        in_specs=[row, row, vec, vec],
        out_specs=[row, row],
        out_shape=[jax.ShapeDtypeStruct((m, d), _F32), jax.ShapeDtypeStruct((m, d), _BF16)],
        compiler_params=_compiler_params("parallel"),
        name="residual",
    )(y, x, g_post.reshape(1, d), g_pre_next.reshape(1, d))


def _weight_spec(w, block, index):
    lead = w[1]
    return pl.BlockSpec((None,) * len(lead) + block, lambda *g: lead + index(*g))


def _cast_specs(cast_srcs, steps, step_of):
    ins, outs, shapes, arrays = [], [], [], []
    for arr, lead in cast_srcs:
        rows, cols = arr.shape[-2:]
        slab = rows // steps
        assert rows == slab * steps and slab % (2 * _SUBLANES) == 0, (rows, steps)
        ins.append(pl.BlockSpec((None,) * len(lead) + (slab, cols),
                                lambda *g, lead=lead: lead + (step_of(*g), 0)))
        outs.append(pl.BlockSpec((slab, cols), lambda *g: (step_of(*g), 0)))
        shapes.append(jax.ShapeDtypeStruct((rows, cols), _BF16))
        arrays.append(arr)
    return ins, outs, shapes, arrays


def _cast_slabs(refs):
    n = len(refs) // 2
    for src_ref, dst_ref in zip(refs[:n], refs[n:]):
        dst_ref[...] = src_ref[...].astype(dst_ref.dtype)


def _mm_kernel(a_ref, w_ref, *refs, n_cast, scaled_col_blocks, scale):
    o_ref = refs[n_cast]
    acc = jnp.dot(a_ref[...].astype(_BF16), w_ref[...], preferred_element_type=_F32)
    if scaled_col_blocks:
        acc = acc * jnp.where(pl.program_id(1) < scaled_col_blocks, scale, 1.0)
    o_ref[...] = acc.astype(o_ref.dtype)
    _cast_slabs(refs[:n_cast] + refs[n_cast + 1:])


def _mm(a, w, out_dtype, *, tm=1024, tn=1024, a_col_block=None, scaled_cols=0, scale=1.0, cast_srcs=()):
    m = a.shape[0]
    k, n = w[0].shape[-2:]
    tm, tn = _tile(m, tm), _tile(n, tn)
    a_col = 0
    if a_col_block is not None:
        a_col, width = a_col_block
        assert width == k
    else:
        assert a.shape[1] == k
    assert scaled_cols % tn == 0
    nj = n // tn
    c_in, c_out, c_shape, c_arr = _cast_specs(cast_srcs, (m // tm) * nj, lambda i, j: i * nj + j)
    out = pl.pallas_call(
        functools.partial(_mm_kernel, n_cast=len(c_arr), scaled_col_blocks=scaled_cols // tn, scale=scale),
        grid=(m // tm, nj),
        in_specs=[pl.BlockSpec((tm, k), lambda i, j: (i, a_col)),
                  _weight_spec(w, (k, tn), lambda i, j: (0, j))] + c_in,
        out_specs=[pl.BlockSpec((tm, tn), lambda i, j: (i, j))] + c_out,
        out_shape=[jax.ShapeDtypeStruct((m, n), out_dtype)] + c_shape,
        compiler_params=_compiler_params("parallel", "arbitrary"),
        name="matmul",
    )(a, w[0], *c_arr)
    return out if c_arr else out[0]


def _gate_up_kernel(h_ref, wg_ref, wu_ref, *refs, n_cast):
    o_ref = refs[n_cast]
    h = h_ref[...]
    g = jnp.dot(h, wg_ref[...], preferred_element_type=_F32)
    u = jnp.dot(h, wu_ref[...], preferred_element_type=_F32)
    o_ref[...] = (g * jax.nn.sigmoid(g) * u).astype(o_ref.dtype)
    _cast_slabs(refs[:n_cast] + refs[n_cast + 1:])


def _gate_up(h, wg, wu, *, tm=1024, tn=768, cast_srcs=()):
    m, k = h.shape
    n = wg[0].shape[-1]
    tm, tn = _tile(m, tm), _tile(n, tn)
    nj = n // tn
    c_in, c_out, c_shape, c_arr = _cast_specs(cast_srcs, (m // tm) * nj, lambda i, j: i * nj + j)
    out = pl.pallas_call(
        functools.partial(_gate_up_kernel, n_cast=len(c_arr)),
        grid=(m // tm, nj),
        in_specs=[pl.BlockSpec((tm, k), lambda i, j: (i, 0)),
                  _weight_spec(wg, (k, tn), lambda i, j: (0, j)),
                  _weight_spec(wu, (k, tn), lambda i, j: (0, j))] + c_in,
        out_specs=[pl.BlockSpec((tm, tn), lambda i, j: (i, j))] + c_out,
        out_shape=[jax.ShapeDtypeStruct((m, n), _BF16)] + c_shape,
        compiler_params=_compiler_params("parallel", "arbitrary"),
        name="gate_up",
    )(h, wg[0], wu[0], *c_arr)
    return out if c_arr else out[0]


def _mm_cat_kernel(a_ref, f_ref, w_ref, o_ref):
    ka = a_ref.shape[1]
    acc = jnp.dot(a_ref[...], w_ref[:ka, :], preferred_element_type=_F32)
    acc = acc + jnp.dot(f_ref[...], w_ref[ka:, :], preferred_element_type=_F32)
    o_ref[...] = acc.astype(o_ref.dtype)


def _mm_cat(a, f, w, *, tm=1024, tn=1024):
    m, ka = a.shape
    kf = f.shape[1]
    k, n = w[0].shape[-2:]
    assert ka + kf == k
    tm, tn = _tile(m, tm), _tile(n, tn)
    return pl.pallas_call(
        _mm_cat_kernel,
        grid=(m // tm, n // tn),
        in_specs=[pl.BlockSpec((tm, ka), lambda i, j: (i, 0)),
                  pl.BlockSpec((tm, kf), lambda i, j: (i, 0)),
                  _weight_spec(w, (k, tn), lambda i, j: (0, j))],
        out_specs=pl.BlockSpec((tm, tn), lambda i, j: (i, j)),
        out_shape=jax.ShapeDtypeStruct((m, n), _F32),
        compiler_params=_compiler_params("parallel", "arbitrary"),
        name="matmul_cat",
    )(a, f, w[0])


_POOL_HALO = 8
_POOL_ROWS = 256


def _pool_kernel(prev_ref, x_ref, next_ref, o_ref, *, seq, group):
    ts = x_ref.shape[0]
    ext_rows = ts + 2 * _POOL_HALO
    tile = pl.program_id(0) % (seq // ts)
    t = tile * ts + lax.broadcasted_iota(jnp.int32, (ts, 1), 0)
    first, last = tile == 0, tile == seq // ts - 1
    for g, w in enumerate(_POOL_WINDOWS):
        cols = slice(g * group, (g + 1) * group)
        x = x_ref[:, cols]
        run = jnp.concatenate([jnp.where(first, 0.0, prev_ref[:, cols]), x,
                               jnp.where(last, 0.0, next_ref[:, cols])], axis=0)
        span = 1
        while span < w:
            run = run + pltpu.roll(run, ext_rows - span, 0)
            span *= 2
        total = pltpu.roll(run, w // 2, 0)[_POOL_HALO:_POOL_HALO + ts]
        lo = jnp.clip(t - w // 2, 0, seq)
        hi = jnp.clip(t - w // 2 + w, 0, seq)
        o_ref[:, cols] = (total / (hi - lo).astype(_F32) - x).astype(o_ref.dtype)


def _pool(u, seq, pool_width):
    m = u.shape[0]
    group = pool_width // len(_POOL_WINDOWS)
    assert max(_POOL_WINDOWS) // 2 <= _POOL_HALO
    ts = _tile(seq, _POOL_ROWS)
    per = ts // _POOL_HALO
    last = m // _POOL_HALO - 1
    return pl.pallas_call(
        functools.partial(_pool_kernel, seq=seq, group=group),
        grid=(m // ts,),
        in_specs=[pl.BlockSpec((_POOL_HALO, pool_width), lambda i: (jnp.maximum(i * per - 1, 0), 0)),
                  pl.BlockSpec((ts, pool_width), lambda i: (i, 0)),
                  pl.BlockSpec((_POOL_HALO, pool_width), lambda i: (jnp.minimum((i + 1) * per, last), 0))],
        out_specs=pl.BlockSpec((ts, pool_width), lambda i: (i, 0)),
        out_shape=jax.ShapeDtypeStruct((m, pool_width), _BF16),
        compiler_params=_compiler_params("parallel"),
        name="pool",
    )(u, u, u)


def _pool_proj_kernel(p_ref, w_ref, s_ref, o_ref):
    acc = jnp.dot(p_ref[...], w_ref[...], preferred_element_type=_F32)
    o_ref[...] = (acc * s_ref[...]).astype(o_ref.dtype)


def _pool_proj(pooled, w_pool, pool_scale, *, tm=1024):
    m, width = pooled.shape
    groups, group, _ = w_pool[0].shape[-3:]
    tm = _tile(m, tm)
    return pl.pallas_call(
        _pool_proj_kernel,
        grid=(m // tm, groups),
        in_specs=[pl.BlockSpec((tm, group), lambda i, g: (i, g)),
                  _weight_spec(w_pool, (None, group, group), lambda i, g: (g, 0, 0)),
                  pl.BlockSpec((1, group), lambda i, g: (0, g))],
        out_specs=pl.BlockSpec((tm, group), lambda i, g: (i, g)),
        out_shape=jax.ShapeDtypeStruct((m, width), _BF16),
        compiler_params=_compiler_params("parallel", "arbitrary"),
        name="pool_proj",
    )(pooled, w_pool[0], pool_scale.reshape(1, width))


_DFT_LO_ROWS = 64


def _dft_rows(rows, n):
    idx = (rows[:, None] * jnp.arange(n, dtype=jnp.int32)[None, :]) % n
    ang = idx.astype(_F32) * (2.0 * math.pi / n)
    return jnp.cos(ang), jnp.sin(ang)


def _dft_tables(n):
    lo_rows = _DFT_LO_ROWS if n % _DFT_LO_ROWS == 0 and n > _DFT_LO_ROWS else 1
    c_hi, s_hi = _dft_rows(jnp.arange(n // lo_rows, dtype=jnp.int32) * lo_rows, n)
    c_lo, s_lo = _dft_rows(jnp.arange(lo_rows, dtype=jnp.int32), n)
    cos = c_hi[:, None, :] * c_lo[None, :, :] - s_hi[:, None, :] * s_lo[None, :, :]
    sin = s_hi[:, None, :] * c_lo[None, :, :] + c_hi[:, None, :] * s_lo[None, :, :]
    return cos.reshape(n, n), sin.reshape(n, n)


def _seq_dft_kernel(c_ref, s_ref, xc_ref, xs_ref, o_ref, acc_ref, *, norm):
    k = pl.program_id(3)

    @pl.when(k == 0)
    def _():
        acc_ref[...] = jnp.zeros_like(acc_ref)

    acc_ref[...] += (jnp.dot(c_ref[...], xc_ref[...], preferred_element_type=_F32)
                     - jnp.dot(s_ref[...], xs_ref[...], preferred_element_type=_F32))

    @pl.when(k == pl.num_programs(3) - 1)
    def _():
        o_ref[...] = (acc_ref[...] * norm).astype(o_ref.dtype)


def _seq_dft(cos_s, sin_s, xcs, batch, seq, width, norm, *, tm=1024, tn=1024, tk=1024):
    tm, tn, tk = _tile(seq, tm), _tile(width, tn), _tile(seq, tk)
    nj = width // tn
    rows_k = seq // tk
    rows_m = seq // tm
    return pl.pallas_call(
        functools.partial(_seq_dft_kernel, norm=norm),
        grid=(batch, rows_m, nj, rows_k),
        in_specs=[pl.BlockSpec((tm, tk), lambda b, i, j, k: (i, k)),
                  pl.BlockSpec((tm, tk), lambda b, i, j, k: (i, k)),
                  pl.BlockSpec((tk, tn), lambda b, i, j, k: (b * rows_k + k, j)),
                  pl.BlockSpec((tk, tn), lambda b, i, j, k: (b * rows_k + k, nj + j))],
        out_specs=pl.BlockSpec((tm, tn), lambda b, i, j, k: (b * rows_m + i, j)),
        out_shape=jax.ShapeDtypeStruct((batch * seq, width), _BF16),
        scratch_shapes=[pltpu.VMEM((tm, tn), _F32)],
        compiler_params=_compiler_params("parallel", "parallel", "parallel", "arbitrary"),
        name="seq_dft",
    )(cos_s, sin_s, xcs, xcs)


_LOG2E = math.log2(math.e)
_RAMP_PARTS = 3
_SIGNS = (1.0, -1.0, 0.0)
_OWN = 2
_WIDEN_KEYS = 256
_BIAS_ROWS = 64


def _diff_attn_kernel(slope_ref, q_ref, k_ref, v_ref, lq1_ref, lk1_ref, lq2_ref, lk2_ref, gsub_ref,
                      o_ref, kaug_ref, bias_ref, qaug_ref, s_ref, p_ref, alpha_ref, sub_ref,
                      mx_ref, m_ref, l_ref, acc_ref, *, lambda_init, tk, qi=None):
    h = pl.program_id(1)
    qi = pl.program_id(2) if qi is None else qi
    tq, dv = q_ref.shape
    dh = dv // 2
    seq = k_ref.shape[0]
    n_chunks = seq // tk
    n_col = tk // _LANES
    n_row = tq // _SUBLANES
    assert n_chunks % 2 == 0 and tq == tk
    slope = slope_ref[h]
    c_own = (qi * tq) // tk

    @pl.when(qi == 0)
    def _():
        kb = min(tk, _WIDEN_KEYS)

        def widen(i, carry):
            start = pl.multiple_of(i * kb, kb)
            local = (i % (tk // kb)) * kb + lax.broadcasted_iota(jnp.int32, (dh, kb), 1)
            rest = slope * local.astype(_F32)
            row = lax.broadcasted_iota(jnp.int32, (dh, kb), 0)
            ext = jnp.zeros((dh, kb), _F32)
            for part in range(_RAMP_PARTS):
                term = rest.astype(_BF16).astype(_F32)
                ext = jnp.where(row == part, term, ext)
                rest = rest - term
            ext = jnp.where((row >= _RAMP_PARTS) & (row < 2 * _RAMP_PARTS), 1.0, ext)
            for half in range(2):
                k_half = k_ref[pl.ds(start, kb), half * dh:(half + 1) * dh]
                kaug_ref[half, :dh, pl.ds(start, kb)] = k_half.T
                kaug_ref[half, dh:, pl.ds(start, kb)] = ext.astype(_BF16)
            return carry

        lax.fori_loop(0, seq // kb, widen, 0)

        for r in range(0, tq, _BIAS_ROWS):
            dist = (lax.broadcasted_iota(jnp.int32, (_BIAS_ROWS, tk), 0) + r
                    - lax.broadcasted_iota(jnp.int32, (_BIAS_ROWS, tk), 1))
            bias_ref[r:r + _BIAS_ROWS] = -slope * jnp.abs(dist).astype(_F32)

    ramp_q = slope * lax.broadcasted_iota(jnp.int32, (tq, _LANES), 0).astype(_F32)
    lane_q = lax.broadcasted_iota(jnp.int32, (tq, _LANES), 1)
    ramp_q_parts, rest = [], ramp_q
    for part in range(_RAMP_PARTS):
        ramp_q_parts.append(rest.astype(_BF16).astype(_F32))
        rest = rest - ramp_q_parts[-1]
    for variant, sign in enumerate(_SIGNS):
        ext = jnp.where(lane_q < _RAMP_PARTS, sign, 0.0)
        for part in range(_RAMP_PARTS):
            ext = jnp.where(lane_q == _RAMP_PARTS + part, -sign * ramp_q_parts[part], ext)
        for half in range(2):
            qaug_ref[variant, half] = jnp.concatenate(
                [q_ref[:, half * dh:(half + 1) * dh], ext.astype(_BF16)], axis=1)

    m_ref[...] = jnp.full_like(m_ref, _NEG_BIG)
    l_ref[...] = jnp.zeros_like(l_ref)
    acc_ref[...] = jnp.zeros_like(acc_ref)

    def chunk_of(j):
        rest = jnp.where(j - 1 >= c_own, j, j - 1)
        return jnp.where(j == 0, c_own, rest)

    def group(g):
        return slice(g * _SUBLANES, (g + 1) * _SUBLANES)

    def scores(c, slot, own):
        start = pl.multiple_of(c * tk, tk)
        variant = _OWN if own else jnp.where(c < c_own, 0, 1)
        for half in range(2):
            s_ref[slot, half] = jnp.dot(qaug_ref[variant, half], kaug_ref[half, :, pl.ds(start, tk)],
                                        preferred_element_type=_F32)

    def values(c, slot):
        start = pl.multiple_of(c * tk, tk)
        v = v_ref[pl.ds(start, tk), :]
        for half in range(2):
            alpha = alpha_ref[slot, half]
            acc_ref[half] = (jnp.concatenate([alpha] * (dv // _LANES), axis=1) * acc_ref[half]
                             + jnp.dot(p_ref[slot, half], v, preferred_element_type=_F32))

    def softmax(c, slot, own):
        if own:
            kappa = 0.0
        else:
            shift = slope * (qi * tq - c * tk).astype(_F32)
            kappa = jnp.where(c < c_own, -shift, shift)

        def biased(half, g, j):
            lanes = slice(j * _LANES, (j + 1) * _LANES)
            s = s_ref[slot, half, group(g), lanes]
            return s + bias_ref[group(g), lanes] if own else s

        for half in range(2):
            if own:
                for g in range(n_row):
                    mx = biased(half, g, 0)
                    for j in range(1, n_col):
                        mx = jnp.maximum(mx, biased(half, g, j))
                    mx_ref[group(g)] = mx
                row_max = jnp.max(mx_ref[...], axis=-1, keepdims=True)
            else:
                row_max = jnp.max(s_ref[slot, half], axis=-1, keepdims=True)
            m_old = m_ref[half]
            m_new = jnp.maximum(m_old, row_max + kappa)
            alpha_ref[slot, half] = jnp.exp2(m_old - m_new)
            m_ref[half] = m_new
            sub_ref[...] = m_new - kappa
            for g in range(0, n_row, 2):
                pair_rows = slice(g * _SUBLANES, (g + 2) * _SUBLANES)
                sub = [sub_ref[group(g + i)] for i in range(2)]
                tot = [None, None]
                for j in range(n_col):
                    ps = [jnp.exp2(biased(half, g + i, j) - sub[i]) for i in range(2)]
                    tot = [p if t is None else t + p for t, p in zip(tot, ps)]
                    p_ref[slot, half, pair_rows, j * _LANES:(j + 1) * _LANES] = (
                        jnp.concatenate(ps, axis=0).astype(p_ref.dtype))
                for i in range(2):
                    l_ref[half, group(g + i)] = (alpha_ref[slot, half, group(g + i)] * l_ref[half, group(g + i)]
                                                 + tot[i])

    def phase(score_job=None, softmax_job=None, value_job=None):
        if score_job is not None:
            scores(*score_job)
        if softmax_job is not None:
            softmax(*softmax_job)
        if value_job is not None:
            values(*value_job)

    phase(score_job=(c_own, 0, True))
    phase(score_job=(chunk_of(1), 1, False), softmax_job=(c_own, 0, True))

    def pair(jj, carry):
        j = 2 * jj + 1
        phase((chunk_of(j + 1), 0, False), (chunk_of(j), 1, False), (chunk_of(j - 1), 0))
        phase((chunk_of(j + 2), 1, False), (chunk_of(j + 1), 0, False), (chunk_of(j), 1))
        return carry

    lax.fori_loop(0, (n_chunks - 2) // 2, pair, 0)
    phase(softmax_job=(chunk_of(n_chunks - 1), 1, False), value_job=(chunk_of(n_chunks - 2), 0))
    phase(value_job=(chunk_of(n_chunks - 1), 1))

    lam = (jnp.exp(jnp.sum(lq1_ref[...] * lk1_ref[...], keepdims=True))
           - jnp.exp(jnp.sum(lq2_ref[...] * lk2_ref[...], keepdims=True)) + lambda_init)
    l0 = jnp.sum(l_ref[0], axis=-1, keepdims=True)
    l1 = jnp.sum(l_ref[1], axis=-1, keepdims=True)
    o = acc_ref[0] / l0 - lam * (acc_ref[1] / l1)
    o_ref[...] = (_rms(o, gsub_ref[...]) * (1.0 - lambda_init)).astype(o_ref.dtype)


_ATTN_TILES_PER_STEP = 4
_O_REF_POS = 7


def _diff_attn_step_kernel(slope_ref, q_ref, *rest, **kw):
    tq = q_ref.shape[0] // _ATTN_TILES_PER_STEP

    def tile(t, carry):
        rows = pl.ds(pl.multiple_of(t * tq, tq), tq)
        args = list(rest)
        args[_O_REF_POS] = rest[_O_REF_POS].at[rows]
        _diff_attn_kernel(slope_ref, q_ref.at[rows], *args,
                          qi=pl.program_id(2) * _ATTN_TILES_PER_STEP + t, **kw)
        return carry

    lax.fori_loop(0, _ATTN_TILES_PER_STEP, tile, 0)


def _diff_attn(qkv, batch, seq, lq1, lk1, lq2, lk2, g_sub, lambda_init, *, tq=512, tk=512):
    m, d3 = qkv.shape
    d = d3 // 3
    heads = _ATTN_HEADS
    dv = d // heads
    dh = dv // 2
    tq, tk = _tile(seq, tq), _tile(seq, tk)
    assert tk == tq
    tiles = _ATTN_TILES_PER_STEP
    nq = seq // (tiles * tq)
    assert seq % (tiles * tq) == 0
    slopes = jnp.exp2(-8.0 * jnp.arange(1, heads + 1, dtype=_F32) / heads) * _LOG2E
    lam_spec = pl.BlockSpec((1, dh), lambda b, h, i: (0, 0))
    stat = pltpu.VMEM((tq, _LANES), _F32)
    return pl.pallas_call(
        functools.partial(_diff_attn_step_kernel, lambda_init=lambda_init, tk=tk),
        grid=(batch, heads, nq),
        in_specs=[pl.BlockSpec(memory_space=pltpu.SMEM),
                  pl.BlockSpec((tiles * tq, dv), lambda b, h, i: (b * nq + i, h)),
                  pl.BlockSpec((seq, dv), lambda b, h, i: (b, heads + h)),
                  pl.BlockSpec((seq, dv), lambda b, h, i: (b, 2 * heads + h)),
                  lam_spec, lam_spec, lam_spec, lam_spec,
                  pl.BlockSpec((1, dv), lambda b, h, i: (0, 0))],
        out_specs=pl.BlockSpec((tiles * tq, dv), lambda b, h, i: (b * nq + i, h)),
        out_shape=jax.ShapeDtypeStruct((m, d), _BF16),
        scratch_shapes=[pltpu.VMEM((2, 2 * dh, seq), _BF16),
                        pltpu.VMEM((tq, tk), _F32),
                        pltpu.VMEM((len(_SIGNS), 2, tq, 2 * dh), _BF16),
                        pltpu.VMEM((2, 2, tq, tk), _F32),
                        pltpu.VMEM((2, 2, tq, tk), _BF16),
                        pltpu.VMEM((2, 2, tq, _LANES), _F32),
                        stat, stat,
                        pltpu.VMEM((2, tq, _LANES), _F32),
                        pltpu.VMEM((2, tq, _LANES), _F32),
                        pltpu.VMEM((2, tq, dv), _F32)],
        compiler_params=_compiler_params("parallel", "parallel", "arbitrary"),
        name="diff_attn",
    )(slopes, qkv, qkv, qkv, lq1.reshape(1, dh), lk1.reshape(1, dh), lq2.reshape(1, dh),
      lk2.reshape(1, dh), g_sub.reshape(1, dv))


def _ffn(h, wg, wu, wd_f32, next_f32):
    a, wd = _gate_up(h, wg, wu, cast_srcs=(wd_f32,))
    y, *cast = _mm(a, (wd, ()), _F32, tm=1024, tn=512, cast_srcs=next_f32) if next_f32 else [
        _mm(a, (wd, ()), _F32, tm=1024, tn=512)]
    return y, [(w, ()) for w in cast]


def _pool_fourier(h, batch, seq, w_in, w_pool, pool_scale, w_fourier, w_out, dft):
    cos_s, sin_s, chan_dft = dft
    pool_width = pool_scale.shape[0]
    four_width = w_fourier[0].shape[-1]
    u = _mm(h, w_in, _F32)
    a = _pool_proj(_pool(u, seq, pool_width), w_pool, pool_scale)
    xcs = _mm(u, (chan_dft, ()), _BF16, a_col_block=(pool_width // four_width, four_width))
    head_dim = four_width // _FOURIER_HEADS
    f = _seq_dft(cos_s, sin_s, xcs, batch, seq, four_width, 1.0 / math.sqrt(seq * head_dim))
    f = _mm(f, w_fourier, _BF16)
    return _mm_cat(a, f, w_out)


def kernel(x, norm_pre, norm_post, w_ffn_gate, w_ffn_up, w_ffn_down, w_mix_in, w_pool, pool_scale,
           w_fourier, w_mix_out, w_qkv, w_attn_out, lambda_q1, lambda_k1, lambda_q2, lambda_k2,
           subln_gain):
    batch, seq, d = x.shape
    depth = norm_pre.shape[0]
    m = batch * seq
    x = x.reshape(m, d)
    bf = lambda w: w.astype(_BF16)

    four_width = w_fourier.shape[1]
    head_dim = four_width // _FOURIER_HEADS
    cos_s, sin_s = _dft_tables(seq)
    cos_c, sin_c = _dft_tables(head_dim)
    eye = jnp.eye(_FOURIER_HEADS, dtype=_F32)
    chan_dft = jnp.concatenate([jnp.kron(eye, cos_c), jnp.kron(eye, sin_c)], axis=1)
    dft = (bf(cos_s), bf(sin_s), bf(chan_dft))

    attn_head_dim = d // (2 * _ATTN_HEADS)
    w_pl, w_fr = bf(w_pool), bf(w_fourier)
    wg, wu = (bf(w_ffn_gate[0, 0]), ()), (bf(w_ffn_up[0, 0]), ())

    h = _prenorm(x, norm_pre[0, 0])
    for l in range(depth):
        i = l // 2
        mixer = ([(w_mix_in, (i,)), (w_mix_out, (i,))] if l % 2 == 0 else
                 [(w_qkv, (i,)), (w_attn_out, (i,))])
        y, (wg, wu, w_first, w_last) = _ffn(h, wg, wu, (w_ffn_down, (l, 0)),
                                            [(w_ffn_gate, (l, 1)), (w_ffn_up, (l, 1))] + mixer)
        x, h = _residual(y, x, norm_post[l, 0], norm_pre[l, 1], 0.5)
        if l % 2 == 0:
            y = _pool_fourier(h, batch, seq, w_first, (w_pl, (i,)), pool_scale[i],
                              (w_fr, (i,)), w_last, dft)
        else:
            lambda_init = 0.8 - 0.6 * math.exp(-0.3 * l)
            qkv = _mm(h, w_first, _BF16, scaled_cols=d, scale=attn_head_dim ** -0.5 * _LOG2E)
            o = _diff_attn(qkv, batch, seq, lambda_q1[i], lambda_k1[i], lambda_q2[i], lambda_k2[i],
                           subln_gain[i], lambda_init)
            y = _mm(o, w_last, _F32)
        x, h = _residual(y, x, norm_post[l, 1], norm_pre[l, 2], 1.0)
        following = [(w_ffn_gate, (l + 1, 0)), (w_ffn_up, (l + 1, 0))] if l + 1 < depth else []
        y, cast = _ffn(h, wg, wu, (w_ffn_down, (l, 1)), following)
        if cast:
            wg, wu = cast
        g_next = norm_pre[l + 1, 0] if l + 1 < depth else None
        x, h = _residual(y, x, norm_post[l, 2], g_next, 0.5)
    return x.reshape(batch, seq, d)
```
